```python
import math
import jax
import jax.numpy as jnp
from jax import lax
import numpy as np

D_MODEL = 2048
BATCH = 2
SEQ = 16384
DEPTH = 4

GRID_W = 64
CTX_LEN = 256
N_MOD = 9
FFN_HIDDEN = 5632
RET_HEADS = 4
RET_WIDTH = D_MODEL // 2
RET_HEAD_DIM = RET_WIDTH // RET_HEADS
RET_CHUNK = 128
POOL_WINDOWS = (2, 4, 8, 16)
POOL_GROUPS = len(POOL_WINDOWS)
POOL_WIDTH = D_MODEL - RET_WIDTH
POOL_GROUP_DIM = POOL_WIDTH // POOL_GROUPS
EVEN_IN = 4 * RET_WIDTH + POOL_WIDTH
EVEN_MIX = RET_WIDTH + POOL_WIDTH
DIFF_HEADS = 8
DIFF_HEAD_DIM = D_MODEL // DIFF_HEADS // 2
DIFF_V_DIM = 2 * DIFF_HEAD_DIM
DIFF_QK_WIDTH = DIFF_HEADS * 2 * DIFF_HEAD_DIM
DIFF_V_WIDTH = DIFF_HEADS * DIFF_V_DIM
ODD_IN = 2 * DIFF_QK_WIDTH + DIFF_V_WIDTH
Q_BLOCK = 128
ROPE_BASE = 10000.0
N_EVEN = (DEPTH + 1) // 2
N_ODD = DEPTH // 2

kernel_name = 'hybrid_retention_pool_diffattn_dit_trunk'


def _rmsnorm(x, g, eps=1e-6):
    xf = x.astype(jnp.float32)
    y = xf * lax.rsqrt(jnp.mean(xf * xf, axis=-1, keepdims=True) + eps)
    return y.astype(x.dtype) * g


def _head_rms(x, eps):
    xf = x.astype(jnp.float32)
    return xf * lax.rsqrt(jnp.mean(xf * xf, axis=-1, keepdims=True) + eps)


def _modulated_norm(h, g, shift, scale):
    return _rmsnorm(h, g) * (1.0 + scale) + shift


def _residual_add(h, y, g, gate, weight):
    return h + weight * gate * _rmsnorm(y, g)


def _swiglu(u, wg, wu, wd):
    return (jax.nn.silu(u @ wg) * (u @ wu)) @ wd


def _ffn_sublayer(h, shift, scale, gate, g_pre, g_post, wg, wu, wd):
    y = _swiglu(_modulated_norm(h, g_pre, shift, scale), wg, wu, wd)
    return _residual_add(h, y, g_post, gate, 0.5)


def _axial_rope(x, rows, cols):
    dh = x.shape[-1]
    half = dh // 2
    nf = half // 2
    inv = ROPE_BASE ** (-jnp.arange(nf, dtype=jnp.float32) / nf)
    shape = (1, x.shape[1]) + (1,) * (x.ndim - 3) + (nf,)

    def rot(xa, pos):
        ang = pos[:, None] * inv[None, :]
        cos = jnp.cos(ang).reshape(shape).astype(x.dtype)
        sin = jnp.sin(ang).reshape(shape).astype(x.dtype)
        x1, x2 = xa[..., :nf], xa[..., nf:]
        return jnp.concatenate([x1 * cos - x2 * sin, x1 * sin + x2 * cos], axis=-1)

    return jnp.concatenate([rot(x[..., :half], rows), rot(x[..., half:], cols)], axis=-1)


def _retention_scan(q, k, v, log_gamma, s0, strict):
    b, h, L, dk = q.shape
    dv = v.shape[-1]
    n = L // RET_CHUNK
    qc = q.reshape(b, h, n, RET_CHUNK, dk)
    kc = k.reshape(b, h, n, RET_CHUNK, dk)
    vc = v.reshape(b, h, n, RET_CHUNK, dv)
    idx = jnp.arange(RET_CHUNK, dtype=jnp.float32)
    dist = idx[:, None] - idx[None, :]
    keep = (dist > 0) if strict else (dist >= 0)
    lg = log_gamma.astype(jnp.float32)
    d_intra = jnp.where(keep[None], jnp.exp(lg[:, None, None] * jnp.maximum(dist, 0.0)[None]), 0.0)
    scores = jnp.einsum('bhnid,bhnjd->bhnij', qc, kc) * d_intra[None, :, None]
    intra = jnp.einsum('bhnij,bhnje->bhnie', scores, vc)
    k_dec = kc * jnp.exp(lg[:, None] * (RET_CHUNK - 1.0 - idx)[None])[None, :, None, :, None]
    chunk_states = jnp.einsum('bhnjd,bhnje->nbhde', k_dec, vc)
    g_chunk = jnp.exp(lg * RET_CHUNK)[None, :, None, None]

    def step(s, cs):
        return g_chunk * s + cs, s

    _, s_prev = lax.scan(step, s0, chunk_states)
    q_dec = qc * jnp.exp(lg[:, None] * (idx + 1.0)[None])[None, :, None, :, None]
    cross = jnp.einsum('bhnid,nbhde->bhnie', q_dec, s_prev)
    return (intra + cross).reshape(b, h, L, dv)


def _bidir_retention(q, k, v, lg_f, lg_b, s_f, s_b):
    flip = lambda t: jnp.flip(t, axis=2)
    fwd = _retention_scan(q, k, v, lg_f, s_f, False)
    bwd = flip(_retention_scan(flip(q), flip(k), flip(v), lg_b, s_b, True))
    return fwd + bwd


def _context_states(k, v, lg_f, lg_b):
    L = k.shape[2]
    pos = jnp.arange(L, dtype=jnp.float32)
    w_f = jnp.exp(lg_f[:, None] * (L - 1.0 - pos)[None])
    w_b = jnp.exp(lg_b[:, None] * pos[None])
    s_f = jnp.einsum('bhmd,hm,bhme->bhde', k, w_f, v)
    s_b = jnp.einsum('bhmd,hm,bhme->bhde', k, w_b, v)
    return s_f, s_b


def _multiscale_pool(p, pool_w, pool_scale):
    b, L, _ = p.shape
    pg = p.astype(jnp.float32).reshape(b, L, POOL_GROUPS, POOL_GROUP_DIM)
    cs = jnp.concatenate([jnp.zeros_like(pg[:, :1]), jnp.cumsum(pg, axis=1)], axis=1)
    t = jnp.arange(L)
    outs = []
    for gi, w in enumerate(POOL_WINDOWS):
        lo = jnp.clip(t - w // 2, 0, L)
        hi = jnp.clip(t + w // 2, 0, L)
        mean = (cs[:, hi, gi] - cs[:, lo, gi]) / (hi - lo).astype(jnp.float32)[None, :, None]
        outs.append(mean - pg[:, :, gi])
    pooled = jnp.stack(outs, axis=2).astype(p.dtype)
    mixed = jnp.einsum('blgc,gcd->blgd', pooled, pool_w)
    return mixed.reshape(b, L, POOL_WIDTH) * pool_scale


def _retention_pool_project(u, w_in, rows, cols, rope):
    b, L, _ = u.shape
    q, k, v, g, p = jnp.split(u @ w_in, [RET_WIDTH, 2 * RET_WIDTH, 3 * RET_WIDTH, 4 * RET_WIDTH], axis=-1)
    q = q.reshape(b, L, RET_HEADS, RET_HEAD_DIM)
    k = k.reshape(b, L, RET_HEADS, RET_HEAD_DIM)
    v = v.reshape(b, L, RET_HEADS, RET_HEAD_DIM)
    if rope:
        q = _axial_rope(q, rows, cols)
        k = _axial_rope(k, rows, cols)
    k = k * RET_HEAD_DIM ** -0.5
    to_heads = lambda t: jnp.transpose(t, (0, 2, 1, 3)).astype(jnp.float32)
    return to_heads(q), to_heads(k), to_heads(v), g, p


def _retention_pool_merge(ret, gate, p, pool_w, pool_scale, w_out):
    b, _, L, _ = ret.shape
    y = jnp.transpose(_head_rms(ret, 1e-6), (0, 2, 1, 3)).reshape(b, L, RET_WIDTH).astype(gate.dtype)
    y = y * jax.nn.silu(gate)
    return jnp.concatenate([y, _multiscale_pool(p, pool_w, pool_scale)], axis=-1) @ w_out


def _retention_pool_mixer(u_ctx, u_lat, w_in, w_out, ret_decay, pool_w, pool_scale, rows, cols, need_ctx):
    lg = jax.nn.log_sigmoid(ret_decay.astype(jnp.float32))
    lg_f, lg_b = lg[0], lg[1]
    qc, kc, vc, gc, pc = _retention_pool_project(u_ctx, w_in, None, None, False)
    ql, kl, vl, gl, pl = _retention_pool_project(u_lat, w_in, rows, cols, True)
    s_f, s_b = _context_states(kc, vc, lg_f, lg_b)
    ret_lat = _bidir_retention(ql, kl, vl, lg_f, lg_b, s_f, s_b)
    y_lat = _retention_pool_merge(ret_lat, gl, pl, pool_w, pool_scale, w_out)
    y_ctx = None
    if need_ctx:
        zeros = jnp.zeros_like(s_f)
        ret_ctx = _bidir_retention(qc, kc, vc, lg_f, lg_b, zeros, zeros)
        y_ctx = _retention_pool_merge(ret_ctx, gc, pc, pool_w, pool_scale, w_out)
    return y_ctx, y_lat


def _diff_attend(qb, kk, vv, lam):
    s = jnp.einsum('bqhmd,bkhmd->bhmqk', qb, kk).astype(jnp.float32)
    a = jax.nn.softmax(s, axis=-1)
    a = a[:, :, 0] - lam * a[:, :, 1]
    return jnp.einsum('bhqk,bkhe->bqhe', a.astype(vv.dtype), vv)


def _diff_attention_mixer(u_ctx, u_lat, w_in, w_out, lq1, lk1, lq2, lk2, subln_g, lam_init, rows, cols, need_ctx):
    lam = jnp.exp(jnp.sum(lq1 * lk1)) - jnp.exp(jnp.sum(lq2 * lk2)) + lam_init
    b, S, _ = u_lat.shape
    Lc = u_ctx.shape[1]
    scale = DIFF_HEAD_DIM ** -0.5
    pl = u_lat @ w_in
    q_l = _axial_rope(pl[..., :DIFF_QK_WIDTH].reshape(b, S, DIFF_HEADS, 2, DIFF_HEAD_DIM), rows, cols) * scale
    k_l = _axial_rope(pl[..., DIFF_QK_WIDTH:2 * DIFF_QK_WIDTH].reshape(b, S, DIFF_HEADS, 2, DIFF_HEAD_DIM), rows, cols)
    v_l = pl[..., 2 * DIFF_QK_WIDTH:].reshape(b, S, DIFF_HEADS, DIFF_V_DIM)
    pc = u_ctx @ w_in[:, DIFF_QK_WIDTH:]
    k_c = pc[..., :DIFF_QK_WIDTH].reshape(b, Lc, DIFF_HEADS, 2, DIFF_HEAD_DIM)
    v_c = pc[..., DIFF_QK_WIDTH:].reshape(b, Lc, DIFF_HEADS, DIFF_V_DIM)
    k_all = jnp.concatenate([k_c, k_l], axis=1)
    v_all = jnp.concatenate([v_c, v_l], axis=1)
    nb = S // Q_BLOCK
    qb = jnp.moveaxis(q_l.reshape(b, nb, Q_BLOCK, DIFF_HEADS, 2, DIFF_HEAD_DIM), 1, 0)
    o_l = lax.map(lambda blk: _diff_attend(blk, k_all, v_all, lam), qb)
    o_l = jnp.moveaxis(o_l, 0, 1).reshape(b, S, DIFF_HEADS, DIFF_V_DIM)

    def finish(o):
        o = (_head_rms(o, 1e-5) * subln_g * (1.0 - lam_init)).astype(u_lat.dtype)
        return o.reshape(o.shape[0], o.shape[1], DIFF_V_WIDTH) @ w_out

    y_lat = finish(o_l)
    y_ctx = None
    if need_ctx:
        q_c = (u_ctx @ w_in[:, :DIFF_QK_WIDTH]).reshape(b, Lc, DIFF_HEADS, 2, DIFF_HEAD_DIM) * scale
        y_ctx = finish(_diff_attend(q_c, k_c, v_c, lam))
    return y_ctx, y_lat


def setup_inputs(seed: int = 0) -> dict:
    key = jax.random.key(seed)
    ks = jax.random.split(key, 24)
    f32 = jnp.float32
    D = D_MODEL

    def nrm(k, shape, s):
        return jax.random.normal(k, shape, f32) * s

    eps = 2.0 ** (-5.0 - np.arange(RET_HEADS))
    decay_logit = jnp.asarray(np.log((1.0 - eps) / eps).astype(np.float32))
    return {
        'x': nrm(ks[0], (BATCH, SEQ, D), 1.0),
        'c': nrm(ks[1], (BATCH, D), 1.0),
        'ctx': nrm(ks[2], (BATCH, CTX_LEN, D), 1.0),
        'c_ctx': nrm(ks[3], (D,), 1.0),
        'mod_w': nrm(ks[4], (DEPTH, D, N_MOD * D), 0.5 * D ** -0.5),
        'mod_b': nrm(ks[5], (DEPTH, N_MOD * D), 0.01),
        'norm_g': 1.0 + nrm(ks[6], (DEPTH, 6, D), 0.05),
        'ffn_w_gate': nrm(ks[7], (DEPTH, 2, D, FFN_HIDDEN), D ** -0.5),
        'ffn_w_up': nrm(ks[8], (DEPTH, 2, D, FFN_HIDDEN), D ** -0.5),
        'ffn_w_down': nrm(ks[9], (DEPTH, 2, FFN_HIDDEN, D), FFN_HIDDEN ** -0.5),
        'ev_w_in': nrm(ks[10], (N_EVEN, D, EVEN_IN), D ** -0.5),
        'ev_w_out': nrm(ks[11], (N_EVEN, EVEN_MIX, D), EVEN_MIX ** -0.5),
        'ret_decay': decay_logit[None, None, :] + nrm(ks[12], (N_EVEN, 2, RET_HEADS), 0.1),
        'pool_w': nrm(ks[13], (N_EVEN, POOL_GROUPS, POOL_GROUP_DIM, POOL_GROUP_DIM), POOL_GROUP_DIM ** -0.5),
        'pool_scale': 1.0 + nrm(ks[14], (N_EVEN, POOL_WIDTH), 0.1),
        'od_w_in': nrm(ks[15], (N_ODD, D, ODD_IN), D ** -0.5),
        'od_w_out': nrm(ks[16], (N_ODD, DIFF_V_WIDTH, D), DIFF_V_WIDTH ** -0.5),
        'lam_q1': nrm(ks[17], (N_ODD, DIFF_HEAD_DIM), 0.1),
        'lam_k1': nrm(ks[18], (N_ODD, DIFF_HEAD_DIM), 0.1),
        'lam_q2': nrm(ks[19], (N_ODD, DIFF_HEAD_DIM), 0.1),
        'lam_k2': nrm(ks[20], (N_ODD, DIFF_HEAD_DIM), 0.1),
        'diff_subln_g': 1.0 + nrm(ks[21], (N_ODD, DIFF_V_DIM), 0.05),
    }


def reference(x, c, ctx, c_ctx, mod_w, mod_b, norm_g, ffn_w_gate, ffn_w_up, ffn_w_down,
              ev_w_in, ev_w_out, ret_decay, pool_w, pool_scale, od_w_in, od_w_out,
              lam_q1, lam_k1, lam_q2, lam_k2, diff_subln_g):
    b, S, D = x.shape
    n_rows = S // GRID_W
    rows = jnp.repeat(jnp.arange(n_rows, dtype=jnp.float32), GRID_W)
    cols = jnp.tile(jnp.arange(GRID_W, dtype=jnp.float32), n_rows)
    silu_c = jax.nn.silu(c)
    silu_cc = jax.nn.silu(c_ctx)
    h_lat, h_ctx = x, ctx
    for l in range(DEPTH):
        need_ctx = l < DEPTH - 1
        m_lat = (silu_c @ mod_w[l] + mod_b[l]).reshape(b, N_MOD, 1, D)
        m_ctx = (silu_cc @ mod_w[l] + mod_b[l]).reshape(N_MOD, 1, D)
        g = norm_g[l]
        h_lat = _ffn_sublayer(h_lat, m_lat[:, 0], m_lat[:, 1], m_lat[:, 2], g[0], g[1],
                              ffn_w_gate[l, 0], ffn_w_up[l, 0], ffn_w_down[l, 0])
        h_ctx = _ffn_sublayer(h_ctx, m_ctx[0], m_ctx[1], m_ctx[2], g[0], g[1],
                              ffn_w_gate[l, 0], ffn_w_up[l, 0], ffn_w_down[l, 0])
        u_lat = _modulated_norm(h_lat, g[2], m_lat[:, 3], m_lat[:, 4])
        u_ctx = _modulated_norm(h_ctx, g[2], m_ctx[3], m_ctx[4])
        if l % 2 == 0:
            e = l // 2
            y_ctx, y_lat = _retention_pool_mixer(u_ctx, u_lat, ev_w_in[e], ev_w_out[e], ret_decay[e],
                                                 pool_w[e], pool_scale[e], rows, cols, need_ctx)
        else:
            o = l // 2
            lam_init = 0.8 - 0.6 * math.exp(-0.3 * l)
            y_ctx, y_lat = _diff_attention_mixer(u_ctx, u_lat, od_w_in[o], od_w_out[o], lam_q1[o], lam_k1[o],
                                                 lam_q2[o], lam_k2[o], diff_subln_g[o], lam_init,
                                                 rows, cols, need_ctx)
        h_lat = _residual_add(h_lat, y_lat, g[3], m_lat[:, 5], 1.0)
        h_lat = _ffn_sublayer(h_lat, m_lat[:, 6], m_lat[:, 7], m_lat[:, 8], g[4], g[5],
                              ffn_w_gate[l, 1], ffn_w_up[l, 1], ffn_w_down[l, 1])
        if need_ctx:
            h_ctx = _residual_add(h_ctx, y_ctx, g[3], m_ctx[5], 1.0)
            h_ctx = _ffn_sublayer(h_ctx, m_ctx[6], m_ctx[7], m_ctx[8], g[4], g[5],
                                  ffn_w_gate[l, 1], ffn_w_up[l, 1], ffn_w_down[l, 1])
    return h_lat
```

```python
import functools
import math

import jax
import jax.numpy as jnp
from jax import lax
from jax.experimental import pallas as pl
from jax.experimental.pallas import tpu as pltpu

F32 = jnp.float32
BF16 = jnp.bfloat16

GRID_W = 64
ROPE_BASE = 10000.0
N_MOD = 9
RET_HEADS = 4
POOL_WINDOWS = (2, 4, 8, 16)
DIFF_HEADS = 8
NORM_EPS = 1e-6
SUBLN_EPS = 1e-5

LANES = 128
ROW_TILE = 512
FFN_CHUNK = 512
PROJ_CHUNK = 512
MOD_CHUNK = 1024
ATTN_Q_TILE = 512
ATTN_K_TILE = 512
POOL_TILE = 256
POOL_HALO = 16
VMEM_LIMIT = 56 * 1024 * 1024


def _params(*sem):
    return pltpu.CompilerParams(dimension_semantics=sem, vmem_limit_bytes=VMEM_LIMIT)


def _silu(x):
    return x * jax.nn.sigmoid(x)


def _rms(x, g, eps=NORM_EPS):
    return x * lax.rsqrt(jnp.mean(x * x, axis=-1, keepdims=True) + eps) * g


def _dot(a, b):
    return jnp.dot(a, b, preferred_element_type=F32)


def _dot_nt(a, b):
    return lax.dot_general(a, b, (((1,), (1,)), ((), ())), preferred_element_type=F32)


def _dot_tn(a, b):
    return lax.dot_general(a, b, (((0,), (0,)), ((), ())), preferred_element_type=F32)


def _mod_kernel(a_ref, w_ref, b_ref, o_ref):
    s = _silu(a_ref[...]).astype(BF16)
    o_ref[...] = _dot(s, w_ref[...].astype(BF16)) + b_ref[...]


def _modulation(cond, mod_w, mod_b):
    depth, d, n = mod_w.shape
    tn = MOD_CHUNK
    return pl.pallas_call(
        _mod_kernel,
        grid=(depth, n // tn),
        in_specs=[
            pl.BlockSpec((8, d), lambda l, j: (0, 0)),
            pl.BlockSpec((None, d, tn), lambda l, j: (l, 0, j)),
            pl.BlockSpec((None, 1, tn), lambda l, j: (l, 0, j)),
        ],
        out_specs=pl.BlockSpec((None, 8, tn), lambda l, j: (l, 0, j)),
        out_shape=jax.ShapeDtypeStruct((depth, 8, n), F32),
        compiler_params=_params("parallel", "parallel"),
        name="adaln_modulation",
    )(cond, mod_w, mod_b.reshape(depth, 1, n))


def _ffn_kernel(h_ref, m_ref, g_ref, wg_ref, wu_ref, wd_ref, o_ref, u_ref, acc_ref, *, mod0, g0):
    k = pl.program_id(1)

    @pl.when(k == 0)
    def _():
        u = _rms(h_ref[...], g_ref[g0:g0 + 1, :]) * (1.0 + m_ref[mod0 + 1:mod0 + 2, :]) + m_ref[mod0:mod0 + 1, :]
        u_ref[...] = u.astype(BF16)

    u = u_ref[...]
    a = _dot(u, wg_ref[...])
    b = _dot(u, wu_ref[...])
    c = _dot((_silu(a) * b).astype(BF16), wd_ref[...])

    @pl.when(k == 0)
    def _():
        acc_ref[...] = c

    @pl.when(k > 0)
    def _():
        acc_ref[...] += c

    @pl.when(k == pl.num_programs(1) - 1)
    def _():
        gate = m_ref[mod0 + 2:mod0 + 3, :]
        o_ref[...] = h_ref[...] + (0.5 * gate) * _rms(acc_ref[...], g_ref[g0 + 1:g0 + 2, :])


def _ffn(h, mods, norm_g, wg, wu, wd, layer, half, n_tiles, grp):
    d = h.shape[1]
    f = wg.shape[-1]
    tm, th = ROW_TILE, FFN_CHUNK
    return pl.pallas_call(
        functools.partial(_ffn_kernel, mod0=6 * half, g0=4 * half),
        grid=(n_tiles, f // th),
        in_specs=[
            pl.BlockSpec((tm, d), lambda i, k: (i, 0)),
            pl.BlockSpec((None, None, N_MOD, d), lambda i, k: (layer, grp(i), 0, 0)),
            pl.BlockSpec((None, 6, d), lambda i, k: (layer, 0, 0)),
            pl.BlockSpec((None, None, d, th), lambda i, k: (layer, half, 0, k)),
            pl.BlockSpec((None, None, d, th), lambda i, k: (layer, half, 0, k)),
            pl.BlockSpec((None, None, th, d), lambda i, k: (layer, half, k, 0)),
        ],
        out_specs=pl.BlockSpec((tm, d), lambda i, k: (i, 0)),
        out_shape=jax.ShapeDtypeStruct((n_tiles * tm, d), F32),
        scratch_shapes=[pltpu.VMEM((tm, d), BF16), pltpu.VMEM((tm, d), F32)],
        compiler_params=_params("parallel", "arbitrary"),
        name=f"swiglu_l{layer}_{half}",
    )(h, mods, norm_g, wg, wu, wd)


def _proj_kernel(*refs, even, n_rope, scale):
    if even:
        h_ref, m_ref, g_ref, w_ref, c_ref, sa_ref, o_ref, u_ref = refs
    else:
        h_ref, m_ref, g_ref, w_ref, c_ref, sa_ref, sb_ref, o_ref, u_ref = refs
    n = pl.program_id(1)

    @pl.when(n == 0)
    def _():
        u = _rms(h_ref[...], g_ref[2:3, :]) * (1.0 + m_ref[4:5, :]) + m_ref[3:4, :]
        u_ref[...] = u.astype(BF16)

    r = _dot(u_ref[...], w_ref[...])

    @pl.when(n < n_rope)
    def _():
        if even:
            sc = jnp.where(n < n_rope // 2, 1.0, scale)
        else:
            sc = jnp.where(n < n_rope // 2, scale, 1.0)
        for j in range(r.shape[1] // LANES):
            xb = r[:, j * LANES:(j + 1) * LANES]
            if even:
                tb = (j % 2) * LANES
                ob = xb * c_ref[:, tb:tb + LANES] + pltpu.roll(xb, 64, 1) * sa_ref[:, tb:tb + LANES]
            else:
                ob = xb * c_ref[...] + pltpu.roll(xb, 96, 1) * sa_ref[...] + pltpu.roll(xb, 32, 1) * sb_ref[...]
            o_ref[:, j * LANES:(j + 1) * LANES] = (ob * sc).astype(BF16)

    @pl.when(n >= n_rope)
    def _():
        o_ref[...] = r.astype(BF16)


def _proj_in(h, mods, norm_g, w_in, tables, layer, idx, even, n_tiles, grp, rope_blk):
    d = h.shape[1]
    n_out = w_in.shape[-1]
    tm, tn = ROW_TILE, PROJ_CHUNK
    tw = tables[0].shape[1]
    if even:
        n_rope, scale = (2 * (d // 2)) // tn, float((d // 2 // RET_HEADS) ** -0.5)
    else:
        n_rope, scale = (2 * d) // tn, float((d // DIFF_HEADS // 2) ** -0.5)
    return pl.pallas_call(
        functools.partial(_proj_kernel, even=even, n_rope=n_rope, scale=scale),
        grid=(n_tiles, n_out // tn),
        in_specs=[
            pl.BlockSpec((tm, d), lambda i, n: (i, 0)),
            pl.BlockSpec((None, None, N_MOD, d), lambda i, n: (layer, grp(i), 0, 0)),
            pl.BlockSpec((None, 6, d), lambda i, n: (layer, 0, 0)),
            pl.BlockSpec((None, d, tn), lambda i, n: (idx, 0, n)),
        ] + [pl.BlockSpec((tm, tw), lambda i, n: (rope_blk(i), 0)) for _ in tables],
        out_specs=pl.BlockSpec((tm, tn), lambda i, n: (i, n)),
        out_shape=jax.ShapeDtypeStruct((n_tiles * tm, n_out), BF16),
        scratch_shapes=[pltpu.VMEM((tm, d), BF16)],
        compiler_params=_params("parallel", "arbitrary"),
        name=f"mixer_in_l{layer}",
    )(h, mods, norm_g, w_in, *tables)


def _rope_tables(seq, n_ident, even):
    t = jnp.arange(seq)
    rows = (t // GRID_W).astype(F32)
    cols = (t % GRID_W).astype(F32)
    nf = 64 if even else 32
    inv = ROPE_BASE ** (-jnp.arange(nf, dtype=F32) / nf)
    ar, ac = rows[:, None] * inv[None, :], cols[:, None] * inv[None, :]
    cr, sr, cc, sc = jnp.cos(ar), jnp.sin(ar), jnp.cos(ac), jnp.sin(ac)
    z = jnp.zeros_like(sr)
    if even:
        tabs = [jnp.concatenate([cr, cr, cc, cc], 1), jnp.concatenate([-sr, sr, -sc, sc], 1)]
    else:
        tabs = [jnp.concatenate([cr, cr, cc, cc], 1), jnp.concatenate([-sr, z, -sc, z], 1),
                jnp.concatenate([z, sr, z, sc], 1)]
    ident = [jnp.ones((n_ident, tabs[0].shape[1]), F32)] + [jnp.zeros((n_ident, tabs[0].shape[1]), F32)] * (len(tabs) - 1)
    return [jnp.concatenate([a, b], 0) for a, b in zip(tabs, ident)]


def _attn_kernel(*refs, has_lat, lam_init):
    if has_lat:
        lam_ref, sg_ref, q_ref, kc_ref, vc_ref, k_ref, v_ref, o_ref, m_ref, l_ref, acc_ref = refs
    else:
        lam_ref, sg_ref, q_ref, kc_ref, vc_ref, _, o_ref, m_ref, l_ref, acc_ref = refs
    ki = pl.program_id(3)
    dh = q_ref.shape[1] // 2

    def update(kb, vb):
        for mi in range(2):
            s = _dot_nt(q_ref[:, mi * dh:(mi + 1) * dh], kb[:, mi * dh:(mi + 1) * dh])
            m_old = m_ref[mi]
            m_new = jnp.maximum(m_old, jnp.max(s, axis=-1, keepdims=True))
            alpha = jnp.exp(m_old - m_new)
            p = jnp.exp(s - m_new)
            l_ref[mi] = alpha * l_ref[mi] + jnp.sum(p, axis=-1, keepdims=True)
            acc_ref[mi] = alpha * acc_ref[mi] + _dot(p.astype(BF16), vb)
            m_ref[mi] = m_new

    @pl.when(ki == 0)
    def _():
        m_ref[...] = jnp.full(m_ref.shape, -jnp.inf, F32)
        l_ref[...] = jnp.zeros(l_ref.shape, F32)
        acc_ref[...] = jnp.zeros(acc_ref.shape, F32)
        update(kc_ref[...], vc_ref[...])

    if has_lat:
        update(k_ref[...], v_ref[...])

    @pl.when(ki == pl.num_programs(3) - 1)
    def _():
        t1 = jnp.sum(lam_ref[0:1, :] * lam_ref[1:2, :], axis=-1, keepdims=True)
        t2 = jnp.sum(lam_ref[2:3, :] * lam_ref[3:4, :], axis=-1, keepdims=True)
        lam = jnp.exp(t1) - jnp.exp(t2) + lam_init
        o = acc_ref[0] / l_ref[0] - lam * (acc_ref[1] / l_ref[1])
        o = o * lax.rsqrt(jnp.mean(o * o, axis=-1, keepdims=True) + SUBLN_EPS) * sg_ref[...] * (1.0 - lam_init)
        o_ref[...] = o.astype(BF16)


def _diff_attention(qkv, lam_vecs, subln_g, lam_init, n_batch, seq, ctx_len, need_ctx):
    rows, width = qkv.shape
    d = width // 3
    hd = d // DIFF_HEADS
    tq, tk = ATTN_Q_TILE, ATTN_K_TILE
    nq, nk = seq // tq, seq // tk
    ctx_blk0 = (n_batch * seq) // ctx_len
    scratch = lambda t: [pltpu.VMEM((2, t, 1), F32), pltpu.VMEM((2, t, 1), F32), pltpu.VMEM((2, t, hd), F32)]
    small = [pl.BlockSpec((4, hd // 2), lambda b, h, qi, ki: (0, 0)),
             pl.BlockSpec((1, hd), lambda b, h, qi, ki: (0, 0))]
    ctx_kv = [pl.BlockSpec((ctx_len, hd), lambda b, h, qi, ki: (ctx_blk0 + b, DIFF_HEADS + h)),
              pl.BlockSpec((ctx_len, hd), lambda b, h, qi, ki: (ctx_blk0 + b, 2 * DIFF_HEADS + h))]
    y = pl.pallas_call(
        functools.partial(_attn_kernel, has_lat=True, lam_init=lam_init),
        grid=(n_batch, DIFF_HEADS, nq, nk),
        in_specs=small + [pl.BlockSpec((tq, hd), lambda b, h, qi, ki: (b * nq + qi, h))] + ctx_kv + [
            pl.BlockSpec((tk, hd), lambda b, h, qi, ki: (b * nk + ki, DIFF_HEADS + h)),
            pl.BlockSpec((tk, hd), lambda b, h, qi, ki: (b * nk + ki, 2 * DIFF_HEADS + h)),
        ],
        out_specs=pl.BlockSpec((tq, hd), lambda b, h, qi, ki: (b * nq + qi, h)),
        out_shape=jax.ShapeDtypeStruct((rows, d), BF16),
        scratch_shapes=scratch(tq),
        compiler_params=_params("parallel", "parallel", "parallel", "arbitrary"),
        name="diff_attention_latent",
    )(lam_vecs, subln_g, qkv, qkv, qkv, qkv, qkv)
    if not need_ctx:
        return y
    return pl.pallas_call(
        functools.partial(_attn_kernel, has_lat=False, lam_init=lam_init),
        grid=(n_batch, DIFF_HEADS, 1, 1),
        in_specs=small + [pl.BlockSpec((ctx_len, hd), lambda b, h, qi, ki: (ctx_blk0 + b, h))] + ctx_kv + [
            pl.BlockSpec(memory_space=pl.ANY)],
        out_specs=pl.BlockSpec((ctx_len, hd), lambda b, h, qi, ki: (ctx_blk0 + b, h)),
        out_shape=jax.ShapeDtypeStruct((rows, d), BF16),
        scratch_shapes=scratch(ctx_len),
        input_output_aliases={5: 0},
        compiler_params=_params("parallel", "parallel", "arbitrary", "arbitrary"),
        name="diff_attention_context",
    )(lam_vecs, subln_g, qkv, qkv, qkv, y)


def _ret_fwd_kernel(lg_ref, q_ref, k_ref, v_ref, o_ref, d_ref, s_ref):
    n = pl.program_id(1)
    c = q_ref.shape[0]
    hd = q_ref.shape[1] // RET_HEADS

    @pl.when(n == 0)
    def _():
        s_ref[...] = jnp.zeros(s_ref.shape, F32)
        i = lax.broadcasted_iota(jnp.int32, (c, c), 0)
        j = lax.broadcasted_iota(jnp.int32, (c, c), 1)
        dist = (i - j).astype(F32)
        for h in range(RET_HEADS):
            fwd = jnp.exp(lg_ref[0, h] * jnp.maximum(dist, 0.0))
            bwd = jnp.exp(lg_ref[1, h] * jnp.maximum(-dist, 0.0))
            d_ref[h] = jnp.where(dist >= 0, fwd, bwd)

    pos = lax.broadcasted_iota(jnp.int32, (c, 1), 0).astype(F32)
    for h in range(RET_HEADS):
        lg = lg_ref[0, h]
        sl = slice(h * hd, (h + 1) * hd)
        qh, kh, vh = q_ref[:, sl], k_ref[:, sl], v_ref[:, sl]
        p = (_dot_nt(qh, kh) * d_ref[h]).astype(BF16)
        qd = (qh.astype(F32) * jnp.exp(lg * (pos + 1.0))).astype(BF16)
        s = s_ref[h]
        o_ref[:, sl] = _dot(p, vh) + _dot(qd, s.astype(BF16))
        kd = (kh.astype(F32) * jnp.exp(lg * (c - 1.0 - pos))).astype(BF16)
        s_ref[h] = jnp.exp(jnp.full((1, 1), lg * c, F32)) * s + _dot_tn(kd, vh)


def _ret_bwd_kernel(lg_ref, q_ref, k_ref, v_ref, g_ref, r_ref, o_ref, s_ref):
    n = pl.program_id(1)
    c = q_ref.shape[0]
    hd = q_ref.shape[1] // RET_HEADS

    @pl.when(n == 0)
    def _():
        s_ref[...] = jnp.zeros(s_ref.shape, F32)

    pos = lax.broadcasted_iota(jnp.int32, (c, 1), 0).astype(F32)
    for h in range(RET_HEADS):
        lg = lg_ref[1, h]
        sl = slice(h * hd, (h + 1) * hd)
        qh, kh, vh = q_ref[:, sl], k_ref[:, sl], v_ref[:, sl]
        qd = (qh.astype(F32) * jnp.exp(lg * (c - pos))).astype(BF16)
        s = s_ref[h]
        r = r_ref[:, sl] + _dot(qd, s.astype(BF16))
        y = r * lax.rsqrt(jnp.mean(r * r, axis=-1, keepdims=True) + NORM_EPS)
        o_ref[:, sl] = (y * _silu(g_ref[:, sl].astype(F32))).astype(BF16)
        kd = (kh.astype(F32) * jnp.exp(lg * pos)).astype(BF16)
        s_ref[h] = jnp.exp(jnp.full((1, 1), lg * c, F32)) * s + _dot_tn(kd, vh)


def _retention(qkv, log_decay, n_batch, seq, ctx_len):
    rows = qkv.shape[0]
    c = ctx_len
    w = (qkv.shape[1] // 5)
    nc = seq // c
    ctx_blk0 = (n_batch * seq) // c
    fwd_blk = lambda b, n: jnp.where(n == 0, ctx_blk0 + b, b * nc + n - 1)
    bwd_blk = lambda b, n: jnp.where(n == 0, ctx_blk0 + b, b * nc + nc - n)
    smem = pl.BlockSpec(memory_space=pltpu.SMEM)
    state = pltpu.VMEM((RET_HEADS, w // RET_HEADS, w // RET_HEADS), F32)
    r1 = pl.pallas_call(
        _ret_fwd_kernel,
        grid=(n_batch, nc + 1),
        in_specs=[smem] + [pl.BlockSpec((c, w), functools.partial(lambda b, n, j: (fwd_blk(b, n), j), j=j)) for j in range(3)],
        out_specs=pl.BlockSpec((c, w), lambda b, n: (fwd_blk(b, n), 0)),
        out_shape=jax.ShapeDtypeStruct((rows, w), F32),
        scratch_shapes=[pltpu.VMEM((RET_HEADS, c, c), F32), state],
        compiler_params=_params("parallel", "arbitrary"),
        name="retention_forward_sweep",
    )(log_decay, qkv, qkv, qkv)
    return pl.pallas_call(
        _ret_bwd_kernel,
        grid=(n_batch, nc + 1),
        in_specs=[smem] + [pl.BlockSpec((c, w), functools.partial(lambda b, n, j: (bwd_blk(b, n), j), j=j)) for j in range(4)]
        + [pl.BlockSpec((c, w), lambda b, n: (bwd_blk(b, n), 0))],
        out_specs=pl.BlockSpec((c, w), lambda b, n: (bwd_blk(b, n), 0)),
        out_shape=jax.ShapeDtypeStruct((rows, w), BF16),
        scratch_shapes=[state],
        compiler_params=_params("parallel", "arbitrary"),
        name="retention_backward_sweep",
    )(log_decay, qkv, qkv, qkv, qkv, r1)


def _pool_kernel(prev_ref, main_ref, next_ref, w_ref, sc_ref, o_ref, *, lat_tiles, tiles_per_seq, seq, ctx_len):
    i = pl.program_id(0)
    tp = main_ref.shape[0]
    gw = main_ref.shape[1] // len(POOL_WINDOWS)
    is_lat = i < lat_tiles
    t0 = jnp.where(is_lat, (i % tiles_per_seq) * tp, 0)
    length = jnp.where(is_lat, seq, ctx_len)
    n_ext = tp + 2 * POOL_HALO
    t = t0 - POOL_HALO + lax.broadcasted_iota(jnp.int32, (n_ext, 1), 0)
    valid = (t >= 0) & (t < length)
    tm = t[POOL_HALO:POOL_HALO + tp]
    for gi, win in enumerate(POOL_WINDOWS):
        sl = slice(gi * gw, (gi + 1) * gw)
        ext = jnp.concatenate([prev_ref[:, sl], main_ref[:, sl], next_ref[:, sl]], axis=0).astype(F32)
        x = jnp.where(valid, ext, 0.0)
        acc = x + pltpu.roll(x, 1, 0)
        half = 1
        while 2 * half < win:
            acc = pltpu.roll(acc, n_ext - half, 0) + pltpu.roll(acc, half, 0)
            half *= 2
        cnt = (jnp.minimum(tm + win // 2, length) - jnp.maximum(tm - win // 2, 0)).astype(F32)
        pooled = acc[POOL_HALO:POOL_HALO + tp] / cnt - x[POOL_HALO:POOL_HALO + tp]
        o_ref[:, sl] = (_dot(pooled.astype(BF16), w_ref[gi]) * sc_ref[:, sl]).astype(BF16)


def _pool(qkv, pool_w, pool_scale, idx, n_batch, seq, ctx_len):
    rows = qkv.shape[0]
    w = qkv.shape[1] // 5
    tp, halo = POOL_TILE, POOL_HALO
    assert ctx_len == tp and seq % tp == 0
    last_halo_blk = rows // halo - 1
    per = tp // halo
    return pl.pallas_call(
        functools.partial(_pool_kernel, lat_tiles=(n_batch * seq) // tp, tiles_per_seq=seq // tp, seq=seq, ctx_len=ctx_len),
        grid=(rows // tp,),
        in_specs=[
            pl.BlockSpec((halo, w), lambda i: (jnp.maximum(i * per - 1, 0), 4)),
            pl.BlockSpec((tp, w), lambda i: (i, 4)),
            pl.BlockSpec((halo, w), lambda i: (jnp.minimum((i + 1) * per, last_halo_blk), 4)),
            pl.BlockSpec((None,) + pool_w.shape[1:], lambda i: (idx, 0, 0, 0)),
            pl.BlockSpec((None, 1, w), lambda i: (idx, 0, 0)),
        ],
        out_specs=pl.BlockSpec((tp, w), lambda i: (i, 0)),
        out_shape=jax.ShapeDtypeStruct((rows, w), BF16),
        compiler_params=_params("parallel"),
        name="multiscale_pool",
    )(qkv, qkv, qkv, pool_w, pool_scale.reshape(pool_scale.shape[0], 1, w))


def _out_kernel(h_ref, m_ref, g_ref, ya_ref, yb_ref, wa_ref, wb_ref, o_ref):
    y = _dot(ya_ref[...], wa_ref[...]) + _dot(yb_ref[...], wb_ref[...])
    o_ref[...] = h_ref[...] + m_ref[5:6, :] * _rms(y, g_ref[3:4, :])


def _out_proj(h, mods, norm_g, ya, ya_blk, yb, yb_blk, w_out, layer, idx, n_tiles, grp):
    d = h.shape[1]
    tm = ROW_TILE
    return pl.pallas_call(
        _out_kernel,
        grid=(n_tiles,),
        in_specs=[
            pl.BlockSpec((tm, d), lambda i: (i, 0)),
            pl.BlockSpec((None, None, N_MOD, d), lambda i: (layer, grp(i), 0, 0)),
            pl.BlockSpec((None, 6, d), lambda i: (layer, 0, 0)),
            pl.BlockSpec((tm, d // 2), lambda i: (i, ya_blk)),
            pl.BlockSpec((tm, d // 2), lambda i: (i, yb_blk)),
            pl.BlockSpec((None, d // 2, d), lambda i: (idx, 0, 0)),
            pl.BlockSpec((None, d // 2, d), lambda i: (idx, 1, 0)),
        ],
        out_specs=pl.BlockSpec((tm, d), lambda i: (i, 0)),
        out_shape=jax.ShapeDtypeStruct((n_tiles * tm, d), F32),
        compiler_params=_params("parallel"),
        name=f"mixer_out_l{layer}",
    )(h, mods, norm_g, ya, yb, w_out, w_out)


def kernel(x, c, ctx, c_ctx, mod_w, mod_b, norm_g, ffn_w_gate, ffn_w_up, ffn_w_down, ev_w_in, ev_w_out, ret_decay, pool_w, pool_scale, od_w_in, od_w_out, lam_q1, lam_k1, lam_q2, lam_k2, diff_subln_g):
    n_batch, seq, d = x.shape
    ctx_len = ctx.shape[1]
    depth = mod_w.shape[0]
    tm = ROW_TILE
    lat_rows, ctx_rows = n_batch * seq, n_batch * ctx_len
    assert seq % tm == 0 and ctx_rows % tm == 0 and n_batch + 1 <= 8
    assert d // 2 // RET_HEADS == 2 * LANES and d // DIFF_HEADS == 2 * LANES and ctx_len == POOL_TILE
    lat_tiles, all_tiles = lat_rows // tm, (lat_rows + ctx_rows) // tm
    tiles_per_batch = seq // tm
    grp = lambda i: jnp.minimum(i // tiles_per_batch, n_batch)
    rope_blk = lambda i: jnp.where(i < lat_tiles, i % tiles_per_batch, tiles_per_batch)

    cond = jnp.concatenate([c, c_ctx[None, :], jnp.zeros((8 - n_batch - 1, d), F32)], axis=0)
    mods = _modulation(cond, mod_w, mod_b)[:, :n_batch + 1].reshape(depth, n_batch + 1, N_MOD, d)

    wg, wu, wd = ffn_w_gate.astype(BF16), ffn_w_up.astype(BF16), ffn_w_down.astype(BF16)
    ev_in, ev_out, od_in, od_out = ev_w_in.astype(BF16), ev_w_out.astype(BF16), od_w_in.astype(BF16), od_w_out.astype(BF16)
    pw = pool_w.astype(BF16)
    log_decay = jax.nn.log_sigmoid(ret_decay.astype(F32))
    tabs_even = _rope_tables(seq, tm, True)
    tabs_odd = _rope_tables(seq, tm, False)

    h = jnp.concatenate([x.reshape(lat_rows, d), ctx.reshape(ctx_rows, d)], axis=0)
    for l in range(depth):
        need_ctx = l < depth - 1
        out_tiles = all_tiles if need_ctx else lat_tiles
        h = _ffn(h, mods, norm_g, wg, wu, wd, l, 0, all_tiles, grp)
        if l % 2 == 0:
            e = l // 2
            qkv = _proj_in(h, mods, norm_g, ev_in, tabs_even, l, e, True, all_tiles, grp, rope_blk)
            y = _retention(qkv, log_decay[e], n_batch, seq, ctx_len)
            pm = _pool(qkv, pw, pool_scale, e, n_batch, seq, ctx_len)
            h = _out_proj(h, mods, norm_g, y, 0, pm, 0, ev_out, l, e, out_tiles, grp)
        else:
            o = l // 2
            lam_init = 0.8 - 0.6 * math.exp(-0.3 * l)
            qkv = _proj_in(h, mods, norm_g, od_in, tabs_odd, l, o, False, all_tiles, grp, rope_blk)
            lam_vecs = jnp.stack([lam_q1[o], lam_k1[o], lam_q2[o], lam_k2[o]], axis=0)
            y = _diff_attention(qkv, lam_vecs, diff_subln_g[o][None, :], lam_init, n_batch, seq, ctx_len, need_ctx)
            h = _out_proj(h, mods, norm_g, y, 0, y, 1, od_out, l, o, out_tiles, grp)
        h = _ffn(h, mods, norm_g, wg, wu, wd, l, 1, out_tiles, grp)
    return h[:lat_rows].reshape(n_batch, seq, d)
```

```python
import functools
import math

import jax
import jax.numpy as jnp
from jax import lax
from jax.experimental import pallas as pl
from jax.experimental.pallas import tpu as pltpu

F32 = jnp.float32
BF16 = jnp.bfloat16

GRID_W = 64
ROPE_BASE = 10000.0
N_MOD = 9
RET_HEADS = 4
POOL_WINDOWS = (2, 4, 8, 16)
DIFF_HEADS = 8
NORM_EPS = 1e-6
SUBLN_EPS = 1e-5

LANES = 128
ROW_TILE = 512
FFN_CHUNK = 512
PROJ_CHUNK = 512
MOD_CHUNK = 1024
ATTN_Q_TILE = 512
ATTN_K_TILE = 512
POOL_TILE = 256
POOL_HALO = 16
VMEM_LIMIT = 56 * 1024 * 1024


def _params(*sem):
    return pltpu.CompilerParams(dimension_semantics=sem, vmem_limit_bytes=VMEM_LIMIT)


def _silu(x):
    return x * jax.nn.sigmoid(x)


def _rms(x, g, eps=NORM_EPS):
    return x * lax.rsqrt(jnp.mean(x * x, axis=-1, keepdims=True) + eps) * g


def _dot(a, b):
    return jnp.dot(a, b, preferred_element_type=F32)


def _dot_nt(a, b):
    return lax.dot_general(a, b, (((1,), (1,)), ((), ())), preferred_element_type=F32)


def _dot_tn(a, b):
    return lax.dot_general(a, b, (((0,), (0,)), ((), ())), preferred_element_type=F32)


def _mod_kernel(a_ref, w_ref, b_ref, o_ref):
    s = _silu(a_ref[...]).astype(BF16)
    o_ref[...] = _dot(s, w_ref[...].astype(BF16)) + b_ref[...]


def _modulation(cond, mod_w, mod_b):
    depth, d, n = mod_w.shape
    tn = MOD_CHUNK
    return pl.pallas_call(
        _mod_kernel,
        grid=(depth, n // tn),
        in_specs=[
            pl.BlockSpec((8, d), lambda l, j: (0, 0)),
            pl.BlockSpec((None, d, tn), lambda l, j: (l, 0, j)),
            pl.BlockSpec((None, 1, tn), lambda l, j: (l, 0, j)),
        ],
        out_specs=pl.BlockSpec((None, 8, tn), lambda l, j: (l, 0, j)),
        out_shape=jax.ShapeDtypeStruct((depth, 8, n), F32),
        compiler_params=_params("parallel", "parallel"),
        name="adaln_modulation",
    )(cond, mod_w, mod_b.reshape(depth, 1, n))


def _ffn_kernel(h_ref, m_ref, g_ref, wg_ref, wu_ref, wd_ref, o_ref, u_ref, acc_ref, *, mod0, g0):
    k = pl.program_id(1)

    @pl.when(k == 0)
    def _():
        u = _rms(h_ref[...], g_ref[g0:g0 + 1, :]) * (1.0 + m_ref[mod0 + 1:mod0 + 2, :]) + m_ref[mod0:mod0 + 1, :]
        u_ref[...] = u.astype(BF16)

    u = u_ref[...]
    a = _dot(u, wg_ref[...])
    b = _dot(u, wu_ref[...])
    c = _dot((_silu(a) * b).astype(BF16), wd_ref[...])

    @pl.when(k == 0)
    def _():
        acc_ref[...] = c

    @pl.when(k > 0)
    def _():
        acc_ref[...] += c

    @pl.when(k == pl.num_programs(1) - 1)
    def _():
        gate = m_ref[mod0 + 2:mod0 + 3, :]
        o_ref[...] = h_ref[...] + (0.5 * gate) * _rms(acc_ref[...], g_ref[g0 + 1:g0 + 2, :])


def _ffn(h, mods, norm_g, wg, wu, wd, layer, half, n_tiles, grp):
    d = h.shape[1]
    f = wg.shape[-1]
    tm, th = ROW_TILE, FFN_CHUNK
    return pl.pallas_call(
        functools.partial(_ffn_kernel, mod0=6 * half, g0=4 * half),
        grid=(n_tiles, f // th),
        in_specs=[
            pl.BlockSpec((tm, d), lambda i, k: (i, 0)),
            pl.BlockSpec((None, None, N_MOD, d), lambda i, k: (layer, grp(i), 0, 0)),
            pl.BlockSpec((None, 6, d), lambda i, k: (layer, 0, 0)),
            pl.BlockSpec((None, None, d, th), lambda i, k: (layer, half, 0, k)),
            pl.BlockSpec((None, None, d, th), lambda i, k: (layer, half, 0, k)),
            pl.BlockSpec((None, None, th, d), lambda i, k: (layer, half, k, 0)),
        ],
        out_specs=pl.BlockSpec((tm, d), lambda i, k: (i, 0)),
        out_shape=jax.ShapeDtypeStruct((n_tiles * tm, d), F32),
        scratch_shapes=[pltpu.VMEM((tm, d), BF16), pltpu.VMEM((tm, d), F32)],
        compiler_params=_params("parallel", "arbitrary"),
        name=f"swiglu_l{layer}_{half}",
    )(h, mods, norm_g, wg, wu, wd)


def _proj_kernel(*refs, even, n_rope, scale):
    if even:
        h_ref, m_ref, g_ref, w_ref, c_ref, sa_ref, o_ref, u_ref = refs
    else:
        h_ref, m_ref, g_ref, w_ref, c_ref, sa_ref, sb_ref, o_ref, u_ref = refs
    n = pl.program_id(1)

    @pl.when(n == 0)
    def _():
        u = _rms(h_ref[...], g_ref[2:3, :]) * (1.0 + m_ref[4:5, :]) + m_ref[3:4, :]
        u_ref[...] = u.astype(BF16)

    r = _dot(u_ref[...], w_ref[...])

    @pl.when(n < n_rope)
    def _():
        if even:
            sc = jnp.where(n < n_rope // 2, 1.0, scale)
        else:
            sc = jnp.where(n < n_rope // 2, scale, 1.0)
        for j in range(r.shape[1] // LANES):
            xb = r[:, j * LANES:(j + 1) * LANES]
            if even:
                tb = (j % 2) * LANES
                ob = xb * c_ref[:, tb:tb + LANES] + pltpu.roll(xb, 64, 1) * sa_ref[:, tb:tb + LANES]
            else:
                ob = xb * c_ref[...] + pltpu.roll(xb, 96, 1) * sa_ref[...] + pltpu.roll(xb, 32, 1) * sb_ref[...]
            o_ref[:, j * LANES:(j + 1) * LANES] = (ob * sc).astype(BF16)

    @pl.when(n >= n_rope)
    def _():
        o_ref[...] = r.astype(BF16)


def _proj_in(h, mods, norm_g, w_in, tables, layer, idx, even, n_tiles, grp, rope_blk):
    d = h.shape[1]
    n_out = w_in.shape[-1]
    tm, tn = ROW_TILE, PROJ_CHUNK
    tw = tables[0].shape[1]
    if even:
        n_rope, scale = (2 * (d // 2)) // tn, float((d // 2 // RET_HEADS) ** -0.5)
    else:
        n_rope, scale = (2 * d) // tn, float((d // DIFF_HEADS // 2) ** -0.5 * math.log2(math.e))
    return pl.pallas_call(
        functools.partial(_proj_kernel, even=even, n_rope=n_rope, scale=scale),
        grid=(n_tiles, n_out // tn),
        in_specs=[
            pl.BlockSpec((tm, d), lambda i, n: (i, 0)),
            pl.BlockSpec((None, None, N_MOD, d), lambda i, n: (layer, grp(i), 0, 0)),
            pl.BlockSpec((None, 6, d), lambda i, n: (layer, 0, 0)),
            pl.BlockSpec((None, d, tn), lambda i, n: (idx, 0, n)),
        ] + [pl.BlockSpec((tm, tw), lambda i, n: (rope_blk(i), 0)) for _ in tables],
        out_specs=pl.BlockSpec((tm, tn), lambda i, n: (i, n)),
        out_shape=jax.ShapeDtypeStruct((n_tiles * tm, n_out), BF16),
        scratch_shapes=[pltpu.VMEM((tm, d), BF16)],
        compiler_params=_params("parallel", "arbitrary"),
        name=f"mixer_in_l{layer}",
    )(h, mods, norm_g, w_in, *tables)


def _rope_tables(seq, n_ident, even):
    t = jnp.arange(seq)
    rows = (t // GRID_W).astype(F32)
    cols = (t % GRID_W).astype(F32)
    nf = 64 if even else 32
    inv = ROPE_BASE ** (-jnp.arange(nf, dtype=F32) / nf)
    ar, ac = rows[:, None] * inv[None, :], cols[:, None] * inv[None, :]
    cr, sr, cc, sc = jnp.cos(ar), jnp.sin(ar), jnp.cos(ac), jnp.sin(ac)
    z = jnp.zeros_like(sr)
    if even:
        tabs = [jnp.concatenate([cr, cr, cc, cc], 1), jnp.concatenate([-sr, sr, -sc, sc], 1)]
    else:
        tabs = [jnp.concatenate([cr, cr, cc, cc], 1), jnp.concatenate([-sr, z, -sc, z], 1),
                jnp.concatenate([z, sr, z, sc], 1)]
    ident = [jnp.ones((n_ident, tabs[0].shape[1]), F32)] + [jnp.zeros((n_ident, tabs[0].shape[1]), F32)] * (len(tabs) - 1)
    return [jnp.concatenate([a, b], 0) for a, b in zip(tabs, ident)]


def _attn_kernel(*refs, tk, n_kt, lam_init):
    if n_kt:
        lam_ref, sg_ref, q_ref, kc_ref, vc_ref, k_ref, v_ref, o_ref, m_ref, l_ref, acc_ref, s_ref = refs
    else:
        lam_ref, sg_ref, q_ref, kc_ref, vc_ref, _, o_ref, m_ref, l_ref, acc_ref = refs
    dh = q_ref.shape[1] // 2
    m_ref[...] = jnp.full(m_ref.shape, -jnp.inf, F32)
    l_ref[...] = jnp.zeros(l_ref.shape, F32)
    acc_ref[...] = jnp.zeros(acc_ref.shape, F32)

    def scores(kb, mi):
        return _dot_nt(q_ref[:, mi * dh:(mi + 1) * dh], kb[:, mi * dh:(mi + 1) * dh])

    def absorb(s, vb, mi):
        m_prev = m_ref[mi]
        m_next = jnp.maximum(m_prev, jnp.max(s, axis=-1, keepdims=True))
        alpha = jnp.exp2(m_prev - m_next)
        p = jnp.exp2(s - jnp.concatenate([m_next] * (s.shape[1] // LANES), axis=1))
        l_ref[mi] = alpha * l_ref[mi] + jnp.sum(p, axis=-1, keepdims=True)
        acc_ref[mi] = (jnp.concatenate([alpha] * (vb.shape[1] // LANES), axis=1) * acc_ref[mi]
                       + _dot(p.astype(BF16), vb))
        m_ref[mi] = m_next

    kc, vc = kc_ref[...], vc_ref[...]
    for mi in range(2):
        absorb(scores(kc, mi), vc, mi)

    if n_kt:
        def qk(j, slot):
            kb = k_ref[pl.ds(pl.multiple_of(j * tk, tk), tk), :]
            for mi in range(2):
                s_ref[slot, mi] = scores(kb, mi)

        def pv(j, slot):
            vb = v_ref[pl.ds(pl.multiple_of(j * tk, tk), tk), :]
            for mi in range(2):
                absorb(s_ref[slot, mi], vb, mi)

        def body(i, carry):
            j = 2 * i
            qk(j + 1, 1)
            pv(j, 0)
            qk(j + 2, 0)
            pv(j + 1, 1)
            return carry

        qk(0, 0)
        lax.fori_loop(0, n_kt // 2 - 1, body, 0)
        qk(n_kt - 1, 1)
        pv(n_kt - 2, 0)
        pv(n_kt - 1, 1)

    t1 = jnp.sum(lam_ref[0:1, :] * lam_ref[1:2, :], axis=-1, keepdims=True)
    t2 = jnp.sum(lam_ref[2:3, :] * lam_ref[3:4, :], axis=-1, keepdims=True)
    lam = jnp.exp(t1) - jnp.exp(t2) + lam_init
    nrep = acc_ref.shape[2] // LANES
    o = (acc_ref[0] / jnp.concatenate([l_ref[0]] * nrep, axis=1)
         - lam * (acc_ref[1] / jnp.concatenate([l_ref[1]] * nrep, axis=1)))
    o = o * lax.rsqrt(jnp.mean(o * o, axis=-1, keepdims=True) + SUBLN_EPS) * sg_ref[...] * (1.0 - lam_init)
    o_ref[...] = o.astype(BF16)


def _diff_attention(qkv, lam_vecs, subln_g, lam_init, n_batch, seq, ctx_len, need_ctx):
    rows, width = qkv.shape
    d = width // 3
    hd = d // DIFF_HEADS
    tq, tk = ATTN_Q_TILE, ATTN_K_TILE
    nq = seq // tq
    assert seq % (2 * tk) == 0
    ctx_blk0 = (n_batch * seq) // ctx_len
    scratch = lambda t: [pltpu.VMEM((2, t, LANES), F32), pltpu.VMEM((2, t, LANES), F32), pltpu.VMEM((2, t, hd), F32)]
    small = [pl.BlockSpec((4, hd // 2), lambda b, h, qi: (0, 0)),
             pl.BlockSpec((1, hd), lambda b, h, qi: (0, 0))]
    ctx_kv = [pl.BlockSpec((ctx_len, hd), lambda b, h, qi: (ctx_blk0 + b, DIFF_HEADS + h)),
              pl.BlockSpec((ctx_len, hd), lambda b, h, qi: (ctx_blk0 + b, 2 * DIFF_HEADS + h))]
    y = pl.pallas_call(
        functools.partial(_attn_kernel, tk=tk, n_kt=seq // tk, lam_init=lam_init),
        grid=(n_batch, DIFF_HEADS, nq),
        in_specs=small + [pl.BlockSpec((tq, hd), lambda b, h, qi: (b * nq + qi, h))] + ctx_kv + [
            pl.BlockSpec((seq, hd), lambda b, h, qi: (b, DIFF_HEADS + h)),
            pl.BlockSpec((seq, hd), lambda b, h, qi: (b, 2 * DIFF_HEADS + h)),
        ],
        out_specs=pl.BlockSpec((tq, hd), lambda b, h, qi: (b * nq + qi, h)),
        out_shape=jax.ShapeDtypeStruct((rows, d), BF16),
        scratch_shapes=scratch(tq) + [pltpu.VMEM((2, 2, tq, tk), F32)],
        compiler_params=_params("parallel", "parallel", "arbitrary"),
        name="diff_attention_latent",
    )(lam_vecs, subln_g, qkv, qkv, qkv, qkv, qkv)
    if not need_ctx:
        return y
    return pl.pallas_call(
        functools.partial(_attn_kernel, tk=tk, n_kt=0, lam_init=lam_init),
        grid=(n_batch, DIFF_HEADS, 1),
        in_specs=small + [pl.BlockSpec((ctx_len, hd), lambda b, h, qi: (ctx_blk0 + b, h))] + ctx_kv + [
            pl.BlockSpec(memory_space=pl.ANY)],
        out_specs=pl.BlockSpec((ctx_len, hd), lambda b, h, qi: (ctx_blk0 + b, h)),
        out_shape=jax.ShapeDtypeStruct((rows, d), BF16),
        scratch_shapes=scratch(ctx_len),
        input_output_aliases={5: 0},
        compiler_params=_params("parallel", "parallel", "arbitrary"),
        name="diff_attention_context",
    )(lam_vecs, subln_g, qkv, qkv, qkv, y)


def _ret_fwd_kernel(lg_ref, q_ref, k_ref, v_ref, o_ref, d_ref, s_ref):
    n = pl.program_id(1)
    c = q_ref.shape[0]
    hd = q_ref.shape[1] // RET_HEADS

    @pl.when(n == 0)
    def _():
        s_ref[...] = jnp.zeros(s_ref.shape, F32)
        i = lax.broadcasted_iota(jnp.int32, (c, c), 0)
        j = lax.broadcasted_iota(jnp.int32, (c, c), 1)
        dist = (i - j).astype(F32)
        for h in range(RET_HEADS):
            fwd = jnp.exp(lg_ref[0, h] * jnp.maximum(dist, 0.0))
            bwd = jnp.exp(lg_ref[1, h] * jnp.maximum(-dist, 0.0))
            d_ref[h] = jnp.where(dist >= 0, fwd, bwd)

    pos = lax.broadcasted_iota(jnp.int32, (c, 1), 0).astype(F32)
    for h in range(RET_HEADS):
        lg = lg_ref[0, h]
        sl = slice(h * hd, (h + 1) * hd)
        qh, kh, vh = q_ref[:, sl], k_ref[:, sl], v_ref[:, sl]
        p = (_dot_nt(qh, kh) * d_ref[h]).astype(BF16)
        qd = (qh.astype(F32) * jnp.exp(lg * (pos + 1.0))).astype(BF16)
        s = s_ref[h]
        o_ref[:, sl] = _dot(p, vh) + _dot(qd, s.astype(BF16))
        kd = (kh.astype(F32) * jnp.exp(lg * (c - 1.0 - pos))).astype(BF16)
        s_ref[h] = jnp.exp(jnp.full((1, 1), lg * c, F32)) * s + _dot_tn(kd, vh)


def _ret_bwd_kernel(lg_ref, q_ref, k_ref, v_ref, g_ref, r_ref, o_ref, s_ref):
    n = pl.program_id(1)
    c = q_ref.shape[0]
    hd = q_ref.shape[1] // RET_HEADS

    @pl.when(n == 0)
    def _():
        s_ref[...] = jnp.zeros(s_ref.shape, F32)

    pos = lax.broadcasted_iota(jnp.int32, (c, 1), 0).astype(F32)
    for h in range(RET_HEADS):
        lg = lg_ref[1, h]
        sl = slice(h * hd, (h + 1) * hd)
        qh, kh, vh = q_ref[:, sl], k_ref[:, sl], v_ref[:, sl]
        qd = (qh.astype(F32) * jnp.exp(lg * (c - pos))).astype(BF16)
        s = s_ref[h]
        r = r_ref[:, sl] + _dot(qd, s.astype(BF16))
        y = r * lax.rsqrt(jnp.mean(r * r, axis=-1, keepdims=True) + NORM_EPS)
        o_ref[:, sl] = (y * _silu(g_ref[:, sl].astype(F32))).astype(BF16)
        kd = (kh.astype(F32) * jnp.exp(lg * pos)).astype(BF16)
        s_ref[h] = jnp.exp(jnp.full((1, 1), lg * c, F32)) * s + _dot_tn(kd, vh)


def _retention(qkv, log_decay, n_batch, seq, ctx_len):
    rows = qkv.shape[0]
    c = ctx_len
    w = (qkv.shape[1] // 5)
    nc = seq // c
    ctx_blk0 = (n_batch * seq) // c
    fwd_blk = lambda b, n: jnp.where(n == 0, ctx_blk0 + b, b * nc + n - 1)
    bwd_blk = lambda b, n: jnp.where(n == 0, ctx_blk0 + b, b * nc + nc - n)
    smem = pl.BlockSpec(memory_space=pltpu.SMEM)
    state = pltpu.VMEM((RET_HEADS, w // RET_HEADS, w // RET_HEADS), F32)
    r1 = pl.pallas_call(
        _ret_fwd_kernel,
        grid=(n_batch, nc + 1),
        in_specs=[smem] + [pl.BlockSpec((c, w), functools.partial(lambda b, n, j: (fwd_blk(b, n), j), j=j)) for j in range(3)],
        out_specs=pl.BlockSpec((c, w), lambda b, n: (fwd_blk(b, n), 0)),
        out_shape=jax.ShapeDtypeStruct((rows, w), F32),
        scratch_shapes=[pltpu.VMEM((RET_HEADS, c, c), F32), state],
        compiler_params=_params("parallel", "arbitrary"),
        name="retention_forward_sweep",
    )(log_decay, qkv, qkv, qkv)
    return pl.pallas_call(
        _ret_bwd_kernel,
        grid=(n_batch, nc + 1),
        in_specs=[smem] + [pl.BlockSpec((c, w), functools.partial(lambda b, n, j: (bwd_blk(b, n), j), j=j)) for j in range(4)]
        + [pl.BlockSpec((c, w), lambda b, n: (bwd_blk(b, n), 0))],
        out_specs=pl.BlockSpec((c, w), lambda b, n: (bwd_blk(b, n), 0)),
        out_shape=jax.ShapeDtypeStruct((rows, w), BF16),
        scratch_shapes=[state],
        compiler_params=_params("parallel", "arbitrary"),
        name="retention_backward_sweep",
    )(log_decay, qkv, qkv, qkv, qkv, r1)


def _pool_kernel(prev_ref, main_ref, next_ref, w_ref, sc_ref, o_ref, *, lat_tiles, tiles_per_seq, seq, ctx_len):
    i = pl.program_id(0)
    tp = main_ref.shape[0]
    gw = main_ref.shape[1] // len(POOL_WINDOWS)
    is_lat = i < lat_tiles
    t0 = jnp.where(is_lat, (i % tiles_per_seq) * tp, 0)
    length = jnp.where(is_lat, seq, ctx_len)
    n_ext = tp + 2 * POOL_HALO
    t = t0 - POOL_HALO + lax.broadcasted_iota(jnp.int32, (n_ext, 1), 0)
    valid = (t >= 0) & (t < length)
    tm = t[POOL_HALO:POOL_HALO + tp]
    for gi, win in enumerate(POOL_WINDOWS):
        sl = slice(gi * gw, (gi + 1) * gw)
        ext = jnp.concatenate([prev_ref[:, sl], main_ref[:, sl], next_ref[:, sl]], axis=0).astype(F32)
        x = jnp.where(valid, ext, 0.0)
        acc = x + pltpu.roll(x, 1, 0)
        half = 1
        while 2 * half < win:
            acc = pltpu.roll(acc, n_ext - half, 0) + pltpu.roll(acc, half, 0)
            half *= 2
        cnt = (jnp.minimum(tm + win // 2, length) - jnp.maximum(tm - win // 2, 0)).astype(F32)
        pooled = acc[POOL_HALO:POOL_HALO + tp] / cnt - x[POOL_HALO:POOL_HALO + tp]
        o_ref[:, sl] = (_dot(pooled.astype(BF16), w_ref[gi]) * sc_ref[:, sl]).astype(BF16)


def _pool(qkv, pool_w, pool_scale, idx, n_batch, seq, ctx_len):
    rows = qkv.shape[0]
    w = qkv.shape[1] // 5
    tp, halo = POOL_TILE, POOL_HALO
    assert ctx_len == tp and seq % tp == 0
    last_halo_blk = rows // halo - 1
    per = tp // halo
    return pl.pallas_call(
        functools.partial(_pool_kernel, lat_tiles=(n_batch * seq) // tp, tiles_per_seq=seq // tp, seq=seq, ctx_len=ctx_len),
        grid=(rows // tp,),
        in_specs=[
            pl.BlockSpec((halo, w), lambda i: (jnp.maximum(i * per - 1, 0), 4)),
            pl.BlockSpec((tp, w), lambda i: (i, 4)),
            pl.BlockSpec((halo, w), lambda i: (jnp.minimum((i + 1) * per, last_halo_blk), 4)),
            pl.BlockSpec((None,) + pool_w.shape[1:], lambda i: (idx, 0, 0, 0)),
            pl.BlockSpec((None, 1, w), lambda i: (idx, 0, 0)),
        ],
        out_specs=pl.BlockSpec((tp, w), lambda i: (i, 0)),
        out_shape=jax.ShapeDtypeStruct((rows, w), BF16),
        compiler_params=_params("parallel"),
        name="multiscale_pool",
    )(qkv, qkv, qkv, pool_w, pool_scale.reshape(pool_scale.shape[0], 1, w))


def _out_kernel(h_ref, m_ref, g_ref, ya_ref, yb_ref, wa_ref, wb_ref, o_ref):
    y = _dot(ya_ref[...], wa_ref[...]) + _dot(yb_ref[...], wb_ref[...])
    o_ref[...] = h_ref[...] + m_ref[5:6, :] * _rms(y, g_ref[3:4, :])


def _out_proj(h, mods, norm_g, ya, ya_blk, yb, yb_blk, w_out, layer, idx, n_tiles, grp):
    d = h.shape[1]
    tm = ROW_TILE
    return pl.pallas_call(
        _out_kernel,
        grid=(n_tiles,),
        in_specs=[
            pl.BlockSpec((tm, d), lambda i: (i, 0)),
            pl.BlockSpec((None, None, N_MOD, d), lambda i: (layer, grp(i), 0, 0)),
            pl.BlockSpec((None, 6, d), lambda i: (layer, 0, 0)),
            pl.BlockSpec((tm, d // 2), lambda i: (i, ya_blk)),
            pl.BlockSpec((tm, d // 2), lambda i: (i, yb_blk)),
            pl.BlockSpec((None, d // 2, d), lambda i: (idx, 0, 0)),
            pl.BlockSpec((None, d // 2, d), lambda i: (idx, 1, 0)),
        ],
        out_specs=pl.BlockSpec((tm, d), lambda i: (i, 0)),
        out_shape=jax.ShapeDtypeStruct((n_tiles * tm, d), F32),
        compiler_params=_params("parallel"),
        name=f"mixer_out_l{layer}",
    )(h, mods, norm_g, ya, yb, w_out, w_out)


def kernel(x, c, ctx, c_ctx, mod_w, mod_b, norm_g, ffn_w_gate, ffn_w_up, ffn_w_down, ev_w_in, ev_w_out, ret_decay, pool_w, pool_scale, od_w_in, od_w_out, lam_q1, lam_k1, lam_q2, lam_k2, diff_subln_g):
    n_batch, seq, d = x.shape
    ctx_len = ctx.shape[1]
    depth = mod_w.shape[0]
    tm = ROW_TILE
    lat_rows, ctx_rows = n_batch * seq, n_batch * ctx_len
    assert seq % tm == 0 and ctx_rows % tm == 0 and n_batch + 1 <= 8
    assert d // 2 // RET_HEADS == 2 * LANES and d // DIFF_HEADS == 2 * LANES and ctx_len == POOL_TILE
    lat_tiles, all_tiles = lat_rows // tm, (lat_rows + ctx_rows) // tm
    tiles_per_batch = seq // tm
    grp = lambda i: jnp.minimum(i // tiles_per_batch, n_batch)
    rope_blk = lambda i: jnp.where(i < lat_tiles, i % tiles_per_batch, tiles_per_batch)

    cond = jnp.concatenate([c, c_ctx[None, :], jnp.zeros((8 - n_batch - 1, d), F32)], axis=0)
    mods = _modulation(cond, mod_w, mod_b)[:, :n_batch + 1].reshape(depth, n_batch + 1, N_MOD, d)

    wg, wu, wd = ffn_w_gate.astype(BF16), ffn_w_up.astype(BF16), ffn_w_down.astype(BF16)
    ev_in, ev_out, od_in, od_out = ev_w_in.astype(BF16), ev_w_out.astype(BF16), od_w_in.astype(BF16), od_w_out.astype(BF16)
    pw = pool_w.astype(BF16)
    log_decay = jax.nn.log_sigmoid(ret_decay.astype(F32))
    tabs_even = _rope_tables(seq, tm, True)
    tabs_odd = _rope_tables(seq, tm, False)

    h = jnp.concatenate([x.reshape(lat_rows, d), ctx.reshape(ctx_rows, d)], axis=0)
    for l in range(depth):
        need_ctx = l < depth - 1
        out_tiles = all_tiles if need_ctx else lat_tiles
        h = _ffn(h, mods, norm_g, wg, wu, wd, l, 0, all_tiles, grp)
        if l % 2 == 0:
            e = l // 2
            qkv = _proj_in(h, mods, norm_g, ev_in, tabs_even, l, e, True, all_tiles, grp, rope_blk)
            y = _retention(qkv, log_decay[e], n_batch, seq, ctx_len)
            pm = _pool(qkv, pw, pool_scale, e, n_batch, seq, ctx_len)
            h = _out_proj(h, mods, norm_g, y, 0, pm, 0, ev_out, l, e, out_tiles, grp)
        else:
            o = l // 2
            lam_init = 0.8 - 0.6 * math.exp(-0.3 * l)
            qkv = _proj_in(h, mods, norm_g, od_in, tabs_odd, l, o, False, all_tiles, grp, rope_blk)
            lam_vecs = jnp.stack([lam_q1[o], lam_k1[o], lam_q2[o], lam_k2[o]], axis=0)
            y = _diff_attention(qkv, lam_vecs, diff_subln_g[o][None, :], lam_init, n_batch, seq, ctx_len, need_ctx)
            h = _out_proj(h, mods, norm_g, y, 0, y, 1, od_out, l, o, out_tiles, grp)
        h = _ffn(h, mods, norm_g, wg, wu, wd, l, 1, out_tiles, grp)
    return h[:lat_rows].reshape(n_batch, seq, d)
```

```python
import functools
import math

import jax
import jax.numpy as jnp
from jax import lax
from jax.experimental import pallas as pl
from jax.experimental.pallas import tpu as pltpu

F32 = jnp.float32
BF16 = jnp.bfloat16

GRID_W = 64
ROPE_BASE = 10000.0
N_MOD = 9
RET_HEADS = 4
POOL_WINDOWS = (2, 4, 8, 16)
DIFF_HEADS = 8
NORM_EPS = 1e-6
SUBLN_EPS = 1e-5

LANES = 128
ROW_TILE = 512
FFN_CHUNK = 512
PROJ_CHUNK = 1024
MOD_CHUNK = 1024
ATTN_Q_TILE = 512
ATTN_K_TILE = 512
POOL_TILE = 256
POOL_HALO = 16
VMEM_LIMIT = 56 * 1024 * 1024


def _params(*sem):
    return pltpu.CompilerParams(dimension_semantics=sem, vmem_limit_bytes=VMEM_LIMIT)


def _silu(x):
    return x * jax.nn.sigmoid(x)


def _rms(x, g, eps=NORM_EPS):
    return x * lax.rsqrt(jnp.mean(x * x, axis=-1, keepdims=True) + eps) * g


def _dot(a, b):
    return jnp.dot(a, b, preferred_element_type=F32)


def _dot_nt(a, b):
    return lax.dot_general(a, b, (((1,), (1,)), ((), ())), preferred_element_type=F32)


def _dot_tn(a, b):
    return lax.dot_general(a, b, (((0,), (0,)), ((), ())), preferred_element_type=F32)


def _mod_kernel(a_ref, w_ref, b_ref, o_ref):
    s = _silu(a_ref[...]).astype(BF16)
    o_ref[...] = _dot(s, w_ref[...].astype(BF16)) + b_ref[...]


def _modulation(cond, mod_w, mod_b):
    depth, d, n = mod_w.shape
    tn = MOD_CHUNK
    return pl.pallas_call(
        _mod_kernel,
        grid=(depth, n // tn),
        in_specs=[
            pl.BlockSpec((8, d), lambda l, j: (0, 0)),
            pl.BlockSpec((None, d, tn), lambda l, j: (l, 0, j)),
            pl.BlockSpec((None, 1, tn), lambda l, j: (l, 0, j)),
        ],
        out_specs=pl.BlockSpec((None, 8, tn), lambda l, j: (l, 0, j)),
        out_shape=jax.ShapeDtypeStruct((depth, 8, n), F32),
        compiler_params=_params("parallel", "parallel"),
        name="adaln_modulation",
    )(cond, mod_w, mod_b.reshape(depth, 1, n))


def _ffn_kernel(h_ref, m_ref, g_ref, wg_ref, wu_ref, wd_ref, o_ref, u_ref, acc_ref, *, mod0, g0):
    k = pl.program_id(1)

    @pl.when(k == 0)
    def _():
        u = _rms(h_ref[...], g_ref[g0:g0 + 1, :]) * (1.0 + m_ref[mod0 + 1:mod0 + 2, :]) + m_ref[mod0:mod0 + 1, :]
        u_ref[...] = u.astype(BF16)
        acc_ref[...] = jnp.zeros(acc_ref.shape, F32)

    u = u_ref[...]
    a = _dot(u, wg_ref[...])
    b = _dot(u, wu_ref[...])
    acc_ref[...] += _dot((_silu(a) * b).astype(BF16), wd_ref[...])

    @pl.when(k == pl.num_programs(1) - 1)
    def _():
        gate = m_ref[mod0 + 2:mod0 + 3, :]
        o_ref[...] = h_ref[...] + (0.5 * gate) * _rms(acc_ref[...], g_ref[g0 + 1:g0 + 2, :])


def _ffn(h, mods, norm_g, wg, wu, wd, layer, half, n_tiles, grp):
    d = h.shape[1]
    f = wg.shape[-1]
    tm, th = ROW_TILE, FFN_CHUNK
    return pl.pallas_call(
        functools.partial(_ffn_kernel, mod0=6 * half, g0=4 * half),
        grid=(n_tiles, f // th),
        in_specs=[
            pl.BlockSpec((tm, d), lambda i, k: (i, 0)),
            pl.BlockSpec((None, None, N_MOD, d), lambda i, k: (layer, grp(i), 0, 0)),
            pl.BlockSpec((None, 6, d), lambda i, k: (layer, 0, 0)),
            pl.BlockSpec((None, None, d, th), lambda i, k: (layer, half, 0, k)),
            pl.BlockSpec((None, None, d, th), lambda i, k: (layer, half, 0, k)),
            pl.BlockSpec((None, None, th, d), lambda i, k: (layer, half, k, 0)),
        ],
        out_specs=pl.BlockSpec((tm, d), lambda i, k: (i, 0)),
        out_shape=jax.ShapeDtypeStruct((n_tiles * tm, d), F32),
        scratch_shapes=[pltpu.VMEM((tm, d), BF16), pltpu.VMEM((tm, d), F32)],
        compiler_params=_params("parallel", "arbitrary"),
        name=f"swiglu_l{layer}_{half}",
    )(h, mods, norm_g, wg, wu, wd)


def _proj_kernel(*refs, even, n_rope, scale):
    if even:
        h_ref, m_ref, g_ref, w_ref, c_ref, sa_ref, o_ref, u_ref = refs
    else:
        h_ref, m_ref, g_ref, w_ref, c_ref, sa_ref, sb_ref, o_ref, u_ref = refs
    n = pl.program_id(1)

    @pl.when(n == 0)
    def _():
        u = _rms(h_ref[...], g_ref[2:3, :]) * (1.0 + m_ref[4:5, :]) + m_ref[3:4, :]
        u_ref[...] = u.astype(BF16)

    r = _dot(u_ref[...], w_ref[...])

    is_rope = n < n_rope
    first = n < n_rope // 2
    sc = jnp.where(is_rope, jnp.where(first, 1.0, scale) if even else jnp.where(first, scale, 1.0), 1.0)
    cos = jnp.where(is_rope, c_ref[...], 1.0) * sc
    sin_a = jnp.where(is_rope, sa_ref[...], 0.0) * sc
    if not even:
        sin_b = jnp.where(is_rope, sb_ref[...], 0.0) * sc
    for j in range(r.shape[1] // LANES):
        xb = r[:, j * LANES:(j + 1) * LANES]
        if even:
            tb = (j % 2) * LANES
            ob = xb * cos[:, tb:tb + LANES] + pltpu.roll(xb, 64, 1) * sin_a[:, tb:tb + LANES]
        else:
            ob = xb * cos + pltpu.roll(xb, 96, 1) * sin_a + pltpu.roll(xb, 32, 1) * sin_b
        o_ref[:, j * LANES:(j + 1) * LANES] = ob.astype(BF16)


def _proj_in(h, mods, norm_g, w_in, tables, layer, idx, even, n_tiles, grp, rope_blk):
    d = h.shape[1]
    n_out = w_in.shape[-1]
    tm, tn = ROW_TILE, PROJ_CHUNK
    tw = tables[0].shape[1]
    if even:
        n_rope, scale = (2 * (d // 2)) // tn, float((d // 2 // RET_HEADS) ** -0.5)
    else:
        n_rope, scale = (2 * d) // tn, float((d // DIFF_HEADS // 2) ** -0.5 * math.log2(math.e))
    return pl.pallas_call(
        functools.partial(_proj_kernel, even=even, n_rope=n_rope, scale=scale),
        grid=(n_tiles, n_out // tn),
        in_specs=[
            pl.BlockSpec((tm, d), lambda i, n: (i, 0)),
            pl.BlockSpec((None, None, N_MOD, d), lambda i, n: (layer, grp(i), 0, 0)),
            pl.BlockSpec((None, 6, d), lambda i, n: (layer, 0, 0)),
            pl.BlockSpec((None, d, tn), lambda i, n: (idx, 0, n)),
        ] + [pl.BlockSpec((tm, tw), lambda i, n: (rope_blk(i), 0)) for _ in tables],
        out_specs=pl.BlockSpec((tm, tn), lambda i, n: (i, n)),
        out_shape=jax.ShapeDtypeStruct((n_tiles * tm, n_out), BF16),
        scratch_shapes=[pltpu.VMEM((tm, d), BF16)],
        compiler_params=_params("parallel", "arbitrary"),
        name=f"mixer_in_l{layer}",
    )(h, mods, norm_g, w_in, *tables)


def _rope_tables(seq, n_ident, even):
    t = jnp.arange(seq)
    rows = (t // GRID_W).astype(F32)
    cols = (t % GRID_W).astype(F32)
    nf = 64 if even else 32
    inv = ROPE_BASE ** (-jnp.arange(nf, dtype=F32) / nf)
    ar, ac = rows[:, None] * inv[None, :], cols[:, None] * inv[None, :]
    cr, sr, cc, sc = jnp.cos(ar), jnp.sin(ar), jnp.cos(ac), jnp.sin(ac)
    z = jnp.zeros_like(sr)
    if even:
        tabs = [jnp.concatenate([cr, cr, cc, cc], 1), jnp.concatenate([-sr, sr, -sc, sc], 1)]
    else:
        tabs = [jnp.concatenate([cr, cr, cc, cc], 1), jnp.concatenate([-sr, z, -sc, z], 1),
                jnp.concatenate([z, sr, z, sc], 1)]
    ident = [jnp.ones((n_ident, tabs[0].shape[1]), F32)] + [jnp.zeros((n_ident, tabs[0].shape[1]), F32)] * (len(tabs) - 1)
    return [jnp.concatenate([a, b], 0) for a, b in zip(tabs, ident)]


def _attn_kernel(*refs, tk, n_kt, lam_init):
    if n_kt:
        lam_ref, sg_ref, q_ref, kc_ref, vc_ref, k_ref, v_ref, o_ref, m_ref, l_ref, acc_ref, s_ref, mx_ref = refs
    else:
        lam_ref, sg_ref, q_ref, kc_ref, vc_ref, _, o_ref, m_ref, l_ref, acc_ref = refs
    dh = q_ref.shape[1] // 2
    stat_shape = m_ref.shape[1:]

    def scores(kb, mi):
        return _dot_nt(q_ref[:, mi * dh:(mi + 1) * dh], kb[:, mi * dh:(mi + 1) * dh])

    def lanes(stat, width):
        return jnp.concatenate([stat] * (width // LANES), axis=1)

    def absorb(s, row_max, vb, mi):
        m_prev = m_ref[mi]
        m_next = jnp.maximum(m_prev, row_max)
        alpha = jnp.exp2(m_prev - m_next)
        p = jnp.exp2(s - lanes(m_next, s.shape[1]))
        l_ref[mi] = alpha * l_ref[mi] + jnp.sum(p, axis=-1, keepdims=True)
        acc_ref[mi] = lanes(alpha, vb.shape[1]) * acc_ref[mi] + _dot(p.astype(BF16), vb)
        m_ref[mi] = m_next

    if n_kt:
        def qk(j, slot):
            kb = k_ref[pl.ds(pl.multiple_of(j * tk, tk), tk), :]
            for mi in range(2):
                s = scores(kb, mi)
                s_ref[slot, mi] = s
                mx_ref[slot, mi] = jnp.broadcast_to(jnp.max(s, axis=-1, keepdims=True), stat_shape)

        def pv(j, slot):
            vb = v_ref[pl.ds(pl.multiple_of(j * tk, tk), tk), :]
            for mi in range(2):
                absorb(s_ref[slot, mi], mx_ref[slot, mi], vb, mi)

        qk(0, 0)

    kc, vc = kc_ref[...], vc_ref[...]
    for mi in range(2):
        s = scores(kc, mi)
        m0 = jnp.broadcast_to(jnp.max(s, axis=-1, keepdims=True), stat_shape)
        p = jnp.exp2(s - lanes(m0, s.shape[1]))
        m_ref[mi] = m0
        l_ref[mi] = jnp.broadcast_to(jnp.sum(p, axis=-1, keepdims=True), stat_shape)
        acc_ref[mi] = _dot(p.astype(BF16), vc)

    if n_kt:
        def body(i, carry):
            j = 2 * i
            qk(j + 1, 1)
            pv(j, 0)
            qk(j + 2, 0)
            pv(j + 1, 1)
            return carry

        lax.fori_loop(0, n_kt // 2 - 1, body, 0)
        qk(n_kt - 1, 1)
        pv(n_kt - 2, 0)
        pv(n_kt - 1, 1)

    t1 = jnp.sum(lam_ref[0:1, :] * lam_ref[1:2, :], axis=-1, keepdims=True)
    t2 = jnp.sum(lam_ref[2:3, :] * lam_ref[3:4, :], axis=-1, keepdims=True)
    lam = jnp.exp(t1) - jnp.exp(t2) + lam_init
    nrep = acc_ref.shape[2] // LANES
    o = (acc_ref[0] / jnp.concatenate([l_ref[0]] * nrep, axis=1)
         - lam * (acc_ref[1] / jnp.concatenate([l_ref[1]] * nrep, axis=1)))
    o = o * lax.rsqrt(jnp.mean(o * o, axis=-1, keepdims=True) + SUBLN_EPS) * sg_ref[...] * (1.0 - lam_init)
    o_ref[...] = o.astype(BF16)


def _diff_attention(qkv, lam_vecs, subln_g, lam_init, n_batch, seq, ctx_len, need_ctx):
    rows, width = qkv.shape
    d = width // 3
    hd = d // DIFF_HEADS
    tq, tk = ATTN_Q_TILE, ATTN_K_TILE
    nq = seq // tq
    assert seq % (2 * tk) == 0
    ctx_blk0 = (n_batch * seq) // ctx_len
    scratch = lambda t: [pltpu.VMEM((2, t, LANES), F32), pltpu.VMEM((2, t, LANES), F32), pltpu.VMEM((2, t, hd), F32)]
    small = [pl.BlockSpec((4, hd // 2), lambda b, h, qi: (0, 0)),
             pl.BlockSpec((1, hd), lambda b, h, qi: (0, 0))]
    ctx_kv = [pl.BlockSpec((ctx_len, hd), lambda b, h, qi: (ctx_blk0 + b, DIFF_HEADS + h)),
              pl.BlockSpec((ctx_len, hd), lambda b, h, qi: (ctx_blk0 + b, 2 * DIFF_HEADS + h))]
    y = pl.pallas_call(
        functools.partial(_attn_kernel, tk=tk, n_kt=seq // tk, lam_init=lam_init),
        grid=(n_batch, DIFF_HEADS, nq),
        in_specs=small + [pl.BlockSpec((tq, hd), lambda b, h, qi: (b * nq + qi, h))] + ctx_kv + [
            pl.BlockSpec((seq, hd), lambda b, h, qi: (b, DIFF_HEADS + h)),
            pl.BlockSpec((seq, hd), lambda b, h, qi: (b, 2 * DIFF_HEADS + h)),
        ],
        out_specs=pl.BlockSpec((tq, hd), lambda b, h, qi: (b * nq + qi, h)),
        out_shape=jax.ShapeDtypeStruct((rows, d), BF16),
        scratch_shapes=scratch(tq) + [pltpu.VMEM((2, 2, tq, tk), F32), pltpu.VMEM((2, 2, tq, LANES), F32)],
        compiler_params=_params("parallel", "parallel", "arbitrary"),
        name="diff_attention_latent",
    )(lam_vecs, subln_g, qkv, qkv, qkv, qkv, qkv)
    if not need_ctx:
        return y
    return pl.pallas_call(
        functools.partial(_attn_kernel, tk=tk, n_kt=0, lam_init=lam_init),
        grid=(n_batch, DIFF_HEADS, 1),
        in_specs=small + [pl.BlockSpec((ctx_len, hd), lambda b, h, qi: (ctx_blk0 + b, h))] + ctx_kv + [
            pl.BlockSpec(memory_space=pl.ANY)],
        out_specs=pl.BlockSpec((ctx_len, hd), lambda b, h, qi: (ctx_blk0 + b, h)),
        out_shape=jax.ShapeDtypeStruct((rows, d), BF16),
        scratch_shapes=scratch(ctx_len),
        input_output_aliases={5: 0},
        compiler_params=_params("parallel", "parallel", "arbitrary"),
        name="diff_attention_context",
    )(lam_vecs, subln_g, qkv, qkv, qkv, y)


def _ret_fwd_kernel(lg_ref, q_ref, k_ref, v_ref, o_ref, d_ref, s_ref):
    n = pl.program_id(1)
    c = q_ref.shape[0]
    hd = q_ref.shape[1] // RET_HEADS

    @pl.when(n == 0)
    def _():
        s_ref[...] = jnp.zeros(s_ref.shape, F32)
        i = lax.broadcasted_iota(jnp.int32, (c, c), 0)
        j = lax.broadcasted_iota(jnp.int32, (c, c), 1)
        dist = (i - j).astype(F32)
        for h in range(RET_HEADS):
            fwd = jnp.exp(lg_ref[0, h] * jnp.maximum(dist, 0.0))
            bwd = jnp.exp(lg_ref[1, h] * jnp.maximum(-dist, 0.0))
            d_ref[h] = jnp.where(dist >= 0, fwd, bwd)

    pos = lax.broadcasted_iota(jnp.int32, (c, 1), 0).astype(F32)
    for h in range(RET_HEADS):
        lg = lg_ref[0, h]
        sl = slice(h * hd, (h + 1) * hd)
        qh, kh, vh = q_ref[:, sl], k_ref[:, sl], v_ref[:, sl]
        p = (_dot_nt(qh, kh) * d_ref[h]).astype(BF16)
        qd = (qh.astype(F32) * jnp.exp(lg * (pos + 1.0))).astype(BF16)
        s = s_ref[h]
        o_ref[:, sl] = _dot(p, vh) + _dot(qd, s.astype(BF16))
        kd = (kh.astype(F32) * jnp.exp(lg * (c - 1.0 - pos))).astype(BF16)
        s_ref[h] = jnp.exp(jnp.full((1, 1), lg * c, F32)) * s + _dot_tn(kd, vh)


def _ret_bwd_kernel(lg_ref, q_ref, k_ref, v_ref, g_ref, r_ref, o_ref, s_ref):
    n = pl.program_id(1)
    c = q_ref.shape[0]
    hd = q_ref.shape[1] // RET_HEADS

    @pl.when(n == 0)
    def _():
        s_ref[...] = jnp.zeros(s_ref.shape, F32)

    pos = lax.broadcasted_iota(jnp.int32, (c, 1), 0).astype(F32)
    for h in range(RET_HEADS):
        lg = lg_ref[1, h]
        sl = slice(h * hd, (h + 1) * hd)
        qh, kh, vh = q_ref[:, sl], k_ref[:, sl], v_ref[:, sl]
        qd = (qh.astype(F32) * jnp.exp(lg * (c - pos))).astype(BF16)
        s = s_ref[h]
        r = r_ref[:, sl] + _dot(qd, s.astype(BF16))
        y = r * lax.rsqrt(jnp.mean(r * r, axis=-1, keepdims=True) + NORM_EPS)
        o_ref[:, sl] = (y * _silu(g_ref[:, sl].astype(F32))).astype(BF16)
        kd = (kh.astype(F32) * jnp.exp(lg * pos)).astype(BF16)
        s_ref[h] = jnp.exp(jnp.full((1, 1), lg * c, F32)) * s + _dot_tn(kd, vh)


def _retention(qkv, log_decay, n_batch, seq, ctx_len):
    rows = qkv.shape[0]
    c = ctx_len
    w = (qkv.shape[1] // 5)
    nc = seq // c
    ctx_blk0 = (n_batch * seq) // c
    fwd_blk = lambda b, n: jnp.where(n == 0, ctx_blk0 + b, b * nc + n - 1)
    bwd_blk = lambda b, n: jnp.where(n == 0, ctx_blk0 + b, b * nc + nc - n)
    smem = pl.BlockSpec(memory_space=pltpu.SMEM)
    state = pltpu.VMEM((RET_HEADS, w // RET_HEADS, w // RET_HEADS), F32)
    r1 = pl.pallas_call(
        _ret_fwd_kernel,
        grid=(n_batch, nc + 1),
        in_specs=[smem] + [pl.BlockSpec((c, w), functools.partial(lambda b, n, j: (fwd_blk(b, n), j), j=j)) for j in range(3)],
        out_specs=pl.BlockSpec((c, w), lambda b, n: (fwd_blk(b, n), 0)),
        out_shape=jax.ShapeDtypeStruct((rows, w), F32),
        scratch_shapes=[pltpu.VMEM((RET_HEADS, c, c), F32), state],
        compiler_params=_params("parallel", "arbitrary"),
        name="retention_forward_sweep",
    )(log_decay, qkv, qkv, qkv)
    return pl.pallas_call(
        _ret_bwd_kernel,
        grid=(n_batch, nc + 1),
        in_specs=[smem] + [pl.BlockSpec((c, w), functools.partial(lambda b, n, j: (bwd_blk(b, n), j), j=j)) for j in range(4)]
        + [pl.BlockSpec((c, w), lambda b, n: (bwd_blk(b, n), 0))],
        out_specs=pl.BlockSpec((c, w), lambda b, n: (bwd_blk(b, n), 0)),
        out_shape=jax.ShapeDtypeStruct((rows, w), BF16),
        scratch_shapes=[state],
        compiler_params=_params("parallel", "arbitrary"),
        name="retention_backward_sweep",
    )(log_decay, qkv, qkv, qkv, qkv, r1)


def _pool_kernel(prev_ref, main_ref, next_ref, w_ref, sc_ref, o_ref, *, lat_tiles, tiles_per_seq, seq, ctx_len):
    i = pl.program_id(0)
    tp = main_ref.shape[0]
    gw = main_ref.shape[1] // len(POOL_WINDOWS)
    is_lat = i < lat_tiles
    t0 = jnp.where(is_lat, (i % tiles_per_seq) * tp, 0)
    length = jnp.where(is_lat, seq, ctx_len)
    n_ext = tp + 2 * POOL_HALO
    t = t0 - POOL_HALO + lax.broadcasted_iota(jnp.int32, (n_ext, 1), 0)
    valid = (t >= 0) & (t < length)
    tm = t[POOL_HALO:POOL_HALO + tp]
    for gi, win in enumerate(POOL_WINDOWS):
        sl = slice(gi * gw, (gi + 1) * gw)
        ext = jnp.concatenate([prev_ref[:, sl], main_ref[:, sl], next_ref[:, sl]], axis=0).astype(F32)
        x = jnp.where(valid, ext, 0.0)
        acc = x + pltpu.roll(x, 1, 0)
        half = 1
        while 2 * half < win:
            acc = pltpu.roll(acc, n_ext - half, 0) + pltpu.roll(acc, half, 0)
            half *= 2
        cnt = (jnp.minimum(tm + win // 2, length) - jnp.maximum(tm - win // 2, 0)).astype(F32)
        pooled = acc[POOL_HALO:POOL_HALO + tp] / cnt - x[POOL_HALO:POOL_HALO + tp]
        o_ref[:, sl] = (_dot(pooled.astype(BF16), w_ref[gi]) * sc_ref[:, sl]).astype(BF16)


def _pool(qkv, pool_w, pool_scale, idx, n_batch, seq, ctx_len):
    rows = qkv.shape[0]
    w = qkv.shape[1] // 5
    tp, halo = POOL_TILE, POOL_HALO
    assert ctx_len == tp and seq % tp == 0
    last_halo_blk = rows // halo - 1
    per = tp // halo
    return pl.pallas_call(
        functools.partial(_pool_kernel, lat_tiles=(n_batch * seq) // tp, tiles_per_seq=seq // tp, seq=seq, ctx_len=ctx_len),
        grid=(rows // tp,),
        in_specs=[
            pl.BlockSpec((halo, w), lambda i: (jnp.maximum(i * per - 1, 0), 4)),
            pl.BlockSpec((tp, w), lambda i: (i, 4)),
            pl.BlockSpec((halo, w), lambda i: (jnp.minimum((i + 1) * per, last_halo_blk), 4)),
            pl.BlockSpec((None,) + pool_w.shape[1:], lambda i: (idx, 0, 0, 0)),
            pl.BlockSpec((None, 1, w), lambda i: (idx, 0, 0)),
        ],
        out_specs=pl.BlockSpec((tp, w), lambda i: (i, 0)),
        out_shape=jax.ShapeDtypeStruct((rows, w), BF16),
        compiler_params=_params("parallel"),
        name="multiscale_pool",
    )(qkv, qkv, qkv, pool_w, pool_scale.reshape(pool_scale.shape[0], 1, w))


def _out_kernel(h_ref, m_ref, g_ref, ya_ref, yb_ref, wa_ref, wb_ref, o_ref):
    y = _dot(ya_ref[...], wa_ref[...]) + _dot(yb_ref[...], wb_ref[...])
    o_ref[...] = h_ref[...] + m_ref[5:6, :] * _rms(y, g_ref[3:4, :])


def _out_proj(h, mods, norm_g, ya, ya_blk, yb, yb_blk, w_out, layer, idx, n_tiles, grp):
    d = h.shape[1]
    tm = ROW_TILE
    return pl.pallas_call(
        _out_kernel,
        grid=(n_tiles,),
        in_specs=[
            pl.BlockSpec((tm, d), lambda i: (i, 0)),
            pl.BlockSpec((None, None, N_MOD, d), lambda i: (layer, grp(i), 0, 0)),
            pl.BlockSpec((None, 6, d), lambda i: (layer, 0, 0)),
            pl.BlockSpec((tm, d // 2), lambda i: (i, ya_blk)),
            pl.BlockSpec((tm, d // 2), lambda i: (i, yb_blk)),
            pl.BlockSpec((None, d // 2, d), lambda i: (idx, 0, 0)),
            pl.BlockSpec((None, d // 2, d), lambda i: (idx, 1, 0)),
        ],
        out_specs=pl.BlockSpec((tm, d), lambda i: (i, 0)),
        out_shape=jax.ShapeDtypeStruct((n_tiles * tm, d), F32),
        compiler_params=_params("parallel"),
        name=f"mixer_out_l{layer}",
    )(h, mods, norm_g, ya, yb, w_out, w_out)


def kernel(x, c, ctx, c_ctx, mod_w, mod_b, norm_g, ffn_w_gate, ffn_w_up, ffn_w_down, ev_w_in, ev_w_out, ret_decay, pool_w, pool_scale, od_w_in, od_w_out, lam_q1, lam_k1, lam_q2, lam_k2, diff_subln_g):
    n_batch, seq, d = x.shape
    ctx_len = ctx.shape[1]
    depth = mod_w.shape[0]
    tm = ROW_TILE
    lat_rows, ctx_rows = n_batch * seq, n_batch * ctx_len
    assert seq % tm == 0 and ctx_rows % tm == 0 and n_batch + 1 <= 8
    assert d // 2 // RET_HEADS == 2 * LANES and d // DIFF_HEADS == 2 * LANES and ctx_len == POOL_TILE
    lat_tiles, all_tiles = lat_rows // tm, (lat_rows + ctx_rows) // tm
    tiles_per_batch = seq // tm
    grp = lambda i: jnp.minimum(i // tiles_per_batch, n_batch)
    rope_blk = lambda i: jnp.where(i < lat_tiles, i % tiles_per_batch, tiles_per_batch)

    cond = jnp.concatenate([c, c_ctx[None, :], jnp.zeros((8 - n_batch - 1, d), F32)], axis=0)
    mods = _modulation(cond, mod_w, mod_b)[:, :n_batch + 1].reshape(depth, n_batch + 1, N_MOD, d)

    wg, wu, wd = ffn_w_gate.astype(BF16), ffn_w_up.astype(BF16), ffn_w_down.astype(BF16)
    ev_in, ev_out, od_in, od_out = ev_w_in.astype(BF16), ev_w_out.astype(BF16), od_w_in.astype(BF16), od_w_out.astype(BF16)
    pw = pool_w.astype(BF16)
    log_decay = jax.nn.log_sigmoid(ret_decay.astype(F32))
    tabs_even = _rope_tables(seq, tm, True)
    tabs_odd = _rope_tables(seq, tm, False)

    h = jnp.concatenate([x.reshape(lat_rows, d), ctx.reshape(ctx_rows, d)], axis=0)
    for l in range(depth):
        need_ctx = l < depth - 1
        out_tiles = all_tiles if need_ctx else lat_tiles
        h = _ffn(h, mods, norm_g, wg, wu, wd, l, 0, all_tiles, grp)
        if l % 2 == 0:
            e = l // 2
            qkv = _proj_in(h, mods, norm_g, ev_in, tabs_even, l, e, True, all_tiles, grp, rope_blk)
            y = _retention(qkv, log_decay[e], n_batch, seq, ctx_len)
            pm = _pool(qkv, pw, pool_scale, e, n_batch, seq, ctx_len)
            h = _out_proj(h, mods, norm_g, y, 0, pm, 0, ev_out, l, e, out_tiles, grp)
        else:
            o = l // 2
            lam_init = 0.8 - 0.6 * math.exp(-0.3 * l)
            qkv = _proj_in(h, mods, norm_g, od_in, tabs_odd, l, o, False, all_tiles, grp, rope_blk)
            lam_vecs = jnp.stack([lam_q1[o], lam_k1[o], lam_q2[o], lam_k2[o]], axis=0)
            y = _diff_attention(qkv, lam_vecs, diff_subln_g[o][None, :], lam_init, n_batch, seq, ctx_len, need_ctx)
            h = _out_proj(h, mods, norm_g, y, 0, y, 1, od_out, l, o, out_tiles, grp)
        h = _ffn(h, mods, norm_g, wg, wu, wd, l, 1, out_tiles, grp)
    return h[:lat_rows].reshape(n_batch, seq, d)
```

```python
import functools
import math

import jax
import jax.numpy as jnp
from jax import lax
from jax.experimental import pallas as pl
from jax.experimental.pallas import tpu as pltpu

F32 = jnp.float32
BF16 = jnp.bfloat16

GRID_W = 64
ROPE_BASE = 10000.0
N_MOD = 9
RET_HEADS = 4
POOL_WINDOWS = (2, 4, 8, 16)
DIFF_HEADS = 8
NORM_EPS = 1e-6
SUBLN_EPS = 1e-5

LANES = 128
ROW_TILE = 512
FFN_CHUNK = 512
PROJ_CHUNK = 1024
MOD_CHUNK = 1024
ATTN_Q_TILE = 512
ATTN_K_TILE = 512
POOL_TILE = 256
POOL_HALO = 16
VMEM_LIMIT = 56 * 1024 * 1024


def _params(*sem):
    return pltpu.CompilerParams(dimension_semantics=sem, vmem_limit_bytes=VMEM_LIMIT)


def _silu(x):
    return x * jax.nn.sigmoid(x)


def _rms(x, g, eps=NORM_EPS):
    return x * lax.rsqrt(jnp.mean(x * x, axis=-1, keepdims=True) + eps) * g


def _dot(a, b):
    return jnp.dot(a, b, preferred_element_type=F32)


def _dot_nt(a, b):
    return lax.dot_general(a, b, (((1,), (1,)), ((), ())), preferred_element_type=F32)


def _dot_tn(a, b):
    return lax.dot_general(a, b, (((0,), (0,)), ((), ())), preferred_element_type=F32)


def _mod_kernel(a_ref, w_ref, b_ref, o_ref):
    s = _silu(a_ref[...]).astype(BF16)
    o_ref[...] = _dot(s, w_ref[...].astype(BF16)) + b_ref[...]


def _modulation(cond, mod_w, mod_b):
    depth, d, n = mod_w.shape
    tn = MOD_CHUNK
    return pl.pallas_call(
        _mod_kernel,
        grid=(depth, n // tn),
        in_specs=[
            pl.BlockSpec((8, d), lambda l, j: (0, 0)),
            pl.BlockSpec((None, d, tn), lambda l, j: (l, 0, j)),
            pl.BlockSpec((None, 1, tn), lambda l, j: (l, 0, j)),
        ],
        out_specs=pl.BlockSpec((None, 8, tn), lambda l, j: (l, 0, j)),
        out_shape=jax.ShapeDtypeStruct((depth, 8, n), F32),
        compiler_params=_params("parallel", "parallel"),
        name="adaln_modulation",
    )(cond, mod_w, mod_b.reshape(depth, 1, n))


def _ffn_kernel(h_ref, m_ref, g_ref, wg_ref, wu_ref, wd_ref, o_ref, u_ref, acc_ref, *, mod0, g0):
    k = pl.program_id(1)

    @pl.when(k == 0)
    def _():
        u = _rms(h_ref[...], g_ref[g0:g0 + 1, :]) * (1.0 + m_ref[mod0 + 1:mod0 + 2, :]) + m_ref[mod0:mod0 + 1, :]
        u_ref[...] = u.astype(BF16)
        acc_ref[...] = jnp.zeros(acc_ref.shape, F32)

    u = u_ref[...]
    a = _dot(u, wg_ref[...])
    b = _dot(u, wu_ref[...])
    acc_ref[...] += _dot((_silu(a) * b).astype(BF16), wd_ref[...])

    @pl.when(k == pl.num_programs(1) - 1)
    def _():
        gate = m_ref[mod0 + 2:mod0 + 3, :]
        o_ref[...] = h_ref[...] + (0.5 * gate) * _rms(acc_ref[...], g_ref[g0 + 1:g0 + 2, :])


def _ffn(h, mods, norm_g, wg, wu, wd, layer, half, n_tiles, grp):
    d = h.shape[1]
    f = wg.shape[-1]
    tm, th = ROW_TILE, FFN_CHUNK
    return pl.pallas_call(
        functools.partial(_ffn_kernel, mod0=6 * half, g0=4 * half),
        grid=(n_tiles, f // th),
        in_specs=[
            pl.BlockSpec((tm, d), lambda i, k: (i, 0)),
            pl.BlockSpec((None, None, N_MOD, d), lambda i, k: (layer, grp(i), 0, 0)),
            pl.BlockSpec((None, 6, d), lambda i, k: (layer, 0, 0)),
            pl.BlockSpec((None, None, d, th), lambda i, k: (layer, half, 0, k)),
            pl.BlockSpec((None, None, d, th), lambda i, k: (layer, half, 0, k)),
            pl.BlockSpec((None, None, th, d), lambda i, k: (layer, half, k, 0)),
        ],
        out_specs=pl.BlockSpec((tm, d), lambda i, k: (i, 0)),
        out_shape=jax.ShapeDtypeStruct((n_tiles * tm, d), F32),
        scratch_shapes=[pltpu.VMEM((tm, d), BF16), pltpu.VMEM((tm, d), F32)],
        compiler_params=_params("parallel", "arbitrary"),
        name=f"swiglu_l{layer}_{half}",
    )(h, mods, norm_g, wg, wu, wd)


def _proj_kernel(*refs, even, n_rope, scale):
    if even:
        h_ref, m_ref, g_ref, w_ref, c_ref, sa_ref, o_ref, u_ref = refs
    else:
        h_ref, m_ref, g_ref, w_ref, c_ref, sa_ref, sb_ref, o_ref, u_ref = refs
    n = pl.program_id(1)

    @pl.when(n == 0)
    def _():
        u = _rms(h_ref[...], g_ref[2:3, :]) * (1.0 + m_ref[4:5, :]) + m_ref[3:4, :]
        u_ref[...] = u.astype(BF16)

    r = _dot(u_ref[...], w_ref[...])

    is_rope = n < n_rope
    first = n < n_rope // 2
    sc = jnp.where(is_rope, jnp.where(first, 1.0, scale) if even else jnp.where(first, scale, 1.0), 1.0)
    cos = jnp.where(is_rope, c_ref[...], 1.0) * sc
    sin_a = jnp.where(is_rope, sa_ref[...], 0.0) * sc
    if not even:
        sin_b = jnp.where(is_rope, sb_ref[...], 0.0) * sc
    for j in range(r.shape[1] // LANES):
        xb = r[:, j * LANES:(j + 1) * LANES]
        if even:
            tb = (j % 2) * LANES
            ob = xb * cos[:, tb:tb + LANES] + pltpu.roll(xb, 64, 1) * sin_a[:, tb:tb + LANES]
        else:
            ob = xb * cos + pltpu.roll(xb, 96, 1) * sin_a + pltpu.roll(xb, 32, 1) * sin_b
        o_ref[:, j * LANES:(j + 1) * LANES] = ob.astype(BF16)


def _proj_in(h, mods, norm_g, w_in, tables, layer, idx, even, n_tiles, grp, rope_blk):
    d = h.shape[1]
    n_out = w_in.shape[-1]
    tm, tn = ROW_TILE, PROJ_CHUNK
    tw = tables[0].shape[1]
    if even:
        n_rope, scale = (2 * (d // 2)) // tn, float((d // 2 // RET_HEADS) ** -0.5)
    else:
        n_rope, scale = (2 * d) // tn, float((d // DIFF_HEADS // 2) ** -0.5 * math.log2(math.e))
    return pl.pallas_call(
        functools.partial(_proj_kernel, even=even, n_rope=n_rope, scale=scale),
        grid=(n_tiles, n_out // tn),
        in_specs=[
            pl.BlockSpec((tm, d), lambda i, n: (i, 0)),
            pl.BlockSpec((None, None, N_MOD, d), lambda i, n: (layer, grp(i), 0, 0)),
            pl.BlockSpec((None, 6, d), lambda i, n: (layer, 0, 0)),
            pl.BlockSpec((None, d, tn), lambda i, n: (idx, 0, n)),
        ] + [pl.BlockSpec((tm, tw), lambda i, n: (rope_blk(i), 0)) for _ in tables],
        out_specs=pl.BlockSpec((tm, tn), lambda i, n: (i, n)),
        out_shape=jax.ShapeDtypeStruct((n_tiles * tm, n_out), BF16),
        scratch_shapes=[pltpu.VMEM((tm, d), BF16)],
        compiler_params=_params("parallel", "arbitrary"),
        name=f"mixer_in_l{layer}",
    )(h, mods, norm_g, w_in, *tables)


def _rope_tables(seq, n_ident, even):
    t = jnp.arange(seq)
    rows = (t // GRID_W).astype(F32)
    cols = (t % GRID_W).astype(F32)
    nf = 64 if even else 32
    inv = ROPE_BASE ** (-jnp.arange(nf, dtype=F32) / nf)
    ar, ac = rows[:, None] * inv[None, :], cols[:, None] * inv[None, :]
    cr, sr, cc, sc = jnp.cos(ar), jnp.sin(ar), jnp.cos(ac), jnp.sin(ac)
    z = jnp.zeros_like(sr)
    if even:
        tabs = [jnp.concatenate([cr, cr, cc, cc], 1), jnp.concatenate([-sr, sr, -sc, sc], 1)]
    else:
        tabs = [jnp.concatenate([cr, cr, cc, cc], 1), jnp.concatenate([-sr, z, -sc, z], 1),
                jnp.concatenate([z, sr, z, sc], 1)]
    ident = [jnp.ones((n_ident, tabs[0].shape[1]), F32)] + [jnp.zeros((n_ident, tabs[0].shape[1]), F32)] * (len(tabs) - 1)
    return [jnp.concatenate([a, b], 0) for a, b in zip(tabs, ident)]


def _attn_kernel(*refs, tk, n_kt, lam_init):
    if n_kt:
        lam_ref, sg_ref, q_ref, kc_ref, vc_ref, k_ref, vt_ref, o_ref, m_ref, l_ref, acc_ref, s_ref, mx_ref = refs
    else:
        lam_ref, sg_ref, q_ref, kc_ref, vc_ref, _, o_ref, m_ref, l_ref, acc_ref = refs
    dh = q_ref.shape[1] // 2
    tq = q_ref.shape[0]

    def scores_t(kb, mi):
        return _dot_nt(kb[:, mi * dh:(mi + 1) * dh], q_ref[:, mi * dh:(mi + 1) * dh])

    def absorb(s_t, col_max, pv_t, mi):
        m_prev = m_ref[mi]
        m_next = jnp.maximum(m_prev, col_max)
        alpha = jnp.exp2(m_prev - m_next)
        p_t = jnp.exp2(s_t - m_next)
        l_ref[mi] = alpha * l_ref[mi] + jnp.sum(p_t, axis=0, keepdims=True)
        acc_ref[mi] = alpha * acc_ref[mi] + pv_t(p_t.astype(BF16))
        m_ref[mi] = m_next

    if n_kt:
        def qk(j, slot):
            kb = k_ref[pl.ds(pl.multiple_of(j * tk, tk), tk), :]
            for mi in range(2):
                s_t = scores_t(kb, mi)
                s_ref[slot, mi] = s_t
                mx_ref[slot, mi] = jnp.max(s_t, axis=0, keepdims=True)

        def pv(j, slot):
            vt = vt_ref[j]
            for mi in range(2):
                absorb(s_ref[slot, mi], mx_ref[slot, mi], lambda p: _dot(vt, p), mi)

        qk(0, 0)

    kc, vc = kc_ref[...], vc_ref[...]
    for mi in range(2):
        s_t = scores_t(kc, mi)
        m0 = jnp.max(s_t, axis=0, keepdims=True)
        p_t = jnp.exp2(s_t - m0)
        m_ref[mi] = m0
        l_ref[mi] = jnp.sum(p_t, axis=0, keepdims=True)
        acc_ref[mi] = _dot_tn(vc, p_t.astype(BF16))

    if n_kt:
        def body(i, carry):
            j = 2 * i
            qk(j + 1, 1)
            pv(j, 0)
            qk(j + 2, 0)
            pv(j + 1, 1)
            return carry

        lax.fori_loop(0, n_kt // 2 - 1, body, 0)
        qk(n_kt - 1, 1)
        pv(n_kt - 2, 0)
        pv(n_kt - 1, 1)

    t1 = jnp.sum(lam_ref[0:1, :] * lam_ref[1:2, :], axis=-1, keepdims=True)
    t2 = jnp.sum(lam_ref[2:3, :] * lam_ref[3:4, :], axis=-1, keepdims=True)
    lam = jnp.exp(t1) - jnp.exp(t2) + lam_init
    o_t = acc_ref[0] * (1.0 / l_ref[0]) - acc_ref[1] * (lam / l_ref[1])
    gain = jnp.concatenate([sg_ref[...]] * (tq // LANES), axis=1) * (1.0 - lam_init)
    o_t = o_t * lax.rsqrt(jnp.mean(o_t * o_t, axis=0, keepdims=True) + SUBLN_EPS) * gain
    o_ref[...] = o_t.T.astype(BF16)


def _transpose_kernel(x_ref, o_ref):
    o_ref[...] = x_ref[...].astype(F32).T.astype(BF16)


def _diff_attention(qkv, lam_vecs, subln_g, lam_init, n_batch, seq, ctx_len, need_ctx):
    rows, width = qkv.shape
    d = width // 3
    hd = d // DIFF_HEADS
    tq, tk = ATTN_Q_TILE, ATTN_K_TILE
    nq = seq // tq
    assert seq % (2 * tk) == 0
    ctx_blk0 = (n_batch * seq) // ctx_len
    nkt = seq // tk
    scratch = lambda t: [pltpu.VMEM((2, 1, t), F32), pltpu.VMEM((2, 1, t), F32), pltpu.VMEM((2, hd, t), F32)]
    small = [pl.BlockSpec((4, hd // 2), lambda b, h, qi: (0, 0)),
             pl.BlockSpec((hd, LANES), lambda b, h, qi: (0, 0))]
    ctx_kv = [pl.BlockSpec((ctx_len, hd), lambda b, h, qi: (ctx_blk0 + b, DIFF_HEADS + h)),
              pl.BlockSpec((ctx_len, hd), lambda b, h, qi: (ctx_blk0 + b, 2 * DIFF_HEADS + h))]
    v_t = pl.pallas_call(
        _transpose_kernel,
        grid=(n_batch, DIFF_HEADS, nkt),
        in_specs=[pl.BlockSpec((tk, hd), lambda b, h, j: (b * nkt + j, 2 * DIFF_HEADS + h))],
        out_specs=pl.BlockSpec((None, hd, tk), lambda b, h, j: ((b * DIFF_HEADS + h) * nkt + j, 0, 0)),
        out_shape=jax.ShapeDtypeStruct((n_batch * DIFF_HEADS * nkt, hd, tk), BF16),
        compiler_params=_params("parallel", "parallel", "parallel"),
        name="value_transpose",
    )(qkv)
    y = pl.pallas_call(
        functools.partial(_attn_kernel, tk=tk, n_kt=nkt, lam_init=lam_init),
        grid=(n_batch, DIFF_HEADS, nq),
        in_specs=small + [pl.BlockSpec((tq, hd), lambda b, h, qi: (b * nq + qi, h))] + ctx_kv + [
            pl.BlockSpec((seq, hd), lambda b, h, qi: (b, DIFF_HEADS + h)),
            pl.BlockSpec((nkt, hd, tk), lambda b, h, qi: (b * DIFF_HEADS + h, 0, 0)),
        ],
        out_specs=pl.BlockSpec((tq, hd), lambda b, h, qi: (b * nq + qi, h)),
        out_shape=jax.ShapeDtypeStruct((rows, d), BF16),
        scratch_shapes=scratch(tq) + [pltpu.VMEM((2, 2, tk, tq), F32), pltpu.VMEM((2, 2, 1, tq), F32)],
        compiler_params=_params("parallel", "parallel", "arbitrary"),
        name="diff_attention_latent",
    )(lam_vecs, subln_g, qkv, qkv, qkv, qkv, v_t)
    if not need_ctx:
        return y
    return pl.pallas_call(
        functools.partial(_attn_kernel, tk=tk, n_kt=0, lam_init=lam_init),
        grid=(n_batch, DIFF_HEADS, 1),
        in_specs=small + [pl.BlockSpec((ctx_len, hd), lambda b, h, qi: (ctx_blk0 + b, h))] + ctx_kv + [
            pl.BlockSpec(memory_space=pl.ANY)],
        out_specs=pl.BlockSpec((ctx_len, hd), lambda b, h, qi: (ctx_blk0 + b, h)),
        out_shape=jax.ShapeDtypeStruct((rows, d), BF16),
        scratch_shapes=scratch(ctx_len),
        input_output_aliases={5: 0},
        compiler_params=_params("parallel", "parallel", "arbitrary"),
        name="diff_attention_context",
    )(lam_vecs, subln_g, qkv, qkv, qkv, y)


def _ret_fwd_kernel(lg_ref, q_ref, k_ref, v_ref, o_ref, d_ref, s_ref):
    n = pl.program_id(1)
    c = q_ref.shape[0]
    hd = q_ref.shape[1] // RET_HEADS

    @pl.when(n == 0)
    def _():
        s_ref[...] = jnp.zeros(s_ref.shape, F32)
        i = lax.broadcasted_iota(jnp.int32, (c, c), 0)
        j = lax.broadcasted_iota(jnp.int32, (c, c), 1)
        dist = (i - j).astype(F32)
        for h in range(RET_HEADS):
            fwd = jnp.exp(lg_ref[0, h] * jnp.maximum(dist, 0.0))
            bwd = jnp.exp(lg_ref[1, h] * jnp.maximum(-dist, 0.0))
            d_ref[h] = jnp.where(dist >= 0, fwd, bwd)

    pos = lax.broadcasted_iota(jnp.int32, (c, 1), 0).astype(F32)
    for h in range(RET_HEADS):
        lg = lg_ref[0, h]
        sl = slice(h * hd, (h + 1) * hd)
        qh, kh, vh = q_ref[:, sl], k_ref[:, sl], v_ref[:, sl]
        p = (_dot_nt(qh, kh) * d_ref[h]).astype(BF16)
        qd = (qh.astype(F32) * jnp.exp(lg * (pos + 1.0))).astype(BF16)
        s = s_ref[h]
        o_ref[:, sl] = _dot(p, vh) + _dot(qd, s.astype(BF16))
        kd = (kh.astype(F32) * jnp.exp(lg * (c - 1.0 - pos))).astype(BF16)
        s_ref[h] = jnp.exp(jnp.full((1, 1), lg * c, F32)) * s + _dot_tn(kd, vh)


def _ret_bwd_kernel(lg_ref, q_ref, k_ref, v_ref, g_ref, r_ref, o_ref, s_ref):
    n = pl.program_id(1)
    c = q_ref.shape[0]
    hd = q_ref.shape[1] // RET_HEADS

    @pl.when(n == 0)
    def _():
        s_ref[...] = jnp.zeros(s_ref.shape, F32)

    pos = lax.broadcasted_iota(jnp.int32, (c, 1), 0).astype(F32)
    for h in range(RET_HEADS):
        lg = lg_ref[1, h]
        sl = slice(h * hd, (h + 1) * hd)
        qh, kh, vh = q_ref[:, sl], k_ref[:, sl], v_ref[:, sl]
        qd = (qh.astype(F32) * jnp.exp(lg * (c - pos))).astype(BF16)
        s = s_ref[h]
        r = r_ref[:, sl] + _dot(qd, s.astype(BF16))
        y = r * lax.rsqrt(jnp.mean(r * r, axis=-1, keepdims=True) + NORM_EPS)
        o_ref[:, sl] = (y * _silu(g_ref[:, sl].astype(F32))).astype(BF16)
        kd = (kh.astype(F32) * jnp.exp(lg * pos)).astype(BF16)
        s_ref[h] = jnp.exp(jnp.full((1, 1), lg * c, F32)) * s + _dot_tn(kd, vh)


def _retention(qkv, log_decay, n_batch, seq, ctx_len):
    rows = qkv.shape[0]
    c = ctx_len
    w = (qkv.shape[1] // 5)
    nc = seq // c
    ctx_blk0 = (n_batch * seq) // c
    fwd_blk = lambda b, n: jnp.where(n == 0, ctx_blk0 + b, b * nc + n - 1)
    bwd_blk = lambda b, n: jnp.where(n == 0, ctx_blk0 + b, b * nc + nc - n)
    smem = pl.BlockSpec(memory_space=pltpu.SMEM)
    state = pltpu.VMEM((RET_HEADS, w // RET_HEADS, w // RET_HEADS), F32)
    r1 = pl.pallas_call(
        _ret_fwd_kernel,
        grid=(n_batch, nc + 1),
        in_specs=[smem] + [pl.BlockSpec((c, w), functools.partial(lambda b, n, j: (fwd_blk(b, n), j), j=j)) for j in range(3)],
        out_specs=pl.BlockSpec((c, w), lambda b, n: (fwd_blk(b, n), 0)),
        out_shape=jax.ShapeDtypeStruct((rows, w), F32),
        scratch_shapes=[pltpu.VMEM((RET_HEADS, c, c), F32), state],
        compiler_params=_params("parallel", "arbitrary"),
        name="retention_forward_sweep",
    )(log_decay, qkv, qkv, qkv)
    return pl.pallas_call(
        _ret_bwd_kernel,
        grid=(n_batch, nc + 1),
        in_specs=[smem] + [pl.BlockSpec((c, w), functools.partial(lambda b, n, j: (bwd_blk(b, n), j), j=j)) for j in range(4)]
        + [pl.BlockSpec((c, w), lambda b, n: (bwd_blk(b, n), 0))],
        out_specs=pl.BlockSpec((c, w), lambda b, n: (bwd_blk(b, n), 0)),
        out_shape=jax.ShapeDtypeStruct((rows, w), BF16),
        scratch_shapes=[state],
        compiler_params=_params("parallel", "arbitrary"),
        name="retention_backward_sweep",
    )(log_decay, qkv, qkv, qkv, qkv, r1)


def _pool_kernel(prev_ref, main_ref, next_ref, w_ref, sc_ref, o_ref, *, lat_tiles, tiles_per_seq, seq, ctx_len):
    i = pl.program_id(0)
    tp = main_ref.shape[0]
    gw = main_ref.shape[1] // len(POOL_WINDOWS)
    is_lat = i < lat_tiles
    t0 = jnp.where(is_lat, (i % tiles_per_seq) * tp, 0)
    length = jnp.where(is_lat, seq, ctx_len)
    n_ext = tp + 2 * POOL_HALO
    t = t0 - POOL_HALO + lax.broadcasted_iota(jnp.int32, (n_ext, 1), 0)
    valid = (t >= 0) & (t < length)
    tm = t[POOL_HALO:POOL_HALO + tp]
    for gi, win in enumerate(POOL_WINDOWS):
        sl = slice(gi * gw, (gi + 1) * gw)
        ext = jnp.concatenate([prev_ref[:, sl], main_ref[:, sl], next_ref[:, sl]], axis=0).astype(F32)
        x = jnp.where(valid, ext, 0.0)
        acc = x + pltpu.roll(x, 1, 0)
        half = 1
        while 2 * half < win:
            acc = pltpu.roll(acc, n_ext - half, 0) + pltpu.roll(acc, half, 0)
            half *= 2
        cnt = (jnp.minimum(tm + win // 2, length) - jnp.maximum(tm - win // 2, 0)).astype(F32)
        pooled = acc[POOL_HALO:POOL_HALO + tp] / cnt - x[POOL_HALO:POOL_HALO + tp]
        o_ref[:, sl] = (_dot(pooled.astype(BF16), w_ref[gi]) * sc_ref[:, sl]).astype(BF16)


def _pool(qkv, pool_w, pool_scale, idx, n_batch, seq, ctx_len):
    rows = qkv.shape[0]
    w = qkv.shape[1] // 5
    tp, halo = POOL_TILE, POOL_HALO
    assert ctx_len == tp and seq % tp == 0
    last_halo_blk = rows // halo - 1
    per = tp // halo
    return pl.pallas_call(
        functools.partial(_pool_kernel, lat_tiles=(n_batch * seq) // tp, tiles_per_seq=seq // tp, seq=seq, ctx_len=ctx_len),
        grid=(rows // tp,),
        in_specs=[
            pl.BlockSpec((halo, w), lambda i: (jnp.maximum(i * per - 1, 0), 4)),
            pl.BlockSpec((tp, w), lambda i: (i, 4)),
            pl.BlockSpec((halo, w), lambda i: (jnp.minimum((i + 1) * per, last_halo_blk), 4)),
            pl.BlockSpec((None,) + pool_w.shape[1:], lambda i: (idx, 0, 0, 0)),
            pl.BlockSpec((None, 1, w), lambda i: (idx, 0, 0)),
        ],
        out_specs=pl.BlockSpec((tp, w), lambda i: (i, 0)),
        out_shape=jax.ShapeDtypeStruct((rows, w), BF16),
        compiler_params=_params("parallel"),
        name="multiscale_pool",
    )(qkv, qkv, qkv, pool_w, pool_scale.reshape(pool_scale.shape[0], 1, w))


def _out_kernel(h_ref, m_ref, g_ref, ya_ref, yb_ref, wa_ref, wb_ref, o_ref):
    y = _dot(ya_ref[...], wa_ref[...]) + _dot(yb_ref[...], wb_ref[...])
    o_ref[...] = h_ref[...] + m_ref[5:6, :] * _rms(y, g_ref[3:4, :])


def _out_proj(h, mods, norm_g, ya, ya_blk, yb, yb_blk, w_out, layer, idx, n_tiles, grp):
    d = h.shape[1]
    tm = ROW_TILE
    return pl.pallas_call(
        _out_kernel,
        grid=(n_tiles,),
        in_specs=[
            pl.BlockSpec((tm, d), lambda i: (i, 0)),
            pl.BlockSpec((None, None, N_MOD, d), lambda i: (layer, grp(i), 0, 0)),
            pl.BlockSpec((None, 6, d), lambda i: (layer, 0, 0)),
            pl.BlockSpec((tm, d // 2), lambda i: (i, ya_blk)),
            pl.BlockSpec((tm, d // 2), lambda i: (i, yb_blk)),
            pl.BlockSpec((None, d // 2, d), lambda i: (idx, 0, 0)),
            pl.BlockSpec((None, d // 2, d), lambda i: (idx, 1, 0)),
        ],
        out_specs=pl.BlockSpec((tm, d), lambda i: (i, 0)),
        out_shape=jax.ShapeDtypeStruct((n_tiles * tm, d), F32),
        compiler_params=_params("parallel"),
        name=f"mixer_out_l{layer}",
    )(h, mods, norm_g, ya, yb, w_out, w_out)


def kernel(x, c, ctx, c_ctx, mod_w, mod_b, norm_g, ffn_w_gate, ffn_w_up, ffn_w_down, ev_w_in, ev_w_out, ret_decay, pool_w, pool_scale, od_w_in, od_w_out, lam_q1, lam_k1, lam_q2, lam_k2, diff_subln_g):
    n_batch, seq, d = x.shape
    ctx_len = ctx.shape[1]
    depth = mod_w.shape[0]
    tm = ROW_TILE
    lat_rows, ctx_rows = n_batch * seq, n_batch * ctx_len
    assert seq % tm == 0 and ctx_rows % tm == 0 and n_batch + 1 <= 8
    assert d // 2 // RET_HEADS == 2 * LANES and d // DIFF_HEADS == 2 * LANES and ctx_len == POOL_TILE
    lat_tiles, all_tiles = lat_rows // tm, (lat_rows + ctx_rows) // tm
    tiles_per_batch = seq // tm
    grp = lambda i: jnp.minimum(i // tiles_per_batch, n_batch)
    rope_blk = lambda i: jnp.where(i < lat_tiles, i % tiles_per_batch, tiles_per_batch)

    cond = jnp.concatenate([c, c_ctx[None, :], jnp.zeros((8 - n_batch - 1, d), F32)], axis=0)
    mods = _modulation(cond, mod_w, mod_b)[:, :n_batch + 1].reshape(depth, n_batch + 1, N_MOD, d)

    wg, wu, wd = ffn_w_gate.astype(BF16), ffn_w_up.astype(BF16), ffn_w_down.astype(BF16)
    ev_in, ev_out, od_in, od_out = ev_w_in.astype(BF16), ev_w_out.astype(BF16), od_w_in.astype(BF16), od_w_out.astype(BF16)
    pw = pool_w.astype(BF16)
    log_decay = jax.nn.log_sigmoid(ret_decay.astype(F32))
    tabs_even = _rope_tables(seq, tm, True)
    tabs_odd = _rope_tables(seq, tm, False)

    h = jnp.concatenate([x.reshape(lat_rows, d), ctx.reshape(ctx_rows, d)], axis=0)
    for l in range(depth):
        need_ctx = l < depth - 1
        out_tiles = all_tiles if need_ctx else lat_tiles
        h = _ffn(h, mods, norm_g, wg, wu, wd, l, 0, all_tiles, grp)
        if l % 2 == 0:
            e = l // 2
            qkv = _proj_in(h, mods, norm_g, ev_in, tabs_even, l, e, True, all_tiles, grp, rope_blk)
            y = _retention(qkv, log_decay[e], n_batch, seq, ctx_len)
            pm = _pool(qkv, pw, pool_scale, e, n_batch, seq, ctx_len)
            h = _out_proj(h, mods, norm_g, y, 0, pm, 0, ev_out, l, e, out_tiles, grp)
        else:
            o = l // 2
            lam_init = 0.8 - 0.6 * math.exp(-0.3 * l)
            qkv = _proj_in(h, mods, norm_g, od_in, tabs_odd, l, o, False, all_tiles, grp, rope_blk)
            lam_vecs = jnp.stack([lam_q1[o], lam_k1[o], lam_q2[o], lam_k2[o]], axis=0)
            gain = jnp.broadcast_to(diff_subln_g[o][:, None], (diff_subln_g.shape[1], LANES))
            y = _diff_attention(qkv, lam_vecs, gain, lam_init, n_batch, seq, ctx_len, need_ctx)
            h = _out_proj(h, mods, norm_g, y, 0, y, 1, od_out, l, o, out_tiles, grp)
        h = _ffn(h, mods, norm_g, wg, wu, wd, l, 1, out_tiles, grp)
    return h[:lat_rows].reshape(n_batch, seq, d)
```

```python
import functools
import math

import jax
import jax.numpy as jnp
from jax import lax
from jax.experimental import pallas as pl
from jax.experimental.pallas import tpu as pltpu

F32 = jnp.float32
BF16 = jnp.bfloat16

GRID_W = 64
ROPE_BASE = 10000.0
N_MOD = 9
RET_HEADS = 4
POOL_WINDOWS = (2, 4, 8, 16)
DIFF_HEADS = 8
NORM_EPS = 1e-6
SUBLN_EPS = 1e-5

LANES = 128
ROW_TILE = 512
FFN_CHUNK = 512
PROJ_CHUNK = 1024
MOD_CHUNK = 1024
ATTN_Q_TILE = 1024
ATTN_K_TILE = 512
POOL_TILE = 256
POOL_HALO = 16
VMEM_LIMIT = 56 * 1024 * 1024


def _params(*sem):
    return pltpu.CompilerParams(dimension_semantics=sem, vmem_limit_bytes=VMEM_LIMIT)


def _silu(x):
    return x * jax.nn.sigmoid(x)


def _rms(x, g, eps=NORM_EPS):
    return x * lax.rsqrt(jnp.mean(x * x, axis=-1, keepdims=True) + eps) * g


def _dot(a, b):
    return jnp.dot(a, b, preferred_element_type=F32)


def _dot_nt(a, b):
    return lax.dot_general(a, b, (((1,), (1,)), ((), ())), preferred_element_type=F32)


def _dot_tn(a, b):
    return lax.dot_general(a, b, (((0,), (0,)), ((), ())), preferred_element_type=F32)


def _mod_kernel(a_ref, w_ref, b_ref, o_ref):
    s = _silu(a_ref[...]).astype(BF16)
    o_ref[...] = _dot(s, w_ref[...].astype(BF16)) + b_ref[...]


def _modulation(cond, mod_w, mod_b):
    depth, d, n = mod_w.shape
    tn = MOD_CHUNK
    return pl.pallas_call(
        _mod_kernel,
        grid=(depth, n // tn),
        in_specs=[
            pl.BlockSpec((8, d), lambda l, j: (0, 0)),
            pl.BlockSpec((None, d, tn), lambda l, j: (l, 0, j)),
            pl.BlockSpec((None, 1, tn), lambda l, j: (l, 0, j)),
        ],
        out_specs=pl.BlockSpec((None, 8, tn), lambda l, j: (l, 0, j)),
        out_shape=jax.ShapeDtypeStruct((depth, 8, n), F32),
        compiler_params=_params("parallel", "parallel"),
        name="adaln_modulation",
    )(cond, mod_w, mod_b.reshape(depth, 1, n))


def _ffn_kernel(h_ref, m_ref, g_ref, wg_ref, wu_ref, wd_ref, o_ref, u_ref, acc_ref, *, mod0, g0):
    k = pl.program_id(1)

    @pl.when(k == 0)
    def _():
        u = _rms(h_ref[...], g_ref[g0:g0 + 1, :]) * (1.0 + m_ref[mod0 + 1:mod0 + 2, :]) + m_ref[mod0:mod0 + 1, :]
        u_ref[...] = u.astype(BF16)
        acc_ref[...] = jnp.zeros(acc_ref.shape, F32)

    u = u_ref[...]
    a = _dot(u, wg_ref[...])
    b = _dot(u, wu_ref[...])
    acc_ref[...] += _dot((_silu(a) * b).astype(BF16), wd_ref[...])

    @pl.when(k == pl.num_programs(1) - 1)
    def _():
        gate = m_ref[mod0 + 2:mod0 + 3, :]
        o_ref[...] = h_ref[...] + (0.5 * gate) * _rms(acc_ref[...], g_ref[g0 + 1:g0 + 2, :])


def _ffn(h, mods, norm_g, wg, wu, wd, layer, half, n_tiles, grp):
    d = h.shape[1]
    f = wg.shape[-1]
    tm, th = ROW_TILE, FFN_CHUNK
    return pl.pallas_call(
        functools.partial(_ffn_kernel, mod0=6 * half, g0=4 * half),
        grid=(n_tiles, f // th),
        in_specs=[
            pl.BlockSpec((tm, d), lambda i, k: (i, 0)),
            pl.BlockSpec((None, None, N_MOD, d), lambda i, k: (layer, grp(i), 0, 0)),
            pl.BlockSpec((None, 6, d), lambda i, k: (layer, 0, 0)),
            pl.BlockSpec((None, None, d, th), lambda i, k: (layer, half, 0, k)),
            pl.BlockSpec((None, None, d, th), lambda i, k: (layer, half, 0, k)),
            pl.BlockSpec((None, None, th, d), lambda i, k: (layer, half, k, 0)),
        ],
        out_specs=pl.BlockSpec((tm, d), lambda i, k: (i, 0)),
        out_shape=jax.ShapeDtypeStruct((n_tiles * tm, d), F32),
        scratch_shapes=[pltpu.VMEM((tm, d), BF16), pltpu.VMEM((tm, d), F32)],
        compiler_params=_params("parallel", "arbitrary"),
        name=f"swiglu_l{layer}_{half}",
    )(h, mods, norm_g, wg, wu, wd)


def _proj_kernel(*refs, even, n_rope, scale):
    if even:
        h_ref, m_ref, g_ref, w_ref, c_ref, sa_ref, o_ref, u_ref = refs
    else:
        h_ref, m_ref, g_ref, w_ref, c_ref, sa_ref, sb_ref, o_ref, u_ref = refs
    n = pl.program_id(1)

    @pl.when(n == 0)
    def _():
        u = _rms(h_ref[...], g_ref[2:3, :]) * (1.0 + m_ref[4:5, :]) + m_ref[3:4, :]
        u_ref[...] = u.astype(BF16)

    r = _dot(u_ref[...], w_ref[...])

    is_rope = n < n_rope
    first = n < n_rope // 2
    sc = jnp.where(is_rope, jnp.where(first, 1.0, scale) if even else jnp.where(first, scale, 1.0), 1.0)
    cos = jnp.where(is_rope, c_ref[...], 1.0) * sc
    sin_a = jnp.where(is_rope, sa_ref[...], 0.0) * sc
    if not even:
        sin_b = jnp.where(is_rope, sb_ref[...], 0.0) * sc
    for j in range(r.shape[1] // LANES):
        xb = r[:, j * LANES:(j + 1) * LANES]
        if even:
            tb = (j % 2) * LANES
            ob = xb * cos[:, tb:tb + LANES] + pltpu.roll(xb, 64, 1) * sin_a[:, tb:tb + LANES]
        else:
            ob = xb * cos + pltpu.roll(xb, 96, 1) * sin_a + pltpu.roll(xb, 32, 1) * sin_b
        o_ref[:, j * LANES:(j + 1) * LANES] = ob.astype(BF16)


def _proj_in(h, mods, norm_g, w_in, tables, layer, idx, even, n_tiles, grp, rope_blk):
    d = h.shape[1]
    n_out = w_in.shape[-1]
    tm, tn = ROW_TILE, PROJ_CHUNK
    tw = tables[0].shape[1]
    if even:
        n_rope, scale = (2 * (d // 2)) // tn, float((d // 2 // RET_HEADS) ** -0.5)
    else:
        n_rope, scale = (2 * d) // tn, float((d // DIFF_HEADS // 2) ** -0.5 * math.log2(math.e))
    return pl.pallas_call(
        functools.partial(_proj_kernel, even=even, n_rope=n_rope, scale=scale),
        grid=(n_tiles, n_out // tn),
        in_specs=[
            pl.BlockSpec((tm, d), lambda i, n: (i, 0)),
            pl.BlockSpec((None, None, N_MOD, d), lambda i, n: (layer, grp(i), 0, 0)),
            pl.BlockSpec((None, 6, d), lambda i, n: (layer, 0, 0)),
            pl.BlockSpec((None, d, tn), lambda i, n: (idx, 0, n)),
        ] + [pl.BlockSpec((tm, tw), lambda i, n: (rope_blk(i), 0)) for _ in tables],
        out_specs=pl.BlockSpec((tm, tn), lambda i, n: (i, n)),
        out_shape=jax.ShapeDtypeStruct((n_tiles * tm, n_out), BF16),
        scratch_shapes=[pltpu.VMEM((tm, d), BF16)],
        compiler_params=_params("parallel", "arbitrary"),
        name=f"mixer_in_l{layer}",
    )(h, mods, norm_g, w_in, *tables)


def _rope_tables(seq, n_ident, even):
    t = jnp.arange(seq)
    rows = (t // GRID_W).astype(F32)
    cols = (t % GRID_W).astype(F32)
    nf = 64 if even else 32
    inv = ROPE_BASE ** (-jnp.arange(nf, dtype=F32) / nf)
    ar, ac = rows[:, None] * inv[None, :], cols[:, None] * inv[None, :]
    cr, sr, cc, sc = jnp.cos(ar), jnp.sin(ar), jnp.cos(ac), jnp.sin(ac)
    z = jnp.zeros_like(sr)
    if even:
        tabs = [jnp.concatenate([cr, cr, cc, cc], 1), jnp.concatenate([-sr, sr, -sc, sc], 1)]
    else:
        tabs = [jnp.concatenate([cr, cr, cc, cc], 1), jnp.concatenate([-sr, z, -sc, z], 1),
                jnp.concatenate([z, sr, z, sc], 1)]
    ident = [jnp.ones((n_ident, tabs[0].shape[1]), F32)] + [jnp.zeros((n_ident, tabs[0].shape[1]), F32)] * (len(tabs) - 1)
    return [jnp.concatenate([a, b], 0) for a, b in zip(tabs, ident)]


def _attn_kernel(*refs, tk, n_kt, lam_init):
    if n_kt:
        lam_ref, sg_ref, q_ref, kc_ref, vc_ref, k_ref, vt_ref, o_ref, m_ref, l_ref, acc_ref, s_ref, mx_ref = refs
    else:
        lam_ref, sg_ref, q_ref, kc_ref, vc_ref, _, o_ref, m_ref, l_ref, acc_ref = refs
    if n_kt:
        dh, tq = q_ref.shape[0] // 2, q_ref.shape[1]

        def scores_t(kb, mi):
            return _dot(kb[:, mi * dh:(mi + 1) * dh], q_ref[mi * dh:(mi + 1) * dh, :])
    else:
        dh, tq = q_ref.shape[1] // 2, q_ref.shape[0]

        def scores_t(kb, mi):
            return _dot_nt(kb[:, mi * dh:(mi + 1) * dh], q_ref[:, mi * dh:(mi + 1) * dh])

    def absorb(s_t, col_max, pv_t, mi):
        m_prev = m_ref[mi]
        m_next = jnp.maximum(m_prev, col_max)
        alpha = jnp.exp2(m_prev - m_next)
        p_t = jnp.exp2(s_t - m_next)
        l_ref[mi] = alpha * l_ref[mi] + jnp.sum(p_t, axis=0, keepdims=True)
        acc_ref[mi] = alpha * acc_ref[mi] + pv_t(p_t.astype(BF16))
        m_ref[mi] = m_next

    if n_kt:
        def qk(j, slot):
            kb = k_ref[pl.ds(pl.multiple_of(j * tk, tk), tk), :]
            for mi in range(2):
                s_t = scores_t(kb, mi)
                s_ref[slot, mi] = s_t
                mx_ref[slot, mi] = jnp.max(s_t, axis=0, keepdims=True)

        def pv(j, slot):
            vt = vt_ref[j]
            for mi in range(2):
                absorb(s_ref[slot, mi], mx_ref[slot, mi], lambda p: _dot(vt, p), mi)

        qk(0, 0)

    kc, vc = kc_ref[...], vc_ref[...]
    for mi in range(2):
        s_t = scores_t(kc, mi)
        m0 = jnp.max(s_t, axis=0, keepdims=True)
        p_t = jnp.exp2(s_t - m0)
        m_ref[mi] = m0
        l_ref[mi] = jnp.sum(p_t, axis=0, keepdims=True)
        acc_ref[mi] = _dot_tn(vc, p_t.astype(BF16))

    if n_kt:
        def body(i, carry):
            j = 2 * i
            qk(j + 1, 1)
            pv(j, 0)
            qk(j + 2, 0)
            pv(j + 1, 1)
            return carry

        lax.fori_loop(0, n_kt // 2 - 1, body, 0)
        qk(n_kt - 1, 1)
        pv(n_kt - 2, 0)
        pv(n_kt - 1, 1)

    t1 = jnp.sum(lam_ref[0:1, :] * lam_ref[1:2, :], axis=-1, keepdims=True)
    t2 = jnp.sum(lam_ref[2:3, :] * lam_ref[3:4, :], axis=-1, keepdims=True)
    lam = jnp.exp(t1) - jnp.exp(t2) + lam_init
    o_t = acc_ref[0] * (1.0 / l_ref[0]) - acc_ref[1] * (lam / l_ref[1])
    gain = jnp.concatenate([sg_ref[...]] * (tq // LANES), axis=1) * (1.0 - lam_init)
    o_t = o_t * lax.rsqrt(jnp.mean(o_t * o_t, axis=0, keepdims=True) + SUBLN_EPS) * gain
    o_ref[...] = o_t.T.astype(BF16)


def _transpose_kernel(x_ref, o_ref):
    o_ref[...] = x_ref[...].astype(F32).T.astype(BF16)


def _diff_attention(qkv, lam_vecs, subln_g, lam_init, n_batch, seq, ctx_len, need_ctx):
    rows, width = qkv.shape
    d = width // 3
    hd = d // DIFF_HEADS
    tq, tk = ATTN_Q_TILE, ATTN_K_TILE
    nq = seq // tq
    assert seq % (2 * tk) == 0
    ctx_blk0 = (n_batch * seq) // ctx_len
    nkt = seq // tk
    scratch = lambda t: [pltpu.VMEM((2, 1, t), F32), pltpu.VMEM((2, 1, t), F32), pltpu.VMEM((2, hd, t), F32)]
    small = [pl.BlockSpec((4, hd // 2), lambda b, h, qi: (0, 0)),
             pl.BlockSpec((hd, LANES), lambda b, h, qi: (0, 0))]
    ctx_kv = [pl.BlockSpec((ctx_len, hd), lambda b, h, qi: (ctx_blk0 + b, DIFF_HEADS + h)),
              pl.BlockSpec((ctx_len, hd), lambda b, h, qi: (ctx_blk0 + b, 2 * DIFF_HEADS + h))]
    v_t = pl.pallas_call(
        _transpose_kernel,
        grid=(n_batch, DIFF_HEADS, nkt),
        in_specs=[pl.BlockSpec((tk, hd), lambda b, h, j: (b * nkt + j, 2 * DIFF_HEADS + h))],
        out_specs=pl.BlockSpec((None, hd, tk), lambda b, h, j: ((b * DIFF_HEADS + h) * nkt + j, 0, 0)),
        out_shape=jax.ShapeDtypeStruct((n_batch * DIFF_HEADS * nkt, hd, tk), BF16),
        compiler_params=_params("parallel", "parallel", "parallel"),
        name="value_transpose",
    )(qkv)
    q_t = pl.pallas_call(
        _transpose_kernel,
        grid=(n_batch, DIFF_HEADS, nq),
        in_specs=[pl.BlockSpec((tq, hd), lambda b, h, j: (b * nq + j, h))],
        out_specs=pl.BlockSpec((None, hd, tq), lambda b, h, j: ((b * DIFF_HEADS + h) * nq + j, 0, 0)),
        out_shape=jax.ShapeDtypeStruct((n_batch * DIFF_HEADS * nq, hd, tq), BF16),
        compiler_params=_params("parallel", "parallel", "parallel"),
        name="query_transpose",
    )(qkv)
    y = pl.pallas_call(
        functools.partial(_attn_kernel, tk=tk, n_kt=nkt, lam_init=lam_init),
        grid=(n_batch, DIFF_HEADS, nq),
        in_specs=small + [pl.BlockSpec((None, hd, tq), lambda b, h, qi: ((b * DIFF_HEADS + h) * nq + qi, 0, 0))] + ctx_kv + [
            pl.BlockSpec((seq, hd), lambda b, h, qi: (b, DIFF_HEADS + h)),
            pl.BlockSpec((nkt, hd, tk), lambda b, h, qi: (b * DIFF_HEADS + h, 0, 0)),
        ],
        out_specs=pl.BlockSpec((tq, hd), lambda b, h, qi: (b * nq + qi, h)),
        out_shape=jax.ShapeDtypeStruct((rows, d), BF16),
        scratch_shapes=scratch(tq) + [pltpu.VMEM((2, 2, tk, tq), F32), pltpu.VMEM((2, 2, 1, tq), F32)],
        compiler_params=_params("parallel", "parallel", "arbitrary"),
        name="diff_attention_latent",
    )(lam_vecs, subln_g, q_t, qkv, qkv, qkv, v_t)
    if not need_ctx:
        return y
    return pl.pallas_call(
        functools.partial(_attn_kernel, tk=tk, n_kt=0, lam_init=lam_init),
        grid=(n_batch, DIFF_HEADS, 1),
        in_specs=small + [pl.BlockSpec((ctx_len, hd), lambda b, h, qi: (ctx_blk0 + b, h))] + ctx_kv + [
            pl.BlockSpec(memory_space=pl.ANY)],
        out_specs=pl.BlockSpec((ctx_len, hd), lambda b, h, qi: (ctx_blk0 + b, h)),
        out_shape=jax.ShapeDtypeStruct((rows, d), BF16),
        scratch_shapes=scratch(ctx_len),
        input_output_aliases={5: 0},
        compiler_params=_params("parallel", "parallel", "arbitrary"),
        name="diff_attention_context",
    )(lam_vecs, subln_g, qkv, qkv, qkv, y)


def _ret_fwd_kernel(lg_ref, q_ref, k_ref, v_ref, o_ref, d_ref, s_ref):
    n = pl.program_id(1)
    c = q_ref.shape[0]
    hd = q_ref.shape[1] // RET_HEADS

    @pl.when(n == 0)
    def _():
        s_ref[...] = jnp.zeros(s_ref.shape, F32)
        i = lax.broadcasted_iota(jnp.int32, (c, c), 0)
        j = lax.broadcasted_iota(jnp.int32, (c, c), 1)
        dist = (i - j).astype(F32)
        for h in range(RET_HEADS):
            fwd = jnp.exp(lg_ref[0, h] * jnp.maximum(dist, 0.0))
            bwd = jnp.exp(lg_ref[1, h] * jnp.maximum(-dist, 0.0))
            d_ref[h] = jnp.where(dist >= 0, fwd, bwd)

    pos = lax.broadcasted_iota(jnp.int32, (c, 1), 0).astype(F32)
    for h in range(RET_HEADS):
        lg = lg_ref[0, h]
        sl = slice(h * hd, (h + 1) * hd)
        qh, kh, vh = q_ref[:, sl], k_ref[:, sl], v_ref[:, sl]
        p = (_dot_nt(qh, kh) * d_ref[h]).astype(BF16)
        qd = (qh.astype(F32) * jnp.exp(lg * (pos + 1.0))).astype(BF16)
        s = s_ref[h]
        o_ref[:, sl] = _dot(p, vh) + _dot(qd, s.astype(BF16))
        kd = (kh.astype(F32) * jnp.exp(lg * (c - 1.0 - pos))).astype(BF16)
        s_ref[h] = jnp.exp(jnp.full((1, 1), lg * c, F32)) * s + _dot_tn(kd, vh)


def _ret_bwd_kernel(lg_ref, q_ref, k_ref, v_ref, g_ref, r_ref, o_ref, s_ref):
    n = pl.program_id(1)
    c = q_ref.shape[0]
    hd = q_ref.shape[1] // RET_HEADS

    @pl.when(n == 0)
    def _():
        s_ref[...] = jnp.zeros(s_ref.shape, F32)

    pos = lax.broadcasted_iota(jnp.int32, (c, 1), 0).astype(F32)
    for h in range(RET_HEADS):
        lg = lg_ref[1, h]
        sl = slice(h * hd, (h + 1) * hd)
        qh, kh, vh = q_ref[:, sl], k_ref[:, sl], v_ref[:, sl]
        qd = (qh.astype(F32) * jnp.exp(lg * (c - pos))).astype(BF16)
        s = s_ref[h]
        r = r_ref[:, sl] + _dot(qd, s.astype(BF16))
        y = r * lax.rsqrt(jnp.mean(r * r, axis=-1, keepdims=True) + NORM_EPS)
        o_ref[:, sl] = (y * _silu(g_ref[:, sl].astype(F32))).astype(BF16)
        kd = (kh.astype(F32) * jnp.exp(lg * pos)).astype(BF16)
        s_ref[h] = jnp.exp(jnp.full((1, 1), lg * c, F32)) * s + _dot_tn(kd, vh)


def _retention(qkv, log_decay, n_batch, seq, ctx_len):
    rows = qkv.shape[0]
    c = ctx_len
    w = (qkv.shape[1] // 5)
    nc = seq // c
    ctx_blk0 = (n_batch * seq) // c
    fwd_blk = lambda b, n: jnp.where(n == 0, ctx_blk0 + b, b * nc + n - 1)
    bwd_blk = lambda b, n: jnp.where(n == 0, ctx_blk0 + b, b * nc + nc - n)
    smem = pl.BlockSpec(memory_space=pltpu.SMEM)
    state = pltpu.VMEM((RET_HEADS, w // RET_HEADS, w // RET_HEADS), F32)
    r1 = pl.pallas_call(
        _ret_fwd_kernel,
        grid=(n_batch, nc + 1),
        in_specs=[smem] + [pl.BlockSpec((c, w), functools.partial(lambda b, n, j: (fwd_blk(b, n), j), j=j)) for j in range(3)],
        out_specs=pl.BlockSpec((c, w), lambda b, n: (fwd_blk(b, n), 0)),
        out_shape=jax.ShapeDtypeStruct((rows, w), F32),
        scratch_shapes=[pltpu.VMEM((RET_HEADS, c, c), F32), state],
        compiler_params=_params("parallel", "arbitrary"),
        name="retention_forward_sweep",
    )(log_decay, qkv, qkv, qkv)
    return pl.pallas_call(
        _ret_bwd_kernel,
        grid=(n_batch, nc + 1),
        in_specs=[smem] + [pl.BlockSpec((c, w), functools.partial(lambda b, n, j: (bwd_blk(b, n), j), j=j)) for j in range(4)]
        + [pl.BlockSpec((c, w), lambda b, n: (bwd_blk(b, n), 0))],
        out_specs=pl.BlockSpec((c, w), lambda b, n: (bwd_blk(b, n), 0)),
        out_shape=jax.ShapeDtypeStruct((rows, w), BF16),
        scratch_shapes=[state],
        compiler_params=_params("parallel", "arbitrary"),
        name="retention_backward_sweep",
    )(log_decay, qkv, qkv, qkv, qkv, r1)


def _pool_kernel(prev_ref, main_ref, next_ref, w_ref, sc_ref, o_ref, *, lat_tiles, tiles_per_seq, seq, ctx_len):
    i = pl.program_id(0)
    tp = main_ref.shape[0]
    gw = main_ref.shape[1] // len(POOL_WINDOWS)
    is_lat = i < lat_tiles
    t0 = jnp.where(is_lat, (i % tiles_per_seq) * tp, 0)
    length = jnp.where(is_lat, seq, ctx_len)
    n_ext = tp + 2 * POOL_HALO
    t = t0 - POOL_HALO + lax.broadcasted_iota(jnp.int32, (n_ext, 1), 0)
    valid = (t >= 0) & (t < length)
    tm = t[POOL_HALO:POOL_HALO + tp]
    for gi, win in enumerate(POOL_WINDOWS):
        sl = slice(gi * gw, (gi + 1) * gw)
        ext = jnp.concatenate([prev_ref[:, sl], main_ref[:, sl], next_ref[:, sl]], axis=0).astype(F32)
        x = jnp.where(valid, ext, 0.0)
        acc = x + pltpu.roll(x, 1, 0)
        half = 1
        while 2 * half < win:
            acc = pltpu.roll(acc, n_ext - half, 0) + pltpu.roll(acc, half, 0)
            half *= 2
        cnt = (jnp.minimum(tm + win // 2, length) - jnp.maximum(tm - win // 2, 0)).astype(F32)
        pooled = acc[POOL_HALO:POOL_HALO + tp] / cnt - x[POOL_HALO:POOL_HALO + tp]
        o_ref[:, sl] = (_dot(pooled.astype(BF16), w_ref[gi]) * sc_ref[:, sl]).astype(BF16)


def _pool(qkv, pool_w, pool_scale, idx, n_batch, seq, ctx_len):
    rows = qkv.shape[0]
    w = qkv.shape[1] // 5
    tp, halo = POOL_TILE, POOL_HALO
    assert ctx_len == tp and seq % tp == 0
    last_halo_blk = rows // halo - 1
    per = tp // halo
    return pl.pallas_call(
        functools.partial(_pool_kernel, lat_tiles=(n_batch * seq) // tp, tiles_per_seq=seq // tp, seq=seq, ctx_len=ctx_len),
        grid=(rows // tp,),
        in_specs=[
            pl.BlockSpec((halo, w), lambda i: (jnp.maximum(i * per - 1, 0), 4)),
            pl.BlockSpec((tp, w), lambda i: (i, 4)),
            pl.BlockSpec((halo, w), lambda i: (jnp.minimum((i + 1) * per, last_halo_blk), 4)),
            pl.BlockSpec((None,) + pool_w.shape[1:], lambda i: (idx, 0, 0, 0)),
            pl.BlockSpec((None, 1, w), lambda i: (idx, 0, 0)),
        ],
        out_specs=pl.BlockSpec((tp, w), lambda i: (i, 0)),
        out_shape=jax.ShapeDtypeStruct((rows, w), BF16),
        compiler_params=_params("parallel"),
        name="multiscale_pool",
    )(qkv, qkv, qkv, pool_w, pool_scale.reshape(pool_scale.shape[0], 1, w))


def _out_kernel(h_ref, m_ref, g_ref, ya_ref, yb_ref, wa_ref, wb_ref, o_ref):
    y = _dot(ya_ref[...], wa_ref[...]) + _dot(yb_ref[...], wb_ref[...])
    o_ref[...] = h_ref[...] + m_ref[5:6, :] * _rms(y, g_ref[3:4, :])


def _out_proj(h, mods, norm_g, ya, ya_blk, yb, yb_blk, w_out, layer, idx, n_tiles, grp):
    d = h.shape[1]
    tm = ROW_TILE
    return pl.pallas_call(
        _out_kernel,
        grid=(n_tiles,),
        in_specs=[
            pl.BlockSpec((tm, d), lambda i: (i, 0)),
            pl.BlockSpec((None, None, N_MOD, d), lambda i: (layer, grp(i), 0, 0)),
            pl.BlockSpec((None, 6, d), lambda i: (layer, 0, 0)),
            pl.BlockSpec((tm, d // 2), lambda i: (i, ya_blk)),
            pl.BlockSpec((tm, d // 2), lambda i: (i, yb_blk)),
            pl.BlockSpec((None, d // 2, d), lambda i: (idx, 0, 0)),
            pl.BlockSpec((None, d // 2, d), lambda i: (idx, 1, 0)),
        ],
        out_specs=pl.BlockSpec((tm, d), lambda i: (i, 0)),
        out_shape=jax.ShapeDtypeStruct((n_tiles * tm, d), F32),
        compiler_params=_params("parallel"),
        name=f"mixer_out_l{layer}",
    )(h, mods, norm_g, ya, yb, w_out, w_out)


def kernel(x, c, ctx, c_ctx, mod_w, mod_b, norm_g, ffn_w_gate, ffn_w_up, ffn_w_down, ev_w_in, ev_w_out, ret_decay, pool_w, pool_scale, od_w_in, od_w_out, lam_q1, lam_k1, lam_q2, lam_k2, diff_subln_g):
    n_batch, seq, d = x.shape
    ctx_len = ctx.shape[1]
    depth = mod_w.shape[0]
    tm = ROW_TILE
    lat_rows, ctx_rows = n_batch * seq, n_batch * ctx_len
    assert seq % tm == 0 and ctx_rows % tm == 0 and n_batch + 1 <= 8
    assert d // 2 // RET_HEADS == 2 * LANES and d // DIFF_HEADS == 2 * LANES and ctx_len == POOL_TILE
    lat_tiles, all_tiles = lat_rows // tm, (lat_rows + ctx_rows) // tm
    tiles_per_batch = seq // tm
    grp = lambda i: jnp.minimum(i // tiles_per_batch, n_batch)
    rope_blk = lambda i: jnp.where(i < lat_tiles, i % tiles_per_batch, tiles_per_batch)

    cond = jnp.concatenate([c, c_ctx[None, :], jnp.zeros((8 - n_batch - 1, d), F32)], axis=0)
    mods = _modulation(cond, mod_w, mod_b)[:, :n_batch + 1].reshape(depth, n_batch + 1, N_MOD, d)

    wg, wu, wd = ffn_w_gate.astype(BF16), ffn_w_up.astype(BF16), ffn_w_down.astype(BF16)
    ev_in, ev_out, od_in, od_out = ev_w_in.astype(BF16), ev_w_out.astype(BF16), od_w_in.astype(BF16), od_w_out.astype(BF16)
    pw = pool_w.astype(BF16)
    log_decay = jax.nn.log_sigmoid(ret_decay.astype(F32))
    tabs_even = _rope_tables(seq, tm, True)
    tabs_odd = _rope_tables(seq, tm, False)

    h = jnp.concatenate([x.reshape(lat_rows, d), ctx.reshape(ctx_rows, d)], axis=0)
    for l in range(depth):
        need_ctx = l < depth - 1
        out_tiles = all_tiles if need_ctx else lat_tiles
        h = _ffn(h, mods, norm_g, wg, wu, wd, l, 0, all_tiles, grp)
        if l % 2 == 0:
            e = l // 2
            qkv = _proj_in(h, mods, norm_g, ev_in, tabs_even, l, e, True, all_tiles, grp, rope_blk)
            y = _retention(qkv, log_decay[e], n_batch, seq, ctx_len)
            pm = _pool(qkv, pw, pool_scale, e, n_batch, seq, ctx_len)
            h = _out_proj(h, mods, norm_g, y, 0, pm, 0, ev_out, l, e, out_tiles, grp)
        else:
            o = l // 2
            lam_init = 0.8 - 0.6 * math.exp(-0.3 * l)
            qkv = _proj_in(h, mods, norm_g, od_in, tabs_odd, l, o, False, all_tiles, grp, rope_blk)
            lam_vecs = jnp.stack([lam_q1[o], lam_k1[o], lam_q2[o], lam_k2[o]], axis=0)
            gain = jnp.broadcast_to(diff_subln_g[o][:, None], (diff_subln_g.shape[1], LANES))
            y = _diff_attention(qkv, lam_vecs, gain, lam_init, n_batch, seq, ctx_len, need_ctx)
            h = _out_proj(h, mods, norm_g, y, 0, y, 1, od_out, l, o, out_tiles, grp)
        h = _ffn(h, mods, norm_g, wg, wu, wd, l, 1, out_tiles, grp)
    return h[:lat_rows].reshape(n_batch, seq, d)
```

```python
import functools
import math

import jax
import jax.numpy as jnp
from jax import lax
from jax.experimental import pallas as pl
from jax.experimental.pallas import tpu as pltpu

F32 = jnp.float32
BF16 = jnp.bfloat16

GRID_W = 64
ROPE_BASE = 10000.0
N_MOD = 9
RET_HEADS = 4
POOL_WINDOWS = (2, 4, 8, 16)
DIFF_HEADS = 8
NORM_EPS = 1e-6
SUBLN_EPS = 1e-5

LANES = 128
ROW_TILE = 512
WIDE_ROW_TILE = 1024
FFN_CHUNK = 512
PROJ_CHUNK = 1024
MOD_CHUNK = 1024
ATTN_Q_TILE = 1024
ATTN_K_TILE = 512
POOL_TILE = 256
POOL_HALO = 16
VMEM_LIMIT = 56 * 1024 * 1024
FFN_VMEM_LIMIT = 60 * 1024 * 1024


def _params(*sem):
    return pltpu.CompilerParams(dimension_semantics=sem, vmem_limit_bytes=VMEM_LIMIT)


def _silu(x):
    return x * jax.nn.sigmoid(x)


def _rms(x, g, eps=NORM_EPS):
    return x * lax.rsqrt(jnp.mean(x * x, axis=-1, keepdims=True) + eps) * g


def _dot(a, b):
    return jnp.dot(a, b, preferred_element_type=F32)


def _dot_nt(a, b):
    return lax.dot_general(a, b, (((1,), (1,)), ((), ())), preferred_element_type=F32)


def _dot_tn(a, b):
    return lax.dot_general(a, b, (((0,), (0,)), ((), ())), preferred_element_type=F32)


def _mod_kernel(a_ref, w_ref, b_ref, o_ref):
    s = _silu(a_ref[...]).astype(BF16)
    o_ref[...] = _dot(s, w_ref[...].astype(BF16)) + b_ref[...]


def _modulation(cond, mod_w, mod_b):
    depth, d, n = mod_w.shape
    tn = MOD_CHUNK
    return pl.pallas_call(
        _mod_kernel,
        grid=(depth, n // tn),
        in_specs=[
            pl.BlockSpec((8, d), lambda l, j: (0, 0)),
            pl.BlockSpec((None, d, tn), lambda l, j: (l, 0, j)),
            pl.BlockSpec((None, 1, tn), lambda l, j: (l, 0, j)),
        ],
        out_specs=pl.BlockSpec((None, 8, tn), lambda l, j: (l, 0, j)),
        out_shape=jax.ShapeDtypeStruct((depth, 8, n), F32),
        compiler_params=_params("parallel", "parallel"),
        name="adaln_modulation",
    )(cond, mod_w, mod_b.reshape(depth, 1, n))


def _ffn_kernel(h_ref, m_ref, g_ref, wg_ref, wu_ref, wd_ref, o_ref, u_ref, *, mod0, g0):
    k = pl.program_id(1)
    sub = ROW_TILE

    @pl.when(k == 0)
    def _():
        u = _rms(h_ref[...], g_ref[g0:g0 + 1, :]) * (1.0 + m_ref[mod0 + 1:mod0 + 2, :]) + m_ref[mod0:mod0 + 1, :]
        u_ref[...] = u.astype(BF16)
        o_ref[...] = jnp.zeros(o_ref.shape, F32)

    for s in range(h_ref.shape[0] // sub):
        rows = slice(s * sub, (s + 1) * sub)
        u = u_ref[rows, :]
        a = _dot(u, wg_ref[...])
        b = _dot(u, wu_ref[...])
        o_ref[rows, :] += _dot((_silu(a) * b).astype(BF16), wd_ref[...])

    @pl.when(k == pl.num_programs(1) - 1)
    def _():
        gate = m_ref[mod0 + 2:mod0 + 3, :]
        o_ref[...] = h_ref[...] + (0.5 * gate) * _rms(o_ref[...], g_ref[g0 + 1:g0 + 2, :])


def _ffn(h, mods, norm_g, wg, wu, wd, layer, half, out_rows, grp):
    d = h.shape[1]
    f = wg.shape[-1]
    tm, th = WIDE_ROW_TILE, FFN_CHUNK
    return pl.pallas_call(
        functools.partial(_ffn_kernel, mod0=6 * half, g0=4 * half),
        grid=(pl.cdiv(out_rows, tm), f // th),
        in_specs=[
            pl.BlockSpec((tm, d), lambda i, k: (i, 0)),
            pl.BlockSpec((None, None, N_MOD, d), lambda i, k: (layer, grp(i), 0, 0)),
            pl.BlockSpec((None, 6, d), lambda i, k: (layer, 0, 0)),
            pl.BlockSpec((None, None, d, th), lambda i, k: (layer, half, 0, k)),
            pl.BlockSpec((None, None, d, th), lambda i, k: (layer, half, 0, k)),
            pl.BlockSpec((None, None, th, d), lambda i, k: (layer, half, k, 0)),
        ],
        out_specs=pl.BlockSpec((tm, d), lambda i, k: (i, 0)),
        out_shape=jax.ShapeDtypeStruct((out_rows, d), F32),
        scratch_shapes=[pltpu.VMEM((tm, d), BF16)],
        compiler_params=pltpu.CompilerParams(dimension_semantics=("parallel", "arbitrary"),
                                             vmem_limit_bytes=FFN_VMEM_LIMIT),
        name=f"swiglu_l{layer}_{half}",
    )(h, mods, norm_g, wg, wu, wd)


def _proj_kernel(*refs, even, n_rope, scale):
    if even:
        h_ref, m_ref, g_ref, w_ref, c_ref, sa_ref, o_ref, u_ref = refs
    else:
        h_ref, m_ref, g_ref, w_ref, c_ref, sa_ref, sb_ref, o_ref, u_ref = refs
    n = pl.program_id(1)

    @pl.when(n == 0)
    def _():
        u = _rms(h_ref[...], g_ref[2:3, :]) * (1.0 + m_ref[4:5, :]) + m_ref[3:4, :]
        u_ref[...] = u.astype(BF16)

    r = _dot(u_ref[...], w_ref[...])

    is_rope = n < n_rope
    first = n < n_rope // 2
    sc = jnp.where(is_rope, jnp.where(first, 1.0, scale) if even else jnp.where(first, scale, 1.0), 1.0)
    cos = jnp.where(is_rope, c_ref[...], 1.0) * sc
    sin_a = jnp.where(is_rope, sa_ref[...], 0.0) * sc
    if not even:
        sin_b = jnp.where(is_rope, sb_ref[...], 0.0) * sc
    for j in range(r.shape[1] // LANES):
        xb = r[:, j * LANES:(j + 1) * LANES]
        if even:
            tb = (j % 2) * LANES
            ob = xb * cos[:, tb:tb + LANES] + pltpu.roll(xb, 64, 1) * sin_a[:, tb:tb + LANES]
        else:
            ob = xb * cos + pltpu.roll(xb, 96, 1) * sin_a + pltpu.roll(xb, 32, 1) * sin_b
        o_ref[:, j * LANES:(j + 1) * LANES] = ob.astype(BF16)


def _proj_in(h, mods, norm_g, w_in, tables, layer, idx, even, grp, rope_blk):
    out_rows, d = h.shape
    n_out = w_in.shape[-1]
    tm, tn = WIDE_ROW_TILE, PROJ_CHUNK
    tw = tables[0].shape[1]
    if even:
        n_rope, scale = (2 * (d // 2)) // tn, float((d // 2 // RET_HEADS) ** -0.5)
    else:
        n_rope, scale = (2 * d) // tn, float((d // DIFF_HEADS // 2) ** -0.5 * math.log2(math.e))
    return pl.pallas_call(
        functools.partial(_proj_kernel, even=even, n_rope=n_rope, scale=scale),
        grid=(pl.cdiv(out_rows, tm), n_out // tn),
        in_specs=[
            pl.BlockSpec((tm, d), lambda i, n: (i, 0)),
            pl.BlockSpec((None, None, N_MOD, d), lambda i, n: (layer, grp(i), 0, 0)),
            pl.BlockSpec((None, 6, d), lambda i, n: (layer, 0, 0)),
            pl.BlockSpec((None, d, tn), lambda i, n: (idx, 0, n)),
        ] + [pl.BlockSpec((tm, tw), lambda i, n: (rope_blk(i), 0)) for _ in tables],
        out_specs=pl.BlockSpec((tm, tn), lambda i, n: (i, n)),
        out_shape=jax.ShapeDtypeStruct((out_rows, n_out), BF16),
        scratch_shapes=[pltpu.VMEM((tm, d), BF16)],
        compiler_params=_params("parallel", "arbitrary"),
        name=f"mixer_in_l{layer}",
    )(h, mods, norm_g, w_in, *tables)


def _rope_tables(seq, n_ident, even):
    t = jnp.arange(seq)
    rows = (t // GRID_W).astype(F32)
    cols = (t % GRID_W).astype(F32)
    nf = 64 if even else 32
    inv = ROPE_BASE ** (-jnp.arange(nf, dtype=F32) / nf)
    ar, ac = rows[:, None] * inv[None, :], cols[:, None] * inv[None, :]
    cr, sr, cc, sc = jnp.cos(ar), jnp.sin(ar), jnp.cos(ac), jnp.sin(ac)
    z = jnp.zeros_like(sr)
    if even:
        tabs = [jnp.concatenate([cr, cr, cc, cc], 1), jnp.concatenate([-sr, sr, -sc, sc], 1)]
    else:
        tabs = [jnp.concatenate([cr, cr, cc, cc], 1), jnp.concatenate([-sr, z, -sc, z], 1),
                jnp.concatenate([z, sr, z, sc], 1)]
    ident = [jnp.ones((n_ident, tabs[0].shape[1]), F32)] + [jnp.zeros((n_ident, tabs[0].shape[1]), F32)] * (len(tabs) - 1)
    return [jnp.concatenate([a, b], 0) for a, b in zip(tabs, ident)]


def _attn_kernel(*refs, tk, n_kt, lam_init):
    if n_kt:
        lam_ref, sg_ref, q_ref, kc_ref, vc_ref, k_ref, vt_ref, o_ref, m_ref, l_ref, acc_ref, s_ref, mx_ref = refs
    else:
        lam_ref, sg_ref, q_ref, kc_ref, vc_ref, _, o_ref, m_ref, l_ref, acc_ref = refs
    if n_kt:
        dh, tq = q_ref.shape[0] // 2, q_ref.shape[1]

        def scores_t(kb, mi):
            return _dot(kb[:, mi * dh:(mi + 1) * dh], q_ref[mi * dh:(mi + 1) * dh, :])
    else:
        dh, tq = q_ref.shape[1] // 2, q_ref.shape[0]

        def scores_t(kb, mi):
            return _dot_nt(kb[:, mi * dh:(mi + 1) * dh], q_ref[:, mi * dh:(mi + 1) * dh])

    def absorb(slot, vt, mi):
        m_prev = m_ref[mi]
        m_next = jnp.maximum(m_prev, mx_ref[slot, mi])
        alpha = jnp.exp2(m_prev - m_next)
        p_t = jnp.exp2(s_ref[slot, mi] - m_next)
        l_ref[mi] = alpha * l_ref[mi] + jnp.sum(p_t, axis=0, keepdims=True)
        acc_ref[mi] = alpha * acc_ref[mi] + _dot(vt, p_t.astype(BF16))
        m_ref[mi] = m_next

    if n_kt:
        def qk(j, slot):
            kb = k_ref[pl.ds(pl.multiple_of(j * tk, tk), tk), :]
            for mi in range(2):
                s_t = scores_t(kb, mi)
                s_ref[slot, mi] = s_t
                mx_ref[slot, mi] = jnp.max(s_t, axis=0, keepdims=True)

        def pv(j, slot):
            vt = vt_ref[j]
            for mi in range(2):
                absorb(slot, vt, mi)

        qk(0, 0)

    kc, vc = kc_ref[...], vc_ref[...]
    for mi in range(2):
        s_t = scores_t(kc, mi)
        m0 = jnp.max(s_t, axis=0, keepdims=True)
        p_t = jnp.exp2(s_t - m0)
        m_ref[mi] = m0
        l_ref[mi] = jnp.sum(p_t, axis=0, keepdims=True)
        acc_ref[mi] = _dot_tn(vc, p_t.astype(BF16))

    if n_kt:
        def body(i, carry):
            j = 2 * i
            qk(j + 1, 1)
            pv(j, 0)
            qk(j + 2, 0)
            pv(j + 1, 1)
            return carry

        lax.fori_loop(0, n_kt // 2 - 1, body, 0)
        qk(n_kt - 1, 1)
        pv(n_kt - 2, 0)
        pv(n_kt - 1, 1)

    t1 = jnp.sum(lam_ref[0:1, :] * lam_ref[1:2, :], axis=-1, keepdims=True)
    t2 = jnp.sum(lam_ref[2:3, :] * lam_ref[3:4, :], axis=-1, keepdims=True)
    lam = jnp.exp(t1) - jnp.exp(t2) + lam_init
    o_t = acc_ref[0] * (1.0 / l_ref[0]) - acc_ref[1] * (lam / l_ref[1])
    gain = jnp.concatenate([sg_ref[...]] * (tq // LANES), axis=1) * (1.0 - lam_init)
    o_t = o_t * lax.rsqrt(jnp.mean(o_t * o_t, axis=0, keepdims=True) + SUBLN_EPS) * gain
    o_ref[...] = o_t.T.astype(BF16)


def _transpose_kernel(x_ref, o_ref):
    o_ref[...] = x_ref[...].astype(F32).T.astype(BF16)


def _diff_attention(qkv, lam_vecs, subln_g, lam_init, n_batch, seq, ctx_len, need_ctx):
    rows, width = qkv.shape
    d = width // 3
    hd = d // DIFF_HEADS
    tq, tk = ATTN_Q_TILE, ATTN_K_TILE
    nq = seq // tq
    assert seq % (2 * tk) == 0
    ctx_blk0 = (n_batch * seq) // ctx_len
    nkt = seq // tk
    scratch = lambda t: [pltpu.VMEM((2, 1, t), F32), pltpu.VMEM((2, 1, t), F32), pltpu.VMEM((2, hd, t), F32)]
    small = [pl.BlockSpec((4, hd // 2), lambda b, h, qi: (0, 0)),
             pl.BlockSpec((hd, LANES), lambda b, h, qi: (0, 0))]
    ctx_kv = [pl.BlockSpec((ctx_len, hd), lambda b, h, qi: (ctx_blk0 + b, DIFF_HEADS + h)),
              pl.BlockSpec((ctx_len, hd), lambda b, h, qi: (ctx_blk0 + b, 2 * DIFF_HEADS + h))]
    def transposed(tile, col_blk, name):
        n = seq // tile
        return pl.pallas_call(
            _transpose_kernel,
            grid=(n_batch, n),
            in_specs=[pl.BlockSpec((tile, d), lambda b, j: (b * n + j, col_blk))],
            out_specs=pl.BlockSpec((None, None, d, tile), lambda b, j: (b, j, 0, 0)),
            out_shape=jax.ShapeDtypeStruct((n_batch, n, d, tile), BF16),
            compiler_params=_params("parallel", "parallel"),
            name=name,
        )(qkv)

    v_t = transposed(tk, 2, "value_transpose")
    q_t = transposed(tq, 0, "query_transpose")
    y = pl.pallas_call(
        functools.partial(_attn_kernel, tk=tk, n_kt=nkt, lam_init=lam_init),
        grid=(n_batch, DIFF_HEADS, nq),
        in_specs=small + [pl.BlockSpec((None, None, hd, tq), lambda b, h, qi: (b, qi, h, 0))] + ctx_kv + [
            pl.BlockSpec((seq, hd), lambda b, h, qi: (b, DIFF_HEADS + h)),
            pl.BlockSpec((None, nkt, hd, tk), lambda b, h, qi: (b, 0, h, 0)),
        ],
        out_specs=pl.BlockSpec((tq, hd), lambda b, h, qi: (b * nq + qi, h)),
        out_shape=jax.ShapeDtypeStruct((rows, d), BF16),
        scratch_shapes=scratch(tq) + [pltpu.VMEM((2, 2, tk, tq), F32), pltpu.VMEM((2, 2, 1, tq), F32)],
        compiler_params=_params("parallel", "parallel", "arbitrary"),
        name="diff_attention_latent",
    )(lam_vecs, subln_g, q_t, qkv, qkv, qkv, v_t)
    if not need_ctx:
        return y
    return pl.pallas_call(
        functools.partial(_attn_kernel, tk=tk, n_kt=0, lam_init=lam_init),
        grid=(n_batch, DIFF_HEADS, 1),
        in_specs=small + [pl.BlockSpec((ctx_len, hd), lambda b, h, qi: (ctx_blk0 + b, h))] + ctx_kv + [
            pl.BlockSpec(memory_space=pl.ANY)],
        out_specs=pl.BlockSpec((ctx_len, hd), lambda b, h, qi: (ctx_blk0 + b, h)),
        out_shape=jax.ShapeDtypeStruct((rows, d), BF16),
        scratch_shapes=scratch(ctx_len),
        input_output_aliases={5: 0},
        compiler_params=_params("parallel", "parallel", "arbitrary"),
        name="diff_attention_context",
    )(lam_vecs, subln_g, qkv, qkv, qkv, y)


def _ret_fwd_kernel(lg_ref, q_ref, k_ref, v_ref, o_ref, d_ref, s_ref):
    n = pl.program_id(1)
    c = q_ref.shape[0]
    hd = q_ref.shape[1] // RET_HEADS

    @pl.when(n == 0)
    def _():
        s_ref[...] = jnp.zeros(s_ref.shape, F32)
        i = lax.broadcasted_iota(jnp.int32, (c, c), 0)
        j = lax.broadcasted_iota(jnp.int32, (c, c), 1)
        dist = (i - j).astype(F32)
        for h in range(RET_HEADS):
            fwd = jnp.exp(lg_ref[0, h] * jnp.maximum(dist, 0.0))
            bwd = jnp.exp(lg_ref[1, h] * jnp.maximum(-dist, 0.0))
            d_ref[h] = jnp.where(dist >= 0, fwd, bwd)

    pos = lax.broadcasted_iota(jnp.int32, (c, 1), 0).astype(F32)
    for h in range(RET_HEADS):
        lg = lg_ref[0, h]
        sl = slice(h * hd, (h + 1) * hd)
        qh, kh, vh = q_ref[:, sl], k_ref[:, sl], v_ref[:, sl]
        p = (_dot_nt(qh, kh) * d_ref[h]).astype(BF16)
        qd = (qh.astype(F32) * jnp.exp(lg * (pos + 1.0))).astype(BF16)
        s = s_ref[h]
        o_ref[:, sl] = _dot(p, vh) + _dot(qd, s.astype(BF16))
        kd = (kh.astype(F32) * jnp.exp(lg * (c - 1.0 - pos))).astype(BF16)
        s_ref[h] = jnp.exp(jnp.full((1, 1), lg * c, F32)) * s + _dot_tn(kd, vh)


def _ret_bwd_kernel(lg_ref, q_ref, k_ref, v_ref, g_ref, r_ref, o_ref, s_ref):
    n = pl.program_id(1)
    c = q_ref.shape[0]
    hd = q_ref.shape[1] // RET_HEADS

    @pl.when(n == 0)
    def _():
        s_ref[...] = jnp.zeros(s_ref.shape, F32)

    pos = lax.broadcasted_iota(jnp.int32, (c, 1), 0).astype(F32)
    for h in range(RET_HEADS):
        lg = lg_ref[1, h]
        sl = slice(h * hd, (h + 1) * hd)
        qh, kh, vh = q_ref[:, sl], k_ref[:, sl], v_ref[:, sl]
        qd = (qh.astype(F32) * jnp.exp(lg * (c - pos))).astype(BF16)
        s = s_ref[h]
        r = r_ref[:, sl] + _dot(qd, s.astype(BF16))
        y = r * lax.rsqrt(jnp.mean(r * r, axis=-1, keepdims=True) + NORM_EPS)
        o_ref[:, sl] = (y * _silu(g_ref[:, sl].astype(F32))).astype(BF16)
        kd = (kh.astype(F32) * jnp.exp(lg * pos)).astype(BF16)
        s_ref[h] = jnp.exp(jnp.full((1, 1), lg * c, F32)) * s + _dot_tn(kd, vh)


def _retention(qkv, log_decay, n_batch, seq, ctx_len):
    rows = qkv.shape[0]
    c = ctx_len
    w = (qkv.shape[1] // 5)
    nc = seq // c
    ctx_blk0 = (n_batch * seq) // c
    fwd_blk = lambda b, n: jnp.where(n == 0, ctx_blk0 + b, b * nc + n - 1)
    bwd_blk = lambda b, n: jnp.where(n == 0, ctx_blk0 + b, b * nc + nc - n)
    smem = pl.BlockSpec(memory_space=pltpu.SMEM)
    state = pltpu.VMEM((RET_HEADS, w // RET_HEADS, w // RET_HEADS), F32)
    r1 = pl.pallas_call(
        _ret_fwd_kernel,
        grid=(n_batch, nc + 1),
        in_specs=[smem] + [pl.BlockSpec((c, w), functools.partial(lambda b, n, j: (fwd_blk(b, n), j), j=j)) for j in range(3)],
        out_specs=pl.BlockSpec((c, w), lambda b, n: (fwd_blk(b, n), 0)),
        out_shape=jax.ShapeDtypeStruct((rows, w), F32),
        scratch_shapes=[pltpu.VMEM((RET_HEADS, c, c), F32), state],
        compiler_params=_params("parallel", "arbitrary"),
        name="retention_forward_sweep",
    )(log_decay, qkv, qkv, qkv)
    return pl.pallas_call(
        _ret_bwd_kernel,
        grid=(n_batch, nc + 1),
        in_specs=[smem] + [pl.BlockSpec((c, w), functools.partial(lambda b, n, j: (bwd_blk(b, n), j), j=j)) for j in range(4)]
        + [pl.BlockSpec((c, w), lambda b, n: (bwd_blk(b, n), 0))],
        out_specs=pl.BlockSpec((c, w), lambda b, n: (bwd_blk(b, n), 0)),
        out_shape=jax.ShapeDtypeStruct((rows, w), BF16),
        scratch_shapes=[state],
        compiler_params=_params("parallel", "arbitrary"),
        name="retention_backward_sweep",
    )(log_decay, qkv, qkv, qkv, qkv, r1)


def _pool_kernel(prev_ref, main_ref, next_ref, w_ref, sc_ref, o_ref, *, lat_tiles, tiles_per_seq, seq, ctx_len):
    i = pl.program_id(0)
    tp = main_ref.shape[0]
    gw = main_ref.shape[1] // len(POOL_WINDOWS)
    is_lat = i < lat_tiles
    t0 = jnp.where(is_lat, (i % tiles_per_seq) * tp, 0)
    length = jnp.where(is_lat, seq, ctx_len)
    n_ext = tp + 2 * POOL_HALO
    t = t0 - POOL_HALO + lax.broadcasted_iota(jnp.int32, (n_ext, 1), 0)
    valid = (t >= 0) & (t < length)
    tm = t[POOL_HALO:POOL_HALO + tp]
    for gi, win in enumerate(POOL_WINDOWS):
        sl = slice(gi * gw, (gi + 1) * gw)
        ext = jnp.concatenate([prev_ref[:, sl], main_ref[:, sl], next_ref[:, sl]], axis=0).astype(F32)
        x = jnp.where(valid, ext, 0.0)
        acc = x + pltpu.roll(x, 1, 0)
        half = 1
        while 2 * half < win:
            acc = pltpu.roll(acc, n_ext - half, 0) + pltpu.roll(acc, half, 0)
            half *= 2
        cnt = (jnp.minimum(tm + win // 2, length) - jnp.maximum(tm - win // 2, 0)).astype(F32)
        pooled = acc[POOL_HALO:POOL_HALO + tp] / cnt - x[POOL_HALO:POOL_HALO + tp]
        o_ref[:, sl] = (_dot(pooled.astype(BF16), w_ref[gi]) * sc_ref[:, sl]).astype(BF16)


def _pool(qkv, pool_w, pool_scale, idx, n_batch, seq, ctx_len):
    rows = qkv.shape[0]
    w = qkv.shape[1] // 5
    tp, halo = POOL_TILE, POOL_HALO
    assert ctx_len == tp and seq % tp == 0
    last_halo_blk = rows // halo - 1
    per = tp // halo
    return pl.pallas_call(
        functools.partial(_pool_kernel, lat_tiles=(n_batch * seq) // tp, tiles_per_seq=seq // tp, seq=seq, ctx_len=ctx_len),
        grid=(rows // tp,),
        in_specs=[
            pl.BlockSpec((halo, w), lambda i: (jnp.maximum(i * per - 1, 0), 4)),
            pl.BlockSpec((tp, w), lambda i: (i, 4)),
            pl.BlockSpec((halo, w), lambda i: (jnp.minimum((i + 1) * per, last_halo_blk), 4)),
            pl.BlockSpec((None,) + pool_w.shape[1:], lambda i: (idx, 0, 0, 0)),
            pl.BlockSpec((None, 1, w), lambda i: (idx, 0, 0)),
        ],
        out_specs=pl.BlockSpec((tp, w), lambda i: (i, 0)),
        out_shape=jax.ShapeDtypeStruct((rows, w), BF16),
        compiler_params=_params("parallel"),
        name="multiscale_pool",
    )(qkv, qkv, qkv, pool_w, pool_scale.reshape(pool_scale.shape[0], 1, w))


def _out_kernel(h_ref, m_ref, g_ref, ya_ref, yb_ref, wa_ref, wb_ref, o_ref):
    y = _dot(ya_ref[...], wa_ref[...]) + _dot(yb_ref[...], wb_ref[...])
    o_ref[...] = h_ref[...] + m_ref[5:6, :] * _rms(y, g_ref[3:4, :])


def _out_proj(h, mods, norm_g, ya, ya_blk, yb, yb_blk, w_out, layer, idx, n_tiles, grp):
    d = h.shape[1]
    tm = ROW_TILE
    return pl.pallas_call(
        _out_kernel,
        grid=(n_tiles,),
        in_specs=[
            pl.BlockSpec((tm, d), lambda i: (i, 0)),
            pl.BlockSpec((None, None, N_MOD, d), lambda i: (layer, grp(i), 0, 0)),
            pl.BlockSpec((None, 6, d), lambda i: (layer, 0, 0)),
            pl.BlockSpec((tm, d // 2), lambda i: (i, ya_blk)),
            pl.BlockSpec((tm, d // 2), lambda i: (i, yb_blk)),
            pl.BlockSpec((None, d // 2, d), lambda i: (idx, 0, 0)),
            pl.BlockSpec((None, d // 2, d), lambda i: (idx, 1, 0)),
        ],
        out_specs=pl.BlockSpec((tm, d), lambda i: (i, 0)),
        out_shape=jax.ShapeDtypeStruct((n_tiles * tm, d), F32),
        compiler_params=_params("parallel"),
        name=f"mixer_out_l{layer}",
    )(h, mods, norm_g, ya, yb, w_out, w_out)


def kernel(x, c, ctx, c_ctx, mod_w, mod_b, norm_g, ffn_w_gate, ffn_w_up, ffn_w_down, ev_w_in, ev_w_out, ret_decay, pool_w, pool_scale, od_w_in, od_w_out, lam_q1, lam_k1, lam_q2, lam_k2, diff_subln_g):
    n_batch, seq, d = x.shape
    ctx_len = ctx.shape[1]
    depth = mod_w.shape[0]
    tm = ROW_TILE
    lat_rows, ctx_rows = n_batch * seq, n_batch * ctx_len
    assert seq % tm == 0 and ctx_rows % tm == 0 and n_batch + 1 <= 8
    assert d // 2 // RET_HEADS == 2 * LANES and d // DIFF_HEADS == 2 * LANES and ctx_len == POOL_TILE
    lat_tiles, all_tiles = lat_rows // tm, (lat_rows + ctx_rows) // tm
    tiles_per_batch = seq // tm
    grp = lambda i: jnp.minimum(i // tiles_per_batch, n_batch)
    assert seq % WIDE_ROW_TILE == 0
    wide_per_batch = seq // WIDE_ROW_TILE
    wide_grp = lambda i: jnp.minimum(i // wide_per_batch, n_batch)
    rope_blk = lambda i: jnp.where(i < n_batch * wide_per_batch, i % wide_per_batch, wide_per_batch)

    cond = jnp.concatenate([c, c_ctx[None, :], jnp.zeros((8 - n_batch - 1, d), F32)], axis=0)
    mods = _modulation(cond, mod_w, mod_b)[:, :n_batch + 1].reshape(depth, n_batch + 1, N_MOD, d)

    wg, wu, wd = ffn_w_gate.astype(BF16), ffn_w_up.astype(BF16), ffn_w_down.astype(BF16)
    ev_in, ev_out, od_in, od_out = ev_w_in.astype(BF16), ev_w_out.astype(BF16), od_w_in.astype(BF16), od_w_out.astype(BF16)
    pw = pool_w.astype(BF16)
    log_decay = jax.nn.log_sigmoid(ret_decay.astype(F32))
    tabs_even = _rope_tables(seq, WIDE_ROW_TILE, True)
    tabs_odd = _rope_tables(seq, WIDE_ROW_TILE, False)

    h = jnp.concatenate([x.reshape(lat_rows, d), ctx.reshape(ctx_rows, d)], axis=0)
    for l in range(depth):
        need_ctx = l < depth - 1
        out_tiles = all_tiles if need_ctx else lat_tiles
        h = _ffn(h, mods, norm_g, wg, wu, wd, l, 0, lat_rows + ctx_rows, wide_grp)
        if l % 2 == 0:
            e = l // 2
            qkv = _proj_in(h, mods, norm_g, ev_in, tabs_even, l, e, True, wide_grp, rope_blk)
            y = _retention(qkv, log_decay[e], n_batch, seq, ctx_len)
            pm = _pool(qkv, pw, pool_scale, e, n_batch, seq, ctx_len)
            h = _out_proj(h, mods, norm_g, y, 0, pm, 0, ev_out, l, e, out_tiles, grp)
        else:
            o = l // 2
            lam_init = 0.8 - 0.6 * math.exp(-0.3 * l)
            qkv = _proj_in(h, mods, norm_g, od_in, tabs_odd, l, o, False, wide_grp, rope_blk)
            lam_vecs = jnp.stack([lam_q1[o], lam_k1[o], lam_q2[o], lam_k2[o]], axis=0)
            gain = jnp.broadcast_to(diff_subln_g[o][:, None], (diff_subln_g.shape[1], LANES))
            y = _diff_attention(qkv, lam_vecs, gain, lam_init, n_batch, seq, ctx_len, need_ctx)
            h = _out_proj(h, mods, norm_g, y, 0, y, 1, od_out, l, o, out_tiles, grp)
        h = _ffn(h, mods, norm_g, wg, wu, wd, l, 1, out_tiles * tm, wide_grp)
    return h[:lat_rows].reshape(n_batch, seq, d)
```

```python
import functools
import math

import jax
import jax.numpy as jnp
from jax import lax
from jax.experimental import pallas as pl
from jax.experimental.pallas import tpu as pltpu

F32 = jnp.float32
BF16 = jnp.bfloat16

GRID_W = 64
ROPE_BASE = 10000.0
N_MOD = 9
RET_HEADS = 4
POOL_WINDOWS = (2, 4, 8, 16)
DIFF_HEADS = 8
NORM_EPS = 1e-6
SUBLN_EPS = 1e-5

LANES = 128
ROW_TILE = 512
WIDE_ROW_TILE = 1024
FFN_CHUNK = 512
NORM_STRIP = 16
PROJ_CHUNK = 1024
MOD_CHUNK = 1024
ATTN_Q_TILE = 1024
ATTN_K_TILE = 512
POOL_TILE = 256
POOL_HALO = 16
VMEM_LIMIT = 56 * 1024 * 1024
FFN_VMEM_LIMIT = 60 * 1024 * 1024


def _params(*sem):
    return pltpu.CompilerParams(dimension_semantics=sem, vmem_limit_bytes=VMEM_LIMIT)


def _silu(x):
    return x * jax.nn.sigmoid(x)


def _rms(x, g, eps=NORM_EPS):
    return x * lax.rsqrt(jnp.mean(x * x, axis=-1, keepdims=True) + eps) * g


def _modulated_norm_strips(h_ref, u_ref, g, shift, scale, zero_ref=None):
    gain = g * (1.0 + scale)

    def strip(i, carry):
        rows = pl.ds(pl.multiple_of(i * NORM_STRIP, NORM_STRIP), NORM_STRIP)
        x = h_ref[rows, :]
        inv = lax.rsqrt(jnp.mean(x * x, axis=-1, keepdims=True) + NORM_EPS)
        u_ref[rows, :] = (x * inv * gain + shift).astype(BF16)
        if zero_ref is not None:
            zero_ref[rows, :] = jnp.zeros((NORM_STRIP, zero_ref.shape[1]), F32)
        return carry

    lax.fori_loop(0, h_ref.shape[0] // NORM_STRIP, strip, 0, unroll=8)


def _dot(a, b):
    return jnp.dot(a, b, preferred_element_type=F32)


def _dot_nt(a, b):
    return lax.dot_general(a, b, (((1,), (1,)), ((), ())), preferred_element_type=F32)


def _dot_tn(a, b):
    return lax.dot_general(a, b, (((0,), (0,)), ((), ())), preferred_element_type=F32)


def _mod_kernel(a_ref, w_ref, b_ref, o_ref):
    s = _silu(a_ref[...]).astype(BF16)
    o_ref[...] = _dot(s, w_ref[...].astype(BF16)) + b_ref[...]


def _modulation(cond, mod_w, mod_b):
    depth, d, n = mod_w.shape
    tn = MOD_CHUNK
    return pl.pallas_call(
        _mod_kernel,
        grid=(depth, n // tn),
        in_specs=[
            pl.BlockSpec((8, d), lambda l, j: (0, 0)),
            pl.BlockSpec((None, d, tn), lambda l, j: (l, 0, j)),
            pl.BlockSpec((None, 1, tn), lambda l, j: (l, 0, j)),
        ],
        out_specs=pl.BlockSpec((None, 8, tn), lambda l, j: (l, 0, j)),
        out_shape=jax.ShapeDtypeStruct((depth, 8, n), F32),
        compiler_params=_params("parallel", "parallel"),
        name="adaln_modulation",
    )(cond, mod_w, mod_b.reshape(depth, 1, n))


def _ffn_kernel(h_ref, m_ref, g_ref, wg_ref, wu_ref, wd_ref, o_ref, u_ref, *, mod0, g0):
    k = pl.program_id(1)
    sub = ROW_TILE

    n_strips = h_ref.shape[0] // NORM_STRIP

    @pl.when(k == 0)
    def _():
        _modulated_norm_strips(h_ref, u_ref, g_ref[g0:g0 + 1, :], m_ref[mod0:mod0 + 1, :], m_ref[mod0 + 1:mod0 + 2, :],
                               zero_ref=o_ref)

    for s in range(h_ref.shape[0] // sub):
        rows = slice(s * sub, (s + 1) * sub)
        u = u_ref[rows, :]
        a = _dot(u, wg_ref[...])
        b = _dot(u, wu_ref[...])
        o_ref[rows, :] += _dot((_silu(a) * b).astype(BF16), wd_ref[...])

    @pl.when(k == pl.num_programs(1) - 1)
    def _():
        gain = (0.5 * m_ref[mod0 + 2:mod0 + 3, :]) * g_ref[g0 + 1:g0 + 2, :]

        for i in range(n_strips):
            rows = slice(i * NORM_STRIP, (i + 1) * NORM_STRIP)
            y = o_ref[rows, :]
            inv = lax.rsqrt(jnp.mean(y * y, axis=-1, keepdims=True) + NORM_EPS)
            o_ref[rows, :] = h_ref[rows, :] + y * inv * gain


def _ffn(h, mods, norm_g, wg, wu, wd, layer, half, out_rows, grp):
    d = h.shape[1]
    f = wg.shape[-1]
    tm, th = WIDE_ROW_TILE, FFN_CHUNK
    return pl.pallas_call(
        functools.partial(_ffn_kernel, mod0=6 * half, g0=4 * half),
        grid=(pl.cdiv(out_rows, tm), f // th),
        in_specs=[
            pl.BlockSpec((tm, d), lambda i, k: (i, 0)),
            pl.BlockSpec((None, None, N_MOD, d), lambda i, k: (layer, grp(i), 0, 0)),
            pl.BlockSpec((None, 6, d), lambda i, k: (layer, 0, 0)),
            pl.BlockSpec((None, None, d, th), lambda i, k: (layer, half, 0, k)),
            pl.BlockSpec((None, None, d, th), lambda i, k: (layer, half, 0, k)),
            pl.BlockSpec((None, None, th, d), lambda i, k: (layer, half, k, 0)),
        ],
        out_specs=pl.BlockSpec((tm, d), lambda i, k: (i, 0)),
        out_shape=jax.ShapeDtypeStruct((out_rows, d), F32),
        scratch_shapes=[pltpu.VMEM((tm, d), BF16)],
        compiler_params=pltpu.CompilerParams(dimension_semantics=("parallel", "arbitrary"),
                                             vmem_limit_bytes=FFN_VMEM_LIMIT),
        name=f"swiglu_l{layer}_{half}",
    )(h, mods, norm_g, wg, wu, wd)


def _proj_kernel(*refs, even, n_rope, scale):
    if even:
        h_ref, m_ref, g_ref, w_ref, c_ref, sa_ref, o_ref, u_ref = refs
    else:
        h_ref, m_ref, g_ref, w_ref, c_ref, sa_ref, sb_ref, o_ref, u_ref = refs
    n = pl.program_id(1)

    @pl.when(n == 0)
    def _():
        _modulated_norm_strips(h_ref, u_ref, g_ref[2:3, :], m_ref[3:4, :], m_ref[4:5, :])

    r = _dot(u_ref[...], w_ref[...])

    is_rope = n < n_rope
    first = n < n_rope // 2
    sc = jnp.where(is_rope, jnp.where(first, 1.0, scale) if even else jnp.where(first, scale, 1.0), 1.0)
    cos = jnp.where(is_rope, c_ref[...], 1.0) * sc
    sin_a = jnp.where(is_rope, sa_ref[...], 0.0) * sc
    if not even:
        sin_b = jnp.where(is_rope, sb_ref[...], 0.0) * sc
    for j in range(r.shape[1] // LANES):
        xb = r[:, j * LANES:(j + 1) * LANES]
        if even:
            tb = (j % 2) * LANES
            ob = xb * cos[:, tb:tb + LANES] + pltpu.roll(xb, 64, 1) * sin_a[:, tb:tb + LANES]
        else:
            ob = xb * cos + pltpu.roll(xb, 96, 1) * sin_a + pltpu.roll(xb, 32, 1) * sin_b
        o_ref[:, j * LANES:(j + 1) * LANES] = ob.astype(BF16)


def _proj_in(h, mods, norm_g, w_in, tables, layer, idx, even, grp, rope_blk):
    out_rows, d = h.shape
    n_out = w_in.shape[-1]
    tm, tn = WIDE_ROW_TILE, PROJ_CHUNK
    tw = tables[0].shape[1]
    if even:
        n_rope, scale = (2 * (d // 2)) // tn, float((d // 2 // RET_HEADS) ** -0.5)
    else:
        n_rope, scale = (2 * d) // tn, float((d // DIFF_HEADS // 2) ** -0.5 * math.log2(math.e))
    return pl.pallas_call(
        functools.partial(_proj_kernel, even=even, n_rope=n_rope, scale=scale),
        grid=(pl.cdiv(out_rows, tm), n_out // tn),
        in_specs=[
            pl.BlockSpec((tm, d), lambda i, n: (i, 0)),
            pl.BlockSpec((None, None, N_MOD, d), lambda i, n: (layer, grp(i), 0, 0)),
            pl.BlockSpec((None, 6, d), lambda i, n: (layer, 0, 0)),
            pl.BlockSpec((None, d, tn), lambda i, n: (idx, 0, n)),
        ] + [pl.BlockSpec((tm, tw), lambda i, n: (rope_blk(i), 0)) for _ in tables],
        out_specs=pl.BlockSpec((tm, tn), lambda i, n: (i, n)),
        out_shape=jax.ShapeDtypeStruct((out_rows, n_out), BF16),
        scratch_shapes=[pltpu.VMEM((tm, d), BF16)],
        compiler_params=_params("parallel", "arbitrary"),
        name=f"mixer_in_l{layer}",
    )(h, mods, norm_g, w_in, *tables)


def _rope_tables(seq, n_ident, even):
    t = jnp.arange(seq)
    rows = (t // GRID_W).astype(F32)
    cols = (t % GRID_W).astype(F32)
    nf = 64 if even else 32
    inv = ROPE_BASE ** (-jnp.arange(nf, dtype=F32) / nf)
    ar, ac = rows[:, None] * inv[None, :], cols[:, None] * inv[None, :]
    cr, sr, cc, sc = jnp.cos(ar), jnp.sin(ar), jnp.cos(ac), jnp.sin(ac)
    z = jnp.zeros_like(sr)
    if even:
        tabs = [jnp.concatenate([cr, cr, cc, cc], 1), jnp.concatenate([-sr, sr, -sc, sc], 1)]
    else:
        tabs = [jnp.concatenate([cr, cr, cc, cc], 1), jnp.concatenate([-sr, z, -sc, z], 1),
                jnp.concatenate([z, sr, z, sc], 1)]
    ident = [jnp.ones((n_ident, tabs[0].shape[1]), F32)] + [jnp.zeros((n_ident, tabs[0].shape[1]), F32)] * (len(tabs) - 1)
    return [jnp.concatenate([a, b], 0) for a, b in zip(tabs, ident)]


def _attn_kernel(*refs, tk, n_kt, lam_init):
    if n_kt:
        lam_ref, sg_ref, q_ref, kc_ref, vc_ref, k_ref, vt_ref, o_ref, m_ref, l_ref, acc_ref, s_ref, mx_ref = refs
    else:
        lam_ref, sg_ref, q_ref, kc_ref, vc_ref, _, o_ref, m_ref, l_ref, acc_ref = refs
    if n_kt:
        dh, tq = q_ref.shape[0] // 2, q_ref.shape[1]

        def scores_t(kb, mi):
            return _dot(kb[:, mi * dh:(mi + 1) * dh], q_ref[mi * dh:(mi + 1) * dh, :])
    else:
        dh, tq = q_ref.shape[1] // 2, q_ref.shape[0]

        def scores_t(kb, mi):
            return _dot_nt(kb[:, mi * dh:(mi + 1) * dh], q_ref[:, mi * dh:(mi + 1) * dh])

    def absorb(slot, vt, mi):
        m_prev = m_ref[mi]
        m_next = jnp.maximum(m_prev, mx_ref[slot, mi])
        alpha = jnp.exp2(m_prev - m_next)
        p_t = jnp.exp2(s_ref[slot, mi] - m_next)
        l_ref[mi] = alpha * l_ref[mi] + jnp.sum(p_t, axis=0, keepdims=True)
        acc_ref[mi] = alpha * acc_ref[mi] + _dot(vt, p_t.astype(BF16))
        m_ref[mi] = m_next

    if n_kt:
        def qk(j, slot):
            kb = k_ref[pl.ds(pl.multiple_of(j * tk, tk), tk), :]
            for mi in range(2):
                s_t = scores_t(kb, mi)
                s_ref[slot, mi] = s_t
                mx_ref[slot, mi] = jnp.max(s_t, axis=0, keepdims=True)

        def pv(j, slot):
            vt = vt_ref[j]
            for mi in range(2):
                absorb(slot, vt, mi)

        qk(0, 0)

    kc, vc = kc_ref[...], vc_ref[...]
    for mi in range(2):
        s_t = scores_t(kc, mi)
        m0 = jnp.max(s_t, axis=0, keepdims=True)
        p_t = jnp.exp2(s_t - m0)
        m_ref[mi] = m0
        l_ref[mi] = jnp.sum(p_t, axis=0, keepdims=True)
        acc_ref[mi] = _dot_tn(vc, p_t.astype(BF16))

    if n_kt:
        def body(i, carry):
            j = 2 * i
            qk(j + 1, 1)
            pv(j, 0)
            qk(j + 2, 0)
            pv(j + 1, 1)
            return carry

        lax.fori_loop(0, n_kt // 2 - 1, body, 0)
        qk(n_kt - 1, 1)
        pv(n_kt - 2, 0)
        pv(n_kt - 1, 1)

    t1 = jnp.sum(lam_ref[0:1, :] * lam_ref[1:2, :], axis=-1, keepdims=True)
    t2 = jnp.sum(lam_ref[2:3, :] * lam_ref[3:4, :], axis=-1, keepdims=True)
    lam = jnp.exp(t1) - jnp.exp(t2) + lam_init
    o_t = acc_ref[0] * (1.0 / l_ref[0]) - acc_ref[1] * (lam / l_ref[1])
    gain = jnp.concatenate([sg_ref[...]] * (tq // LANES), axis=1) * (1.0 - lam_init)
    o_t = o_t * lax.rsqrt(jnp.mean(o_t * o_t, axis=0, keepdims=True) + SUBLN_EPS) * gain
    o_ref[...] = o_t.T.astype(BF16)


def _transpose_kernel(x_ref, o_ref):
    o_ref[...] = x_ref[...].astype(F32).T.astype(BF16)


def _diff_attention(qkv, lam_vecs, subln_g, lam_init, n_batch, seq, ctx_len, need_ctx):
    rows, width = qkv.shape
    d = width // 3
    hd = d // DIFF_HEADS
    tq, tk = ATTN_Q_TILE, ATTN_K_TILE
    nq = seq // tq
    assert seq % (2 * tk) == 0
    ctx_blk0 = (n_batch * seq) // ctx_len
    nkt = seq // tk
    scratch = lambda t: [pltpu.VMEM((2, 1, t), F32), pltpu.VMEM((2, 1, t), F32), pltpu.VMEM((2, hd, t), F32)]
    small = [pl.BlockSpec((4, hd // 2), lambda b, h, qi: (0, 0)),
             pl.BlockSpec((hd, LANES), lambda b, h, qi: (0, 0))]
    ctx_kv = [pl.BlockSpec((ctx_len, hd), lambda b, h, qi: (ctx_blk0 + b, DIFF_HEADS + h)),
              pl.BlockSpec((ctx_len, hd), lambda b, h, qi: (ctx_blk0 + b, 2 * DIFF_HEADS + h))]
    def transposed(tile, col_blk, name):
        n = seq // tile
        return pl.pallas_call(
            _transpose_kernel,
            grid=(n_batch, n),
            in_specs=[pl.BlockSpec((tile, d), lambda b, j: (b * n + j, col_blk))],
            out_specs=pl.BlockSpec((None, None, d, tile), lambda b, j: (b, j, 0, 0)),
            out_shape=jax.ShapeDtypeStruct((n_batch, n, d, tile), BF16),
            compiler_params=_params("parallel", "parallel"),
            name=name,
        )(qkv)

    v_t = transposed(tk, 2, "value_transpose")
    q_t = transposed(tq, 0, "query_transpose")
    y = pl.pallas_call(
        functools.partial(_attn_kernel, tk=tk, n_kt=nkt, lam_init=lam_init),
        grid=(n_batch, DIFF_HEADS, nq),
        in_specs=small + [pl.BlockSpec((None, None, hd, tq), lambda b, h, qi: (b, qi, h, 0))] + ctx_kv + [
            pl.BlockSpec((seq, hd), lambda b, h, qi: (b, DIFF_HEADS + h)),
            pl.BlockSpec((None, nkt, hd, tk), lambda b, h, qi: (b, 0, h, 0)),
        ],
        out_specs=pl.BlockSpec((tq, hd), lambda b, h, qi: (b * nq + qi, h)),
        out_shape=jax.ShapeDtypeStruct((rows, d), BF16),
        scratch_shapes=scratch(tq) + [pltpu.VMEM((2, 2, tk, tq), F32), pltpu.VMEM((2, 2, 1, tq), F32)],
        compiler_params=_params("parallel", "parallel", "arbitrary"),
        name="diff_attention_latent",
    )(lam_vecs, subln_g, q_t, qkv, qkv, qkv, v_t)
    if not need_ctx:
        return y
    return pl.pallas_call(
        functools.partial(_attn_kernel, tk=tk, n_kt=0, lam_init=lam_init),
        grid=(n_batch, DIFF_HEADS, 1),
        in_specs=small + [pl.BlockSpec((ctx_len, hd), lambda b, h, qi: (ctx_blk0 + b, h))] + ctx_kv + [
            pl.BlockSpec(memory_space=pl.ANY)],
        out_specs=pl.BlockSpec((ctx_len, hd), lambda b, h, qi: (ctx_blk0 + b, h)),
        out_shape=jax.ShapeDtypeStruct((rows, d), BF16),
        scratch_shapes=scratch(ctx_len),
        input_output_aliases={5: 0},
        compiler_params=_params("parallel", "parallel", "arbitrary"),
        name="diff_attention_context",
    )(lam_vecs, subln_g, qkv, qkv, qkv, y)


def _ret_fwd_kernel(lg_ref, q_ref, k_ref, v_ref, o_ref, d_ref, s_ref):
    n = pl.program_id(1)
    c = q_ref.shape[0]
    hd = q_ref.shape[1] // RET_HEADS

    @pl.when(n == 0)
    def _():
        s_ref[...] = jnp.zeros(s_ref.shape, F32)
        i = lax.broadcasted_iota(jnp.int32, (c, c), 0)
        j = lax.broadcasted_iota(jnp.int32, (c, c), 1)
        dist = (i - j).astype(F32)
        for h in range(RET_HEADS):
            fwd = jnp.exp(lg_ref[0, h] * jnp.maximum(dist, 0.0))
            bwd = jnp.exp(lg_ref[1, h] * jnp.maximum(-dist, 0.0))
            d_ref[h] = jnp.where(dist >= 0, fwd, bwd)

    pos = lax.broadcasted_iota(jnp.int32, (c, 1), 0).astype(F32)
    for h in range(RET_HEADS):
        lg = lg_ref[0, h]
        sl = slice(h * hd, (h + 1) * hd)
        qh, kh, vh = q_ref[:, sl], k_ref[:, sl], v_ref[:, sl]
        p = (_dot_nt(qh, kh) * d_ref[h]).astype(BF16)
        qd = (qh.astype(F32) * jnp.exp(lg * (pos + 1.0))).astype(BF16)
        s = s_ref[h]
        o_ref[:, sl] = _dot(p, vh) + _dot(qd, s.astype(BF16))
        kd = (kh.astype(F32) * jnp.exp(lg * (c - 1.0 - pos))).astype(BF16)
        s_ref[h] = jnp.exp(jnp.full((1, 1), lg * c, F32)) * s + _dot_tn(kd, vh)


def _ret_bwd_kernel(lg_ref, q_ref, k_ref, v_ref, g_ref, r_ref, o_ref, s_ref):
    n = pl.program_id(1)
    c = q_ref.shape[0]
    hd = q_ref.shape[1] // RET_HEADS

    @pl.when(n == 0)
    def _():
        s_ref[...] = jnp.zeros(s_ref.shape, F32)

    pos = lax.broadcasted_iota(jnp.int32, (c, 1), 0).astype(F32)
    for h in range(RET_HEADS):
        lg = lg_ref[1, h]
        sl = slice(h * hd, (h + 1) * hd)
        qh, kh, vh = q_ref[:, sl], k_ref[:, sl], v_ref[:, sl]
        qd = (qh.astype(F32) * jnp.exp(lg * (c - pos))).astype(BF16)
        s = s_ref[h]
        r = r_ref[:, sl] + _dot(qd, s.astype(BF16))
        y = r * lax.rsqrt(jnp.mean(r * r, axis=-1, keepdims=True) + NORM_EPS)
        o_ref[:, sl] = (y * _silu(g_ref[:, sl].astype(F32))).astype(BF16)
        kd = (kh.astype(F32) * jnp.exp(lg * pos)).astype(BF16)
        s_ref[h] = jnp.exp(jnp.full((1, 1), lg * c, F32)) * s + _dot_tn(kd, vh)


def _retention(qkv, log_decay, n_batch, seq, ctx_len):
    rows = qkv.shape[0]
    c = ctx_len
    w = (qkv.shape[1] // 5)
    nc = seq // c
    ctx_blk0 = (n_batch * seq) // c
    fwd_blk = lambda b, n: jnp.where(n == 0, ctx_blk0 + b, b * nc + n - 1)
    bwd_blk = lambda b, n: jnp.where(n == 0, ctx_blk0 + b, b * nc + nc - n)
    smem = pl.BlockSpec(memory_space=pltpu.SMEM)
    state = pltpu.VMEM((RET_HEADS, w // RET_HEADS, w // RET_HEADS), F32)
    r1 = pl.pallas_call(
        _ret_fwd_kernel,
        grid=(n_batch, nc + 1),
        in_specs=[smem] + [pl.BlockSpec((c, w), functools.partial(lambda b, n, j: (fwd_blk(b, n), j), j=j)) for j in range(3)],
        out_specs=pl.BlockSpec((c, w), lambda b, n: (fwd_blk(b, n), 0)),
        out_shape=jax.ShapeDtypeStruct((rows, w), F32),
        scratch_shapes=[pltpu.VMEM((RET_HEADS, c, c), F32), state],
        compiler_params=_params("parallel", "arbitrary"),
        name="retention_forward_sweep",
    )(log_decay, qkv, qkv, qkv)
    return pl.pallas_call(
        _ret_bwd_kernel,
        grid=(n_batch, nc + 1),
        in_specs=[smem] + [pl.BlockSpec((c, w), functools.partial(lambda b, n, j: (bwd_blk(b, n), j), j=j)) for j in range(4)]
        + [pl.BlockSpec((c, w), lambda b, n: (bwd_blk(b, n), 0))],
        out_specs=pl.BlockSpec((c, w), lambda b, n: (bwd_blk(b, n), 0)),
        out_shape=jax.ShapeDtypeStruct((rows, w), BF16),
        scratch_shapes=[state],
        compiler_params=_params("parallel", "arbitrary"),
        name="retention_backward_sweep",
    )(log_decay, qkv, qkv, qkv, qkv, r1)


def _pool_kernel(prev_ref, main_ref, next_ref, w_ref, sc_ref, o_ref, *, lat_tiles, tiles_per_seq, seq, ctx_len):
    i = pl.program_id(0)
    tp = main_ref.shape[0]
    gw = main_ref.shape[1] // len(POOL_WINDOWS)
    is_lat = i < lat_tiles
    t0 = jnp.where(is_lat, (i % tiles_per_seq) * tp, 0)
    length = jnp.where(is_lat, seq, ctx_len)
    n_ext = tp + 2 * POOL_HALO
    t = t0 - POOL_HALO + lax.broadcasted_iota(jnp.int32, (n_ext, 1), 0)
    valid = (t >= 0) & (t < length)
    tm = t[POOL_HALO:POOL_HALO + tp]
    for gi, win in enumerate(POOL_WINDOWS):
        sl = slice(gi * gw, (gi + 1) * gw)
        ext = jnp.concatenate([prev_ref[:, sl], main_ref[:, sl], next_ref[:, sl]], axis=0).astype(F32)
        x = jnp.where(valid, ext, 0.0)
        acc = x + pltpu.roll(x, 1, 0)
        half = 1
        while 2 * half < win:
            acc = pltpu.roll(acc, n_ext - half, 0) + pltpu.roll(acc, half, 0)
            half *= 2
        cnt = (jnp.minimum(tm + win // 2, length) - jnp.maximum(tm - win // 2, 0)).astype(F32)
        pooled = acc[POOL_HALO:POOL_HALO + tp] / cnt - x[POOL_HALO:POOL_HALO + tp]
        o_ref[:, sl] = (_dot(pooled.astype(BF16), w_ref[gi]) * sc_ref[:, sl]).astype(BF16)


def _pool(qkv, pool_w, pool_scale, idx, n_batch, seq, ctx_len):
    rows = qkv.shape[0]
    w = qkv.shape[1] // 5
    tp, halo = POOL_TILE, POOL_HALO
    assert ctx_len == tp and seq % tp == 0
    last_halo_blk = rows // halo - 1
    per = tp // halo
    return pl.pallas_call(
        functools.partial(_pool_kernel, lat_tiles=(n_batch * seq) // tp, tiles_per_seq=seq // tp, seq=seq, ctx_len=ctx_len),
        grid=(rows // tp,),
        in_specs=[
            pl.BlockSpec((halo, w), lambda i: (jnp.maximum(i * per - 1, 0), 4)),
            pl.BlockSpec((tp, w), lambda i: (i, 4)),
            pl.BlockSpec((halo, w), lambda i: (jnp.minimum((i + 1) * per, last_halo_blk), 4)),
            pl.BlockSpec((None,) + pool_w.shape[1:], lambda i: (idx, 0, 0, 0)),
            pl.BlockSpec((None, 1, w), lambda i: (idx, 0, 0)),
        ],
        out_specs=pl.BlockSpec((tp, w), lambda i: (i, 0)),
        out_shape=jax.ShapeDtypeStruct((rows, w), BF16),
        compiler_params=_params("parallel"),
        name="multiscale_pool",
    )(qkv, qkv, qkv, pool_w, pool_scale.reshape(pool_scale.shape[0], 1, w))


def _out_kernel(h_ref, m_ref, g_ref, ya_ref, yb_ref, wa_ref, wb_ref, o_ref):
    y = _dot(ya_ref[...], wa_ref[...]) + _dot(yb_ref[...], wb_ref[...])
    o_ref[...] = h_ref[...] + m_ref[5:6, :] * _rms(y, g_ref[3:4, :])


def _out_proj(h, mods, norm_g, ya, ya_blk, yb, yb_blk, w_out, layer, idx, n_tiles, grp):
    d = h.shape[1]
    tm = ROW_TILE
    return pl.pallas_call(
        _out_kernel,
        grid=(n_tiles,),
        in_specs=[
            pl.BlockSpec((tm, d), lambda i: (i, 0)),
            pl.BlockSpec((None, None, N_MOD, d), lambda i: (layer, grp(i), 0, 0)),
            pl.BlockSpec((None, 6, d), lambda i: (layer, 0, 0)),
            pl.BlockSpec((tm, d // 2), lambda i: (i, ya_blk)),
            pl.BlockSpec((tm, d // 2), lambda i: (i, yb_blk)),
            pl.BlockSpec((None, d // 2, d), lambda i: (idx, 0, 0)),
            pl.BlockSpec((None, d // 2, d), lambda i: (idx, 1, 0)),
        ],
        out_specs=pl.BlockSpec((tm, d), lambda i: (i, 0)),
        out_shape=jax.ShapeDtypeStruct((n_tiles * tm, d), F32),
        compiler_params=_params("parallel"),
        name=f"mixer_out_l{layer}",
    )(h, mods, norm_g, ya, yb, w_out, w_out)


def kernel(x, c, ctx, c_ctx, mod_w, mod_b, norm_g, ffn_w_gate, ffn_w_up, ffn_w_down, ev_w_in, ev_w_out, ret_decay, pool_w, pool_scale, od_w_in, od_w_out, lam_q1, lam_k1, lam_q2, lam_k2, diff_subln_g):
    n_batch, seq, d = x.shape
    ctx_len = ctx.shape[1]
    depth = mod_w.shape[0]
    tm = ROW_TILE
    lat_rows, ctx_rows = n_batch * seq, n_batch * ctx_len
    assert seq % tm == 0 and ctx_rows % tm == 0 and n_batch + 1 <= 8
    assert d // 2 // RET_HEADS == 2 * LANES and d // DIFF_HEADS == 2 * LANES and ctx_len == POOL_TILE
    lat_tiles, all_tiles = lat_rows // tm, (lat_rows + ctx_rows) // tm
    tiles_per_batch = seq // tm
    grp = lambda i: jnp.minimum(i // tiles_per_batch, n_batch)
    assert seq % WIDE_ROW_TILE == 0
    wide_per_batch = seq // WIDE_ROW_TILE
    wide_grp = lambda i: jnp.minimum(i // wide_per_batch, n_batch)
    rope_blk = lambda i: jnp.where(i < n_batch * wide_per_batch, i % wide_per_batch, wide_per_batch)

    cond = jnp.concatenate([c, c_ctx[None, :], jnp.zeros((8 - n_batch - 1, d), F32)], axis=0)
    mods = _modulation(cond, mod_w, mod_b)[:, :n_batch + 1].reshape(depth, n_batch + 1, N_MOD, d)

    wg, wu, wd = ffn_w_gate.astype(BF16), ffn_w_up.astype(BF16), ffn_w_down.astype(BF16)
    ev_in, ev_out, od_in, od_out = ev_w_in.astype(BF16), ev_w_out.astype(BF16), od_w_in.astype(BF16), od_w_out.astype(BF16)
    pw = pool_w.astype(BF16)
    log_decay = jax.nn.log_sigmoid(ret_decay.astype(F32))
    tabs_even = _rope_tables(seq, WIDE_ROW_TILE, True)
    tabs_odd = _rope_tables(seq, WIDE_ROW_TILE, False)

    h = jnp.concatenate([x.reshape(lat_rows, d), ctx.reshape(ctx_rows, d)], axis=0)
    for l in range(depth):
        need_ctx = l < depth - 1
        out_tiles = all_tiles if need_ctx else lat_tiles
        h = _ffn(h, mods, norm_g, wg, wu, wd, l, 0, lat_rows + ctx_rows, wide_grp)
        if l % 2 == 0:
            e = l // 2
            qkv = _proj_in(h, mods, norm_g, ev_in, tabs_even, l, e, True, wide_grp, rope_blk)
            y = _retention(qkv, log_decay[e], n_batch, seq, ctx_len)
            pm = _pool(qkv, pw, pool_scale, e, n_batch, seq, ctx_len)
            h = _out_proj(h, mods, norm_g, y, 0, pm, 0, ev_out, l, e, out_tiles, grp)
        else:
            o = l // 2
            lam_init = 0.8 - 0.6 * math.exp(-0.3 * l)
            qkv = _proj_in(h, mods, norm_g, od_in, tabs_odd, l, o, False, wide_grp, rope_blk)
            lam_vecs = jnp.stack([lam_q1[o], lam_k1[o], lam_q2[o], lam_k2[o]], axis=0)
            gain = jnp.broadcast_to(diff_subln_g[o][:, None], (diff_subln_g.shape[1], LANES))
            y = _diff_attention(qkv, lam_vecs, gain, lam_init, n_batch, seq, ctx_len, need_ctx)
            h = _out_proj(h, mods, norm_g, y, 0, y, 1, od_out, l, o, out_tiles, grp)
        h = _ffn(h, mods, norm_g, wg, wu, wd, l, 1, out_tiles * tm, wide_grp)
    return h[:lat_rows].reshape(n_batch, seq, d)
```

```python
import functools
import math

import jax
import jax.numpy as jnp
from jax import lax
from jax.experimental import pallas as pl
from jax.experimental.pallas import tpu as pltpu

F32 = jnp.float32
BF16 = jnp.bfloat16

GRID_W = 64
ROPE_BASE = 10000.0
N_MOD = 9
RET_HEADS = 4
POOL_WINDOWS = (2, 4, 8, 16)
DIFF_HEADS = 8
NORM_EPS = 1e-6
SUBLN_EPS = 1e-5

LANES = 128
ROW_TILE = 512
WIDE_ROW_TILE = 1024
FFN_CHUNK = 512
NORM_STRIP = 16
PROJ_CHUNK = 1024
MOD_CHUNK = 1024
ATTN_Q_TILE = 1024
ATTN_K_TILE = 512
POOL_TILE = 256
POOL_HALO = 16
VMEM_LIMIT = 56 * 1024 * 1024
FFN_VMEM_LIMIT = 60 * 1024 * 1024


def _params(*sem):
    return pltpu.CompilerParams(dimension_semantics=sem, vmem_limit_bytes=VMEM_LIMIT)


def _silu(x):
    return x * jax.nn.sigmoid(x)


def _rms(x, g, eps=NORM_EPS):
    return x * lax.rsqrt(jnp.mean(x * x, axis=-1, keepdims=True) + eps) * g


def _modulated_norm_strips(h_ref, u_ref, g, shift, scale, zero_ref=None):
    gain = g * (1.0 + scale)

    def strip(i, carry):
        rows = pl.ds(pl.multiple_of(i * NORM_STRIP, NORM_STRIP), NORM_STRIP)
        x = h_ref[rows, :]
        inv = lax.rsqrt(jnp.mean(x * x, axis=-1, keepdims=True) + NORM_EPS)
        u_ref[rows, :] = (x * inv * gain + shift).astype(BF16)
        if zero_ref is not None:
            zero_ref[rows, :] = jnp.zeros((NORM_STRIP, zero_ref.shape[1]), F32)
        return carry

    lax.fori_loop(0, h_ref.shape[0] // NORM_STRIP, strip, 0, unroll=8)


def _dot(a, b):
    return jnp.dot(a, b, preferred_element_type=F32)


def _dot_nt(a, b):
    return lax.dot_general(a, b, (((1,), (1,)), ((), ())), preferred_element_type=F32)


def _dot_tn(a, b):
    return lax.dot_general(a, b, (((0,), (0,)), ((), ())), preferred_element_type=F32)


def _mod_kernel(a_ref, w_ref, b_ref, o_ref):
    s = _silu(a_ref[...]).astype(BF16)
    o_ref[...] = _dot(s, w_ref[...].astype(BF16)) + b_ref[...]


def _modulation(cond, mod_w, mod_b):
    depth, d, n = mod_w.shape
    tn = MOD_CHUNK
    return pl.pallas_call(
        _mod_kernel,
        grid=(depth, n // tn),
        in_specs=[
            pl.BlockSpec((8, d), lambda l, j: (0, 0)),
            pl.BlockSpec((None, d, tn), lambda l, j: (l, 0, j)),
            pl.BlockSpec((None, 1, tn), lambda l, j: (l, 0, j)),
        ],
        out_specs=pl.BlockSpec((None, 8, tn), lambda l, j: (l, 0, j)),
        out_shape=jax.ShapeDtypeStruct((depth, 8, n), F32),
        compiler_params=_params("parallel", "parallel"),
        name="adaln_modulation",
    )(cond, mod_w, mod_b.reshape(depth, 1, n))


def _ffn_kernel(h_ref, m_ref, g_ref, wg_ref, wu_ref, wd_ref, o_ref, u_ref, *, mod0, g0):
    k = pl.program_id(1)
    sub = ROW_TILE

    n_strips = h_ref.shape[0] // NORM_STRIP

    @pl.when(k == 0)
    def _():
        _modulated_norm_strips(h_ref, u_ref, g_ref[g0:g0 + 1, :], m_ref[mod0:mod0 + 1, :], m_ref[mod0 + 1:mod0 + 2, :],
                               zero_ref=o_ref)

    for s in range(h_ref.shape[0] // sub):
        rows = slice(s * sub, (s + 1) * sub)
        u = u_ref[rows, :]
        a = _dot(u, wg_ref[...])
        b = _dot(u, wu_ref[...])
        o_ref[rows, :] += _dot((_silu(a) * b).astype(BF16), wd_ref[...])

    @pl.when(k == pl.num_programs(1) - 1)
    def _():
        gain = (0.5 * m_ref[mod0 + 2:mod0 + 3, :]) * g_ref[g0 + 1:g0 + 2, :]

        for i in range(n_strips):
            rows = slice(i * NORM_STRIP, (i + 1) * NORM_STRIP)
            y = o_ref[rows, :]
            inv = lax.rsqrt(jnp.mean(y * y, axis=-1, keepdims=True) + NORM_EPS)
            o_ref[rows, :] = h_ref[rows, :] + y * inv * gain


def _ffn(h, mods, norm_g, wg, wu, wd, layer, half, out_rows, grp):
    d = h.shape[1]
    f = wg.shape[-1]
    tm, th = WIDE_ROW_TILE, FFN_CHUNK
    return pl.pallas_call(
        functools.partial(_ffn_kernel, mod0=6 * half, g0=4 * half),
        grid=(pl.cdiv(out_rows, tm), f // th),
        in_specs=[
            pl.BlockSpec((tm, d), lambda i, k: (i, 0)),
            pl.BlockSpec((None, None, N_MOD, d), lambda i, k: (layer, grp(i), 0, 0)),
            pl.BlockSpec((None, 6, d), lambda i, k: (layer, 0, 0)),
            pl.BlockSpec((None, None, d, th), lambda i, k: (layer, half, 0, k)),
            pl.BlockSpec((None, None, d, th), lambda i, k: (layer, half, 0, k)),
            pl.BlockSpec((None, None, th, d), lambda i, k: (layer, half, k, 0)),
        ],
        out_specs=pl.BlockSpec((tm, d), lambda i, k: (i, 0)),
        out_shape=jax.ShapeDtypeStruct((out_rows, d), F32),
        scratch_shapes=[pltpu.VMEM((tm, d), BF16)],
        compiler_params=pltpu.CompilerParams(dimension_semantics=("parallel", "arbitrary"),
                                             vmem_limit_bytes=FFN_VMEM_LIMIT),
        name=f"swiglu_l{layer}_{half}",
    )(h, mods, norm_g, wg, wu, wd)


def _proj_kernel(*refs, even, n_rope, scale):
    if even:
        h_ref, m_ref, g_ref, w_ref, c_ref, sa_ref, o_ref, u_ref = refs
    else:
        h_ref, m_ref, g_ref, w_ref, c_ref, sa_ref, sb_ref, o_ref, u_ref = refs
    n = pl.program_id(1)

    @pl.when(n == 0)
    def _():
        _modulated_norm_strips(h_ref, u_ref, g_ref[2:3, :], m_ref[3:4, :], m_ref[4:5, :])

    r = _dot(u_ref[...], w_ref[...])

    is_rope = n < n_rope
    first = n < n_rope // 2
    sc = jnp.where(is_rope, jnp.where(first, 1.0, scale) if even else jnp.where(first, scale, 1.0), 1.0)
    cos = jnp.where(is_rope, c_ref[...], 1.0) * sc
    sin_a = jnp.where(is_rope, sa_ref[...], 0.0) * sc
    if not even:
        sin_b = jnp.where(is_rope, sb_ref[...], 0.0) * sc
    for j in range(r.shape[1] // LANES):
        xb = r[:, j * LANES:(j + 1) * LANES]
        if even:
            tb = (j % 2) * LANES
            ob = xb * cos[:, tb:tb + LANES] + pltpu.roll(xb, 64, 1) * sin_a[:, tb:tb + LANES]
        else:
            ob = xb * cos + pltpu.roll(xb, 96, 1) * sin_a + pltpu.roll(xb, 32, 1) * sin_b
        o_ref[:, j * LANES:(j + 1) * LANES] = ob.astype(BF16)


def _proj_in(h, mods, norm_g, w_in, tables, layer, idx, even, grp, rope_blk):
    out_rows, d = h.shape
    n_out = w_in.shape[-1]
    tm, tn = WIDE_ROW_TILE, PROJ_CHUNK
    tw = tables[0].shape[1]
    if even:
        n_rope, scale = (2 * (d // 2)) // tn, float((d // 2 // RET_HEADS) ** -0.5)
    else:
        n_rope, scale = (2 * d) // tn, float((d // DIFF_HEADS // 2) ** -0.5 * math.log2(math.e))
    return pl.pallas_call(
        functools.partial(_proj_kernel, even=even, n_rope=n_rope, scale=scale),
        grid=(pl.cdiv(out_rows, tm), n_out // tn),
        in_specs=[
            pl.BlockSpec((tm, d), lambda i, n: (i, 0)),
            pl.BlockSpec((None, None, N_MOD, d), lambda i, n: (layer, grp(i), 0, 0)),
            pl.BlockSpec((None, 6, d), lambda i, n: (layer, 0, 0)),
            pl.BlockSpec((None, d, tn), lambda i, n: (idx, 0, n)),
        ] + [pl.BlockSpec((tm, tw), lambda i, n: (rope_blk(i), 0)) for _ in tables],
        out_specs=pl.BlockSpec((tm, tn), lambda i, n: (i, n)),
        out_shape=jax.ShapeDtypeStruct((out_rows, n_out), BF16),
        scratch_shapes=[pltpu.VMEM((tm, d), BF16)],
        compiler_params=_params("parallel", "arbitrary"),
        name=f"mixer_in_l{layer}",
    )(h, mods, norm_g, w_in, *tables)


def _rope_tables(seq, n_ident, even):
    n_rows = seq // GRID_W
    nf = 64 if even else 32
    inv = ROPE_BASE ** (-jnp.arange(nf, dtype=F32) / nf)
    ar = jnp.arange(n_rows, dtype=F32)[:, None] * inv[None, :]
    ac = jnp.arange(GRID_W, dtype=F32)[:, None] * inv[None, :]
    cr, sr = jnp.repeat(jnp.cos(ar), GRID_W, axis=0), jnp.repeat(jnp.sin(ar), GRID_W, axis=0)
    cc, sc = jnp.tile(jnp.cos(ac), (n_rows, 1)), jnp.tile(jnp.sin(ac), (n_rows, 1))
    z = jnp.zeros_like(sr)
    if even:
        tabs = [jnp.concatenate([cr, cr, cc, cc], 1), jnp.concatenate([-sr, sr, -sc, sc], 1)]
    else:
        tabs = [jnp.concatenate([cr, cr, cc, cc], 1), jnp.concatenate([-sr, z, -sc, z], 1),
                jnp.concatenate([z, sr, z, sc], 1)]
    ident = [jnp.ones((n_ident, tabs[0].shape[1]), F32)] + [jnp.zeros((n_ident, tabs[0].shape[1]), F32)] * (len(tabs) - 1)
    return [jnp.concatenate([a, b], 0) for a, b in zip(tabs, ident)]


def _attn_kernel(*refs, tk, n_kt, lam_init):
    if n_kt:
        (lam_ref, sg_ref, q_ref, kc_ref, vc_ref, k_ref, vt_ref, o_ref,
         m_ref, l_ref, acc_ref, s_ref, mx_ref, p_ref, al_ref) = refs
    else:
        lam_ref, sg_ref, q_ref, kc_ref, vc_ref, _, o_ref, m_ref, l_ref, acc_ref = refs
    if n_kt:
        dh, tq = q_ref.shape[0] // 2, q_ref.shape[1]

        def scores_t(kb, mi):
            return _dot(kb[:, mi * dh:(mi + 1) * dh], q_ref[mi * dh:(mi + 1) * dh, :])
    else:
        dh, tq = q_ref.shape[1] // 2, q_ref.shape[0]

        def scores_t(kb, mi):
            return _dot_nt(kb[:, mi * dh:(mi + 1) * dh], q_ref[:, mi * dh:(mi + 1) * dh])

    if n_kt:
        def qk(j, slot):
            kb = k_ref[pl.ds(pl.multiple_of(j * tk, tk), tk), :]
            for mi in range(2):
                s_t = scores_t(kb, mi)
                s_ref[slot, mi] = s_t
                mx_ref[slot, mi] = jnp.max(s_t, axis=0, keepdims=True)

        def softmax(slot):
            for mi in range(2):
                m_prev = m_ref[mi]
                m_next = jnp.maximum(m_prev, mx_ref[slot, mi])
                alpha = jnp.exp2(m_prev - m_next)
                p_t = jnp.exp2(s_ref[slot, mi] - m_next)
                l_ref[mi] = alpha * l_ref[mi] + jnp.sum(p_t, axis=0, keepdims=True)
                p_ref[slot, mi] = p_t.astype(BF16)
                al_ref[slot, mi] = alpha
                m_ref[mi] = m_next

        def pv(j, slot):
            vt = vt_ref[j]
            for mi in range(2):
                acc_ref[mi] = al_ref[slot, mi] * acc_ref[mi] + _dot(vt, p_ref[slot, mi])

        qk(0, 0)

    kc, vc = kc_ref[...], vc_ref[...]
    for mi in range(2):
        s_t = scores_t(kc, mi)
        m0 = jnp.max(s_t, axis=0, keepdims=True)
        p_t = jnp.exp2(s_t - m0)
        m_ref[mi] = m0
        l_ref[mi] = jnp.sum(p_t, axis=0, keepdims=True)
        acc_ref[mi] = _dot_tn(vc, p_t.astype(BF16))

    if n_kt:
        qk(1, 1)
        softmax(0)

        def body(i, carry):
            j = 2 * i
            qk(j + 2, 0)
            softmax(1)
            pv(j, 0)
            qk(j + 3, 1)
            softmax(0)
            pv(j + 1, 1)
            return carry

        lax.fori_loop(0, n_kt // 2 - 1, body, 0)
        softmax(1)
        pv(n_kt - 2, 0)
        pv(n_kt - 1, 1)

    t1 = jnp.sum(lam_ref[0:1, :] * lam_ref[1:2, :], axis=-1, keepdims=True)
    t2 = jnp.sum(lam_ref[2:3, :] * lam_ref[3:4, :], axis=-1, keepdims=True)
    lam = jnp.exp(t1) - jnp.exp(t2) + lam_init
    o_t = acc_ref[0] * (1.0 / l_ref[0]) - acc_ref[1] * (lam / l_ref[1])
    gain = jnp.concatenate([sg_ref[...]] * (tq // LANES), axis=1) * (1.0 - lam_init)
    o_t = o_t * lax.rsqrt(jnp.mean(o_t * o_t, axis=0, keepdims=True) + SUBLN_EPS) * gain
    o_ref[...] = o_t.T.astype(BF16)


def _transpose_kernel(x_ref, o_ref):
    o_ref[...] = x_ref[...].astype(F32).T.astype(BF16)


def _diff_attention(qkv, lam_vecs, subln_g, lam_init, n_batch, seq, ctx_len, need_ctx):
    rows, width = qkv.shape
    d = width // 3
    hd = d // DIFF_HEADS
    tq, tk = ATTN_Q_TILE, ATTN_K_TILE
    nq = seq // tq
    assert seq % (2 * tk) == 0
    ctx_blk0 = (n_batch * seq) // ctx_len
    nkt = seq // tk
    scratch = lambda t: [pltpu.VMEM((2, 1, t), F32), pltpu.VMEM((2, 1, t), F32), pltpu.VMEM((2, hd, t), F32)]
    small = [pl.BlockSpec((4, hd // 2), lambda b, h, qi: (0, 0)),
             pl.BlockSpec((hd, LANES), lambda b, h, qi: (0, 0))]
    ctx_kv = [pl.BlockSpec((ctx_len, hd), lambda b, h, qi: (ctx_blk0 + b, DIFF_HEADS + h)),
              pl.BlockSpec((ctx_len, hd), lambda b, h, qi: (ctx_blk0 + b, 2 * DIFF_HEADS + h))]
    def transposed(tile, col_blk, name):
        n = seq // tile
        return pl.pallas_call(
            _transpose_kernel,
            grid=(n_batch, n),
            in_specs=[pl.BlockSpec((tile, d), lambda b, j: (b * n + j, col_blk))],
            out_specs=pl.BlockSpec((None, None, d, tile), lambda b, j: (b, j, 0, 0)),
            out_shape=jax.ShapeDtypeStruct((n_batch, n, d, tile), BF16),
            compiler_params=_params("parallel", "parallel"),
            name=name,
        )(qkv)

    v_t = transposed(tk, 2, "value_transpose")
    q_t = transposed(tq, 0, "query_transpose")
    y = pl.pallas_call(
        functools.partial(_attn_kernel, tk=tk, n_kt=nkt, lam_init=lam_init),
        grid=(n_batch, DIFF_HEADS, nq),
        in_specs=small + [pl.BlockSpec((None, None, hd, tq), lambda b, h, qi: (b, qi, h, 0))] + ctx_kv + [
            pl.BlockSpec((seq, hd), lambda b, h, qi: (b, DIFF_HEADS + h)),
            pl.BlockSpec((None, nkt, hd, tk), lambda b, h, qi: (b, 0, h, 0)),
        ],
        out_specs=pl.BlockSpec((tq, hd), lambda b, h, qi: (b * nq + qi, h)),
        out_shape=jax.ShapeDtypeStruct((rows, d), BF16),
        scratch_shapes=scratch(tq) + [pltpu.VMEM((2, 2, tk, tq), F32), pltpu.VMEM((2, 2, 1, tq), F32),
                                      pltpu.VMEM((2, 2, tk, tq), BF16), pltpu.VMEM((2, 2, 1, tq), F32)],
        compiler_params=_params("parallel", "parallel", "arbitrary"),
        name="diff_attention_latent",
    )(lam_vecs, subln_g, q_t, qkv, qkv, qkv, v_t)
    if not need_ctx:
        return y
    return pl.pallas_call(
        functools.partial(_attn_kernel, tk=tk, n_kt=0, lam_init=lam_init),
        grid=(n_batch, DIFF_HEADS, 1),
        in_specs=small + [pl.BlockSpec((ctx_len, hd), lambda b, h, qi: (ctx_blk0 + b, h))] + ctx_kv + [
            pl.BlockSpec(memory_space=pl.ANY)],
        out_specs=pl.BlockSpec((ctx_len, hd), lambda b, h, qi: (ctx_blk0 + b, h)),
        out_shape=jax.ShapeDtypeStruct((rows, d), BF16),
        scratch_shapes=scratch(ctx_len),
        input_output_aliases={5: 0},
        compiler_params=_params("parallel", "parallel", "arbitrary"),
        name="diff_attention_context",
    )(lam_vecs, subln_g, qkv, qkv, qkv, y)


def _ret_fwd_kernel(lg_ref, q_ref, k_ref, v_ref, o_ref, d_ref, s_ref):
    n = pl.program_id(1)
    c = q_ref.shape[0]
    hd = q_ref.shape[1] // RET_HEADS

    @pl.when(n == 0)
    def _():
        s_ref[...] = jnp.zeros(s_ref.shape, F32)
        i = lax.broadcasted_iota(jnp.int32, (c, c), 0)
        j = lax.broadcasted_iota(jnp.int32, (c, c), 1)
        dist = (i - j).astype(F32)
        for h in range(RET_HEADS):
            fwd = jnp.exp(lg_ref[0, h] * jnp.maximum(dist, 0.0))
            bwd = jnp.exp(lg_ref[1, h] * jnp.maximum(-dist, 0.0))
            d_ref[h] = jnp.where(dist >= 0, fwd, bwd)

    pos = lax.broadcasted_iota(jnp.int32, (c, 1), 0).astype(F32)
    for h in range(RET_HEADS):
        lg = lg_ref[0, h]
        sl = slice(h * hd, (h + 1) * hd)
        qh, kh, vh = q_ref[:, sl], k_ref[:, sl], v_ref[:, sl]
        p = (_dot_nt(qh, kh) * d_ref[h]).astype(BF16)
        qd = (qh.astype(F32) * jnp.exp(lg * (pos + 1.0))).astype(BF16)
        s = s_ref[h]
        o_ref[:, sl] = _dot(p, vh) + _dot(qd, s.astype(BF16))
        kd = (kh.astype(F32) * jnp.exp(lg * (c - 1.0 - pos))).astype(BF16)
        s_ref[h] = jnp.exp(jnp.full((1, 1), lg * c, F32)) * s + _dot_tn(kd, vh)


def _ret_bwd_kernel(lg_ref, q_ref, k_ref, v_ref, g_ref, r_ref, o_ref, s_ref):
    n = pl.program_id(1)
    c = q_ref.shape[0]
    hd = q_ref.shape[1] // RET_HEADS

    @pl.when(n == 0)
    def _():
        s_ref[...] = jnp.zeros(s_ref.shape, F32)

    pos = lax.broadcasted_iota(jnp.int32, (c, 1), 0).astype(F32)
    for h in range(RET_HEADS):
        lg = lg_ref[1, h]
        sl = slice(h * hd, (h + 1) * hd)
        qh, kh, vh = q_ref[:, sl], k_ref[:, sl], v_ref[:, sl]
        qd = (qh.astype(F32) * jnp.exp(lg * (c - pos))).astype(BF16)
        s = s_ref[h]
        r = r_ref[:, sl] + _dot(qd, s.astype(BF16))
        y = r * lax.rsqrt(jnp.mean(r * r, axis=-1, keepdims=True) + NORM_EPS)
        o_ref[:, sl] = (y * _silu(g_ref[:, sl].astype(F32))).astype(BF16)
        kd = (kh.astype(F32) * jnp.exp(lg * pos)).astype(BF16)
        s_ref[h] = jnp.exp(jnp.full((1, 1), lg * c, F32)) * s + _dot_tn(kd, vh)


def _retention(qkv, log_decay, n_batch, seq, ctx_len):
    rows = qkv.shape[0]
    c = ctx_len
    w = (qkv.shape[1] // 5)
    nc = seq // c
    ctx_blk0 = (n_batch * seq) // c
    fwd_blk = lambda b, n: jnp.where(n == 0, ctx_blk0 + b, b * nc + n - 1)
    bwd_blk = lambda b, n: jnp.where(n == 0, ctx_blk0 + b, b * nc + nc - n)
    smem = pl.BlockSpec(memory_space=pltpu.SMEM)
    state = pltpu.VMEM((RET_HEADS, w // RET_HEADS, w // RET_HEADS), F32)
    r1 = pl.pallas_call(
        _ret_fwd_kernel,
        grid=(n_batch, nc + 1),
        in_specs=[smem] + [pl.BlockSpec((c, w), functools.partial(lambda b, n, j: (fwd_blk(b, n), j), j=j)) for j in range(3)],
        out_specs=pl.BlockSpec((c, w), lambda b, n: (fwd_blk(b, n), 0)),
        out_shape=jax.ShapeDtypeStruct((rows, w), F32),
        scratch_shapes=[pltpu.VMEM((RET_HEADS, c, c), F32), state],
        compiler_params=_params("parallel", "arbitrary"),
        name="retention_forward_sweep",
    )(log_decay, qkv, qkv, qkv)
    return pl.pallas_call(
        _ret_bwd_kernel,
        grid=(n_batch, nc + 1),
        in_specs=[smem] + [pl.BlockSpec((c, w), functools.partial(lambda b, n, j: (bwd_blk(b, n), j), j=j)) for j in range(4)]
        + [pl.BlockSpec((c, w), lambda b, n: (bwd_blk(b, n), 0))],
        out_specs=pl.BlockSpec((c, w), lambda b, n: (bwd_blk(b, n), 0)),
        out_shape=jax.ShapeDtypeStruct((rows, w), BF16),
        scratch_shapes=[state],
        compiler_params=_params("parallel", "arbitrary"),
        name="retention_backward_sweep",
    )(log_decay, qkv, qkv, qkv, qkv, r1)


def _pool_kernel(prev_ref, main_ref, next_ref, w_ref, sc_ref, o_ref, *, lat_tiles, tiles_per_seq, seq, ctx_len):
    i = pl.program_id(0)
    tp = main_ref.shape[0]
    gw = main_ref.shape[1] // len(POOL_WINDOWS)
    is_lat = i < lat_tiles
    t0 = jnp.where(is_lat, (i % tiles_per_seq) * tp, 0)
    length = jnp.where(is_lat, seq, ctx_len)
    n_ext = tp + 2 * POOL_HALO
    t = t0 - POOL_HALO + lax.broadcasted_iota(jnp.int32, (n_ext, 1), 0)
    valid = (t >= 0) & (t < length)
    tm = t[POOL_HALO:POOL_HALO + tp]
    for gi, win in enumerate(POOL_WINDOWS):
        sl = slice(gi * gw, (gi + 1) * gw)
        ext = jnp.concatenate([prev_ref[:, sl], main_ref[:, sl], next_ref[:, sl]], axis=0).astype(F32)
        x = jnp.where(valid, ext, 0.0)
        acc = x + pltpu.roll(x, 1, 0)
        half = 1
        while 2 * half < win:
            acc = pltpu.roll(acc, n_ext - half, 0) + pltpu.roll(acc, half, 0)
            half *= 2
        cnt = (jnp.minimum(tm + win // 2, length) - jnp.maximum(tm - win // 2, 0)).astype(F32)
        pooled = acc[POOL_HALO:POOL_HALO + tp] / cnt - x[POOL_HALO:POOL_HALO + tp]
        o_ref[:, sl] = (_dot(pooled.astype(BF16), w_ref[gi]) * sc_ref[:, sl]).astype(BF16)


def _pool(qkv, pool_w, pool_scale, idx, n_batch, seq, ctx_len):
    rows = qkv.shape[0]
    w = qkv.shape[1] // 5
    tp, halo = POOL_TILE, POOL_HALO
    assert ctx_len == tp and seq % tp == 0
    last_halo_blk = rows // halo - 1
    per = tp // halo
    return pl.pallas_call(
        functools.partial(_pool_kernel, lat_tiles=(n_batch * seq) // tp, tiles_per_seq=seq // tp, seq=seq, ctx_len=ctx_len),
        grid=(rows // tp,),
        in_specs=[
            pl.BlockSpec((halo, w), lambda i: (jnp.maximum(i * per - 1, 0), 4)),
            pl.BlockSpec((tp, w), lambda i: (i, 4)),
            pl.BlockSpec((halo, w), lambda i: (jnp.minimum((i + 1) * per, last_halo_blk), 4)),
            pl.BlockSpec((None,) + pool_w.shape[1:], lambda i: (idx, 0, 0, 0)),
            pl.BlockSpec((None, 1, w), lambda i: (idx, 0, 0)),
        ],
        out_specs=pl.BlockSpec((tp, w), lambda i: (i, 0)),
        out_shape=jax.ShapeDtypeStruct((rows, w), BF16),
        compiler_params=_params("parallel"),
        name="multiscale_pool",
    )(qkv, qkv, qkv, pool_w, pool_scale.reshape(pool_scale.shape[0], 1, w))


def _out_kernel(h_ref, m_ref, g_ref, ya_ref, yb_ref, wa_ref, wb_ref, o_ref):
    y = _dot(ya_ref[...], wa_ref[...]) + _dot(yb_ref[...], wb_ref[...])
    o_ref[...] = h_ref[...] + m_ref[5:6, :] * _rms(y, g_ref[3:4, :])


def _out_proj(h, mods, norm_g, ya, ya_blk, yb, yb_blk, w_out, layer, idx, n_tiles, grp):
    d = h.shape[1]
    tm = ROW_TILE
    return pl.pallas_call(
        _out_kernel,
        grid=(n_tiles,),
        in_specs=[
            pl.BlockSpec((tm, d), lambda i: (i, 0)),
            pl.BlockSpec((None, None, N_MOD, d), lambda i: (layer, grp(i), 0, 0)),
            pl.BlockSpec((None, 6, d), lambda i: (layer, 0, 0)),
            pl.BlockSpec((tm, d // 2), lambda i: (i, ya_blk)),
            pl.BlockSpec((tm, d // 2), lambda i: (i, yb_blk)),
            pl.BlockSpec((None, d // 2, d), lambda i: (idx, 0, 0)),
            pl.BlockSpec((None, d // 2, d), lambda i: (idx, 1, 0)),
        ],
        out_specs=pl.BlockSpec((tm, d), lambda i: (i, 0)),
        out_shape=jax.ShapeDtypeStruct((n_tiles * tm, d), F32),
        compiler_params=_params("parallel"),
        name=f"mixer_out_l{layer}",
    )(h, mods, norm_g, ya, yb, w_out, w_out)


def kernel(x, c, ctx, c_ctx, mod_w, mod_b, norm_g, ffn_w_gate, ffn_w_up, ffn_w_down, ev_w_in, ev_w_out, ret_decay, pool_w, pool_scale, od_w_in, od_w_out, lam_q1, lam_k1, lam_q2, lam_k2, diff_subln_g):
    n_batch, seq, d = x.shape
    ctx_len = ctx.shape[1]
    depth = mod_w.shape[0]
    tm = ROW_TILE
    lat_rows, ctx_rows = n_batch * seq, n_batch * ctx_len
    assert seq % tm == 0 and ctx_rows % tm == 0 and n_batch + 1 <= 8
    assert d // 2 // RET_HEADS == 2 * LANES and d // DIFF_HEADS == 2 * LANES and ctx_len == POOL_TILE
    lat_tiles, all_tiles = lat_rows // tm, (lat_rows + ctx_rows) // tm
    tiles_per_batch = seq // tm
    grp = lambda i: jnp.minimum(i // tiles_per_batch, n_batch)
    assert seq % WIDE_ROW_TILE == 0
    wide_per_batch = seq // WIDE_ROW_TILE
    wide_grp = lambda i: jnp.minimum(i // wide_per_batch, n_batch)
    rope_blk = lambda i: jnp.where(i < n_batch * wide_per_batch, i % wide_per_batch, wide_per_batch)

    cond = jnp.concatenate([c, c_ctx[None, :], jnp.zeros((8 - n_batch - 1, d), F32)], axis=0)
    mods = _modulation(cond, mod_w, mod_b)[:, :n_batch + 1].reshape(depth, n_batch + 1, N_MOD, d)

    wg, wu, wd = ffn_w_gate.astype(BF16), ffn_w_up.astype(BF16), ffn_w_down.astype(BF16)
    ev_in, ev_out, od_in, od_out = ev_w_in.astype(BF16), ev_w_out.astype(BF16), od_w_in.astype(BF16), od_w_out.astype(BF16)
    pw = pool_w.astype(BF16)
    log_decay = jax.nn.log_sigmoid(ret_decay.astype(F32))
    tabs_even = _rope_tables(seq, WIDE_ROW_TILE, True)
    tabs_odd = _rope_tables(seq, WIDE_ROW_TILE, False)

    h = jnp.concatenate([x.reshape(lat_rows, d), ctx.reshape(ctx_rows, d)], axis=0)
    for l in range(depth):
        need_ctx = l < depth - 1
        out_tiles = all_tiles if need_ctx else lat_tiles
        h = _ffn(h, mods, norm_g, wg, wu, wd, l, 0, lat_rows + ctx_rows, wide_grp)
        if l % 2 == 0:
            e = l // 2
            qkv = _proj_in(h, mods, norm_g, ev_in, tabs_even, l, e, True, wide_grp, rope_blk)
            y = _retention(qkv, log_decay[e], n_batch, seq, ctx_len)
            pm = _pool(qkv, pw, pool_scale, e, n_batch, seq, ctx_len)
            h = _out_proj(h, mods, norm_g, y, 0, pm, 0, ev_out, l, e, out_tiles, grp)
        else:
            o = l // 2
            lam_init = 0.8 - 0.6 * math.exp(-0.3 * l)
            qkv = _proj_in(h, mods, norm_g, od_in, tabs_odd, l, o, False, wide_grp, rope_blk)
            lam_vecs = jnp.stack([lam_q1[o], lam_k1[o], lam_q2[o], lam_k2[o]], axis=0)
            gain = jnp.broadcast_to(diff_subln_g[o][:, None], (diff_subln_g.shape[1], LANES))
            y = _diff_attention(qkv, lam_vecs, gain, lam_init, n_batch, seq, ctx_len, need_ctx)
            h = _out_proj(h, mods, norm_g, y, 0, y, 1, od_out, l, o, out_tiles, grp)
        h = _ffn(h, mods, norm_g, wg, wu, wd, l, 1, out_tiles * tm, wide_grp)
    return h[:lat_rows].reshape(n_batch, seq, d)
```

```python
import functools
import math

import jax
import jax.numpy as jnp
from jax import lax
from jax.experimental import pallas as pl
from jax.experimental.pallas import tpu as pltpu

F32 = jnp.float32
BF16 = jnp.bfloat16

GRID_W = 64
ROPE_BASE = 10000.0
N_MOD = 9
RET_HEADS = 4
POOL_WINDOWS = (2, 4, 8, 16)
DIFF_HEADS = 8
NORM_EPS = 1e-6
SUBLN_EPS = 1e-5

LANES = 128
ROW_TILE = 512
WIDE_ROW_TILE = 1024
FFN_CHUNK = 512
NORM_STRIP = 16
PROJ_CHUNK = 1024
MOD_CHUNK = 1024
ATTN_Q_TILE = 1024
ATTN_K_TILE = 512
POOL_TILE = 256
POOL_HALO = 16
VMEM_LIMIT = 56 * 1024 * 1024
FFN_VMEM_LIMIT = 60 * 1024 * 1024


def _params(*sem):
    return pltpu.CompilerParams(dimension_semantics=sem, vmem_limit_bytes=VMEM_LIMIT)


def _silu(x):
    return x * jax.nn.sigmoid(x)


def _rms(x, g, eps=NORM_EPS):
    return x * lax.rsqrt(jnp.mean(x * x, axis=-1, keepdims=True) + eps) * g


def _modulated_norm_strips(h_ref, u_ref, g, shift, scale, zero_ref=None):
    gain = g * (1.0 + scale)

    def strip(i, carry):
        rows = pl.ds(pl.multiple_of(i * NORM_STRIP, NORM_STRIP), NORM_STRIP)
        x = h_ref[rows, :]
        inv = lax.rsqrt(jnp.mean(x * x, axis=-1, keepdims=True) + NORM_EPS)
        u_ref[rows, :] = (x * inv * gain + shift).astype(BF16)
        if zero_ref is not None:
            zero_ref[rows, :] = jnp.zeros((NORM_STRIP, zero_ref.shape[1]), F32)
        return carry

    lax.fori_loop(0, h_ref.shape[0] // NORM_STRIP, strip, 0, unroll=8)


def _dot(a, b):
    return jnp.dot(a, b, preferred_element_type=F32)


def _dot_nt(a, b):
    return lax.dot_general(a, b, (((1,), (1,)), ((), ())), preferred_element_type=F32)


def _dot_tn(a, b):
    return lax.dot_general(a, b, (((0,), (0,)), ((), ())), preferred_element_type=F32)


def _mod_kernel(a_ref, w_ref, b_ref, o_ref):
    s = _silu(a_ref[...]).astype(BF16)
    o_ref[...] = _dot(s, w_ref[...].astype(BF16)) + b_ref[...]


def _modulation(cond, mod_w, mod_b):
    depth, d, n = mod_w.shape
    tn = MOD_CHUNK
    return pl.pallas_call(
        _mod_kernel,
        grid=(depth, n // tn),
        in_specs=[
            pl.BlockSpec((8, d), lambda l, j: (0, 0)),
            pl.BlockSpec((None, d, tn), lambda l, j: (l, 0, j)),
            pl.BlockSpec((None, 1, tn), lambda l, j: (l, 0, j)),
        ],
        out_specs=pl.BlockSpec((None, 8, tn), lambda l, j: (l, 0, j)),
        out_shape=jax.ShapeDtypeStruct((depth, 8, n), F32),
        compiler_params=_params("parallel", "parallel"),
        name="adaln_modulation",
    )(cond, mod_w, mod_b.reshape(depth, 1, n))


def _ffn_kernel(h_ref, m_ref, g_ref, wg_ref, wu_ref, wd_ref, o_ref, u_ref, *, mod0, g0):
    k = pl.program_id(1)
    sub = ROW_TILE

    n_strips = h_ref.shape[0] // NORM_STRIP

    @pl.when(k == 0)
    def _():
        _modulated_norm_strips(h_ref, u_ref, g_ref[g0:g0 + 1, :], m_ref[mod0:mod0 + 1, :], m_ref[mod0 + 1:mod0 + 2, :],
                               zero_ref=o_ref)

    for s in range(h_ref.shape[0] // sub):
        rows = slice(s * sub, (s + 1) * sub)
        u = u_ref[rows, :]
        a = _dot(u, wg_ref[...])
        b = _dot(u, wu_ref[...])
        o_ref[rows, :] += _dot((_silu(a) * b).astype(BF16), wd_ref[...])

    @pl.when(k == pl.num_programs(1) - 1)
    def _():
        gain = (0.5 * m_ref[mod0 + 2:mod0 + 3, :]) * g_ref[g0 + 1:g0 + 2, :]

        for i in range(n_strips):
            rows = slice(i * NORM_STRIP, (i + 1) * NORM_STRIP)
            y = o_ref[rows, :]
            inv = lax.rsqrt(jnp.mean(y * y, axis=-1, keepdims=True) + NORM_EPS)
            o_ref[rows, :] = h_ref[rows, :] + y * inv * gain


def _ffn(h, mods, norm_g, wg, wu, wd, layer, half, out_rows, grp):
    d = h.shape[1]
    f = wg.shape[-1]
    tm, th = WIDE_ROW_TILE, FFN_CHUNK
    return pl.pallas_call(
        functools.partial(_ffn_kernel, mod0=6 * half, g0=4 * half),
        grid=(pl.cdiv(out_rows, tm), f // th),
        in_specs=[
            pl.BlockSpec((tm, d), lambda i, k: (i, 0)),
            pl.BlockSpec((None, None, N_MOD, d), lambda i, k: (layer, grp(i), 0, 0)),
            pl.BlockSpec((None, 6, d), lambda i, k: (layer, 0, 0)),
            pl.BlockSpec((None, None, d, th), lambda i, k: (layer, half, 0, k)),
            pl.BlockSpec((None, None, d, th), lambda i, k: (layer, half, 0, k)),
            pl.BlockSpec((None, None, th, d), lambda i, k: (layer, half, k, 0)),
        ],
        out_specs=pl.BlockSpec((tm, d), lambda i, k: (i, 0)),
        out_shape=jax.ShapeDtypeStruct((out_rows, d), F32),
        scratch_shapes=[pltpu.VMEM((tm, d), BF16)],
        compiler_params=pltpu.CompilerParams(dimension_semantics=("parallel", "arbitrary"),
                                             vmem_limit_bytes=FFN_VMEM_LIMIT),
        name=f"swiglu_l{layer}_{half}",
    )(h, mods, norm_g, wg, wu, wd)


def _proj_kernel(*refs, even, n_rope, scale):
    if even:
        h_ref, m_ref, g_ref, w_ref, c_ref, sa_ref, o_ref, u_ref = refs
    else:
        h_ref, m_ref, g_ref, w_ref, c_ref, sa_ref, sb_ref, o_ref, u_ref = refs
    n = pl.program_id(1)

    @pl.when(n == 0)
    def _():
        _modulated_norm_strips(h_ref, u_ref, g_ref[2:3, :], m_ref[3:4, :], m_ref[4:5, :])

    r = _dot(u_ref[...], w_ref[...])

    is_rope = n < n_rope
    first = n < n_rope // 2
    sc = jnp.where(is_rope, jnp.where(first, 1.0, scale) if even else jnp.where(first, scale, 1.0), 1.0)
    cos = jnp.where(is_rope, c_ref[...], 1.0) * sc
    sin_a = jnp.where(is_rope, sa_ref[...], 0.0) * sc
    if not even:
        sin_b = jnp.where(is_rope, sb_ref[...], 0.0) * sc
    for j in range(r.shape[1] // LANES):
        xb = r[:, j * LANES:(j + 1) * LANES]
        if even:
            tb = (j % 2) * LANES
            ob = xb * cos[:, tb:tb + LANES] + pltpu.roll(xb, 64, 1) * sin_a[:, tb:tb + LANES]
        else:
            ob = xb * cos + pltpu.roll(xb, 96, 1) * sin_a + pltpu.roll(xb, 32, 1) * sin_b
        o_ref[:, j * LANES:(j + 1) * LANES] = ob.astype(BF16)


def _proj_in(h, mods, norm_g, w_in, tables, layer, idx, even, grp, rope_blk):
    out_rows, d = h.shape
    n_out = w_in.shape[-1]
    tm, tn = WIDE_ROW_TILE, PROJ_CHUNK
    tw = tables[0].shape[1]
    if even:
        n_rope, scale = (2 * (d // 2)) // tn, float((d // 2 // RET_HEADS) ** -0.5)
    else:
        n_rope, scale = (2 * d) // tn, float((d // DIFF_HEADS // 2) ** -0.5 * math.log2(math.e))
    return pl.pallas_call(
        functools.partial(_proj_kernel, even=even, n_rope=n_rope, scale=scale),
        grid=(pl.cdiv(out_rows, tm), n_out // tn),
        in_specs=[
            pl.BlockSpec((tm, d), lambda i, n: (i, 0)),
            pl.BlockSpec((None, None, N_MOD, d), lambda i, n: (layer, grp(i), 0, 0)),
            pl.BlockSpec((None, 6, d), lambda i, n: (layer, 0, 0)),
            pl.BlockSpec((None, d, tn), lambda i, n: (idx, 0, n)),
        ] + [pl.BlockSpec((tm, tw), lambda i, n: (rope_blk(i), 0)) for _ in tables],
        out_specs=pl.BlockSpec((tm, tn), lambda i, n: (i, n)),
        out_shape=jax.ShapeDtypeStruct((out_rows, n_out), BF16),
        scratch_shapes=[pltpu.VMEM((tm, d), BF16)],
        compiler_params=_params("parallel", "arbitrary"),
        name=f"mixer_in_l{layer}",
    )(h, mods, norm_g, w_in, *tables)


def _rope_tables(seq, n_ident, even):
    n_rows = seq // GRID_W
    nf = 64 if even else 32
    inv = ROPE_BASE ** (-jnp.arange(nf, dtype=F32) / nf)
    ar = jnp.arange(n_rows, dtype=F32)[:, None] * inv[None, :]
    ac = jnp.arange(GRID_W, dtype=F32)[:, None] * inv[None, :]
    cr, sr = jnp.repeat(jnp.cos(ar), GRID_W, axis=0), jnp.repeat(jnp.sin(ar), GRID_W, axis=0)
    cc, sc = jnp.tile(jnp.cos(ac), (n_rows, 1)), jnp.tile(jnp.sin(ac), (n_rows, 1))
    z = jnp.zeros_like(sr)
    if even:
        tabs = [jnp.concatenate([cr, cr, cc, cc], 1), jnp.concatenate([-sr, sr, -sc, sc], 1)]
    else:
        tabs = [jnp.concatenate([cr, cr, cc, cc], 1), jnp.concatenate([-sr, z, -sc, z], 1),
                jnp.concatenate([z, sr, z, sc], 1)]
    ident = [jnp.ones((n_ident, tabs[0].shape[1]), F32)] + [jnp.zeros((n_ident, tabs[0].shape[1]), F32)] * (len(tabs) - 1)
    return [jnp.concatenate([a, b], 0) for a, b in zip(tabs, ident)]


def _attn_kernel(*refs, tk, n_kt, lam_init):
    if n_kt:
        lam_ref, sg_ref, q_ref, kc_ref, vc_ref, k_ref, vt_ref, o_ref, m_ref, l_ref, acc_ref, s_ref, mx_ref = refs
    else:
        lam_ref, sg_ref, q_ref, kc_ref, vc_ref, _, o_ref, m_ref, l_ref, acc_ref = refs
    if n_kt:
        dh, tq = q_ref.shape[0] // 2, q_ref.shape[1]

        def scores_t(kb, mi):
            return _dot(kb[:, mi * dh:(mi + 1) * dh], q_ref[mi * dh:(mi + 1) * dh, :])
    else:
        dh, tq = q_ref.shape[1] // 2, q_ref.shape[0]

        def scores_t(kb, mi):
            return _dot_nt(kb[:, mi * dh:(mi + 1) * dh], q_ref[:, mi * dh:(mi + 1) * dh])

    def absorb(slot, vt, mi):
        m_prev = m_ref[mi]
        m_next = jnp.maximum(m_prev, mx_ref[slot, mi])
        alpha = jnp.exp2(m_prev - m_next)
        p_t = jnp.exp2(s_ref[slot, mi] - m_next)
        l_ref[mi] = alpha * l_ref[mi] + jnp.sum(p_t, axis=0, keepdims=True)
        acc_ref[mi] = alpha * acc_ref[mi] + _dot(vt, p_t.astype(BF16))
        m_ref[mi] = m_next

    if n_kt:
        def qk(j, slot):
            kb = k_ref[pl.ds(pl.multiple_of(j * tk, tk), tk), :]
            for mi in range(2):
                s_t = scores_t(kb, mi)
                s_ref[slot, mi] = s_t
                mx_ref[slot, mi] = jnp.max(s_t, axis=0, keepdims=True)

        def pv(j, slot):
            vt = vt_ref[j]
            for mi in range(2):
                absorb(slot, vt, mi)

        qk(0, 0)

    kc, vc = kc_ref[...], vc_ref[...]
    for mi in range(2):
        s_t = scores_t(kc, mi)
        m0 = jnp.max(s_t, axis=0, keepdims=True)
        p_t = jnp.exp2(s_t - m0)
        m_ref[mi] = m0
        l_ref[mi] = jnp.sum(p_t, axis=0, keepdims=True)
        acc_ref[mi] = _dot_tn(vc, p_t.astype(BF16))

    if n_kt:
        def body(i, carry):
            j = 2 * i
            qk(j + 1, 1)
            pv(j, 0)
            qk(j + 2, 0)
            pv(j + 1, 1)
            return carry

        lax.fori_loop(0, n_kt // 2 - 1, body, 0)
        qk(n_kt - 1, 1)
        pv(n_kt - 2, 0)
        pv(n_kt - 1, 1)

    t1 = jnp.sum(lam_ref[0:1, :] * lam_ref[1:2, :], axis=-1, keepdims=True)
    t2 = jnp.sum(lam_ref[2:3, :] * lam_ref[3:4, :], axis=-1, keepdims=True)
    lam = jnp.exp(t1) - jnp.exp(t2) + lam_init
    o_t = acc_ref[0] * (1.0 / l_ref[0]) - acc_ref[1] * (lam / l_ref[1])
    gain = jnp.concatenate([sg_ref[...]] * (tq // LANES), axis=1) * (1.0 - lam_init)
    o_t = o_t * lax.rsqrt(jnp.mean(o_t * o_t, axis=0, keepdims=True) + SUBLN_EPS) * gain
    o_ref[...] = o_t.T.astype(BF16)


def _transpose_kernel(x_ref, o_ref):
    o_ref[...] = x_ref[...].astype(F32).T.astype(BF16)


def _diff_attention(qkv, lam_vecs, subln_g, lam_init, n_batch, seq, ctx_len, need_ctx):
    rows, width = qkv.shape
    d = width // 3
    hd = d // DIFF_HEADS
    tq, tk = ATTN_Q_TILE, ATTN_K_TILE
    nq = seq // tq
    assert seq % (2 * tk) == 0
    ctx_blk0 = (n_batch * seq) // ctx_len
    nkt = seq // tk
    scratch = lambda t: [pltpu.VMEM((2, 1, t), F32), pltpu.VMEM((2, 1, t), F32), pltpu.VMEM((2, hd, t), F32)]
    small = [pl.BlockSpec((4, hd // 2), lambda b, h, qi: (0, 0)),
             pl.BlockSpec((hd, LANES), lambda b, h, qi: (0, 0))]
    ctx_kv = [pl.BlockSpec((ctx_len, hd), lambda b, h, qi: (ctx_blk0 + b, DIFF_HEADS + h)),
              pl.BlockSpec((ctx_len, hd), lambda b, h, qi: (ctx_blk0 + b, 2 * DIFF_HEADS + h))]
    def transposed(tile, col_blk, name):
        n = seq // tile
        return pl.pallas_call(
            _transpose_kernel,
            grid=(n_batch, n),
            in_specs=[pl.BlockSpec((tile, d), lambda b, j: (b * n + j, col_blk))],
            out_specs=pl.BlockSpec((None, None, d, tile), lambda b, j: (b, j, 0, 0)),
            out_shape=jax.ShapeDtypeStruct((n_batch, n, d, tile), BF16),
            compiler_params=_params("parallel", "parallel"),
            name=name,
        )(qkv)

    v_t = transposed(tk, 2, "value_transpose")
    q_t = transposed(tq, 0, "query_transpose")
    y = pl.pallas_call(
        functools.partial(_attn_kernel, tk=tk, n_kt=nkt, lam_init=lam_init),
        grid=(n_batch, DIFF_HEADS, nq),
        in_specs=small + [pl.BlockSpec((None, None, hd, tq), lambda b, h, qi: (b, qi, h, 0))] + ctx_kv + [
            pl.BlockSpec((seq, hd), lambda b, h, qi: (b, DIFF_HEADS + h)),
            pl.BlockSpec((None, nkt, hd, tk), lambda b, h, qi: (b, 0, h, 0)),
        ],
        out_specs=pl.BlockSpec((tq, hd), lambda b, h, qi: (b * nq + qi, h)),
        out_shape=jax.ShapeDtypeStruct((rows, d), BF16),
        scratch_shapes=scratch(tq) + [pltpu.VMEM((2, 2, tk, tq), F32), pltpu.VMEM((2, 2, 1, tq), F32)],
        compiler_params=_params("parallel", "parallel", "arbitrary"),
        name="diff_attention_latent",
    )(lam_vecs, subln_g, q_t, qkv, qkv, qkv, v_t)
    if not need_ctx:
        return y
    return pl.pallas_call(
        functools.partial(_attn_kernel, tk=tk, n_kt=0, lam_init=lam_init),
        grid=(n_batch, DIFF_HEADS, 1),
        in_specs=small + [pl.BlockSpec((ctx_len, hd), lambda b, h, qi: (ctx_blk0 + b, h))] + ctx_kv + [
            pl.BlockSpec(memory_space=pl.ANY)],
        out_specs=pl.BlockSpec((ctx_len, hd), lambda b, h, qi: (ctx_blk0 + b, h)),
        out_shape=jax.ShapeDtypeStruct((rows, d), BF16),
        scratch_shapes=scratch(ctx_len),
        input_output_aliases={5: 0},
        compiler_params=_params("parallel", "parallel", "arbitrary"),
        name="diff_attention_context",
    )(lam_vecs, subln_g, qkv, qkv, qkv, y)


def _ret_fwd_kernel(lg_ref, q_ref, k_ref, v_ref, o_ref, d_ref, s_ref):
    n = pl.program_id(1)
    c = q_ref.shape[0]
    hd = q_ref.shape[1] // RET_HEADS

    @pl.when(n == 0)
    def _():
        s_ref[...] = jnp.zeros(s_ref.shape, F32)
        i = lax.broadcasted_iota(jnp.int32, (c, c), 0)
        j = lax.broadcasted_iota(jnp.int32, (c, c), 1)
        dist = (i - j).astype(F32)
        for h in range(RET_HEADS):
            fwd = jnp.exp(lg_ref[0, h] * jnp.maximum(dist, 0.0))
            bwd = jnp.exp(lg_ref[1, h] * jnp.maximum(-dist, 0.0))
            d_ref[h] = jnp.where(dist >= 0, fwd, bwd)

    pos = lax.broadcasted_iota(jnp.int32, (c, 1), 0).astype(F32)
    for h in range(RET_HEADS):
        lg = lg_ref[0, h]
        sl = slice(h * hd, (h + 1) * hd)
        qh, kh, vh = q_ref[:, sl], k_ref[:, sl], v_ref[:, sl]
        p = (_dot_nt(qh, kh) * d_ref[h]).astype(BF16)
        qd = (qh.astype(F32) * jnp.exp(lg * (pos + 1.0))).astype(BF16)
        s = s_ref[h]
        o_ref[:, sl] = _dot(p, vh) + _dot(qd, s.astype(BF16))
        kd = (kh.astype(F32) * jnp.exp(lg * (c - 1.0 - pos))).astype(BF16)
        s_ref[h] = jnp.exp(jnp.full((1, 1), lg * c, F32)) * s + _dot_tn(kd, vh)


def _ret_bwd_kernel(lg_ref, q_ref, k_ref, v_ref, g_ref, r_ref, o_ref, s_ref):
    n = pl.program_id(1)
    c = q_ref.shape[0]
    hd = q_ref.shape[1] // RET_HEADS

    @pl.when(n == 0)
    def _():
        s_ref[...] = jnp.zeros(s_ref.shape, F32)

    pos = lax.broadcasted_iota(jnp.int32, (c, 1), 0).astype(F32)
    for h in range(RET_HEADS):
        lg = lg_ref[1, h]
        sl = slice(h * hd, (h + 1) * hd)
        qh, kh, vh = q_ref[:, sl], k_ref[:, sl], v_ref[:, sl]
        qd = (qh.astype(F32) * jnp.exp(lg * (c - pos))).astype(BF16)
        s = s_ref[h]
        r = r_ref[:, sl] + _dot(qd, s.astype(BF16))
        y = r * lax.rsqrt(jnp.mean(r * r, axis=-1, keepdims=True) + NORM_EPS)
        o_ref[:, sl] = (y * _silu(g_ref[:, sl].astype(F32))).astype(BF16)
        kd = (kh.astype(F32) * jnp.exp(lg * pos)).astype(BF16)
        s_ref[h] = jnp.exp(jnp.full((1, 1), lg * c, F32)) * s + _dot_tn(kd, vh)


def _retention(qkv, log_decay, n_batch, seq, ctx_len):
    rows = qkv.shape[0]
    c = ctx_len
    w = (qkv.shape[1] // 5)
    nc = seq // c
    ctx_blk0 = (n_batch * seq) // c
    fwd_blk = lambda b, n: jnp.where(n == 0, ctx_blk0 + b, b * nc + n - 1)
    bwd_blk = lambda b, n: jnp.where(n == 0, ctx_blk0 + b, b * nc + nc - n)
    smem = pl.BlockSpec(memory_space=pltpu.SMEM)
    state = pltpu.VMEM((RET_HEADS, w // RET_HEADS, w // RET_HEADS), F32)
    r1 = pl.pallas_call(
        _ret_fwd_kernel,
        grid=(n_batch, nc + 1),
        in_specs=[smem] + [pl.BlockSpec((c, w), functools.partial(lambda b, n, j: (fwd_blk(b, n), j), j=j)) for j in range(3)],
        out_specs=pl.BlockSpec((c, w), lambda b, n: (fwd_blk(b, n), 0)),
        out_shape=jax.ShapeDtypeStruct((rows, w), F32),
        scratch_shapes=[pltpu.VMEM((RET_HEADS, c, c), F32), state],
        compiler_params=_params("parallel", "arbitrary"),
        name="retention_forward_sweep",
    )(log_decay, qkv, qkv, qkv)
    return pl.pallas_call(
        _ret_bwd_kernel,
        grid=(n_batch, nc + 1),
        in_specs=[smem] + [pl.BlockSpec((c, w), functools.partial(lambda b, n, j: (bwd_blk(b, n), j), j=j)) for j in range(4)]
        + [pl.BlockSpec((c, w), lambda b, n: (bwd_blk(b, n), 0))],
        out_specs=pl.BlockSpec((c, w), lambda b, n: (bwd_blk(b, n), 0)),
        out_shape=jax.ShapeDtypeStruct((rows, w), BF16),
        scratch_shapes=[state],
        compiler_params=_params("parallel", "arbitrary"),
        name="retention_backward_sweep",
    )(log_decay, qkv, qkv, qkv, qkv, r1)


def _pool_kernel(prev_ref, main_ref, next_ref, w_ref, sc_ref, o_ref, *, lat_tiles, tiles_per_seq, seq, ctx_len):
    i = pl.program_id(0)
    tp = main_ref.shape[0]
    gw = main_ref.shape[1] // len(POOL_WINDOWS)
    is_lat = i < lat_tiles
    t0 = jnp.where(is_lat, (i % tiles_per_seq) * tp, 0)
    length = jnp.where(is_lat, seq, ctx_len)
    n_ext = tp + 2 * POOL_HALO
    t = t0 - POOL_HALO + lax.broadcasted_iota(jnp.int32, (n_ext, 1), 0)
    valid = (t >= 0) & (t < length)
    tm = t[POOL_HALO:POOL_HALO + tp]
    for gi, win in enumerate(POOL_WINDOWS):
        sl = slice(gi * gw, (gi + 1) * gw)
        ext = jnp.concatenate([prev_ref[:, sl], main_ref[:, sl], next_ref[:, sl]], axis=0).astype(F32)
        x = jnp.where(valid, ext, 0.0)
        acc = x + pltpu.roll(x, 1, 0)
        half = 1
        while 2 * half < win:
            acc = pltpu.roll(acc, n_ext - half, 0) + pltpu.roll(acc, half, 0)
            half *= 2
        cnt = (jnp.minimum(tm + win // 2, length) - jnp.maximum(tm - win // 2, 0)).astype(F32)
        pooled = acc[POOL_HALO:POOL_HALO + tp] / cnt - x[POOL_HALO:POOL_HALO + tp]
        o_ref[:, sl] = (_dot(pooled.astype(BF16), w_ref[gi]) * sc_ref[:, sl]).astype(BF16)


def _pool(qkv, pool_w, pool_scale, idx, n_batch, seq, ctx_len):
    rows = qkv.shape[0]
    w = qkv.shape[1] // 5
    tp, halo = POOL_TILE, POOL_HALO
    assert ctx_len == tp and seq % tp == 0
    last_halo_blk = rows // halo - 1
    per = tp // halo
    return pl.pallas_call(
        functools.partial(_pool_kernel, lat_tiles=(n_batch * seq) // tp, tiles_per_seq=seq // tp, seq=seq, ctx_len=ctx_len),
        grid=(rows // tp,),
        in_specs=[
            pl.BlockSpec((halo, w), lambda i: (jnp.maximum(i * per - 1, 0), 4)),
            pl.BlockSpec((tp, w), lambda i: (i, 4)),
            pl.BlockSpec((halo, w), lambda i: (jnp.minimum((i + 1) * per, last_halo_blk), 4)),
            pl.BlockSpec((None,) + pool_w.shape[1:], lambda i: (idx, 0, 0, 0)),
            pl.BlockSpec((None, 1, w), lambda i: (idx, 0, 0)),
        ],
        out_specs=pl.BlockSpec((tp, w), lambda i: (i, 0)),
        out_shape=jax.ShapeDtypeStruct((rows, w), BF16),
        compiler_params=_params("parallel"),
        name="multiscale_pool",
    )(qkv, qkv, qkv, pool_w, pool_scale.reshape(pool_scale.shape[0], 1, w))


def _out_kernel(h_ref, m_ref, g_ref, ya_ref, yb_ref, wa_ref, wb_ref, o_ref):
    y = _dot(ya_ref[...], wa_ref[...]) + _dot(yb_ref[...], wb_ref[...])
    o_ref[...] = h_ref[...] + m_ref[5:6, :] * _rms(y, g_ref[3:4, :])


def _out_proj(h, mods, norm_g, ya, ya_blk, yb, yb_blk, w_out, layer, idx, n_tiles, grp):
    d = h.shape[1]
    tm = ROW_TILE
    return pl.pallas_call(
        _out_kernel,
        grid=(n_tiles,),
        in_specs=[
            pl.BlockSpec((tm, d), lambda i: (i, 0)),
            pl.BlockSpec((None, None, N_MOD, d), lambda i: (layer, grp(i), 0, 0)),
            pl.BlockSpec((None, 6, d), lambda i: (layer, 0, 0)),
            pl.BlockSpec((tm, d // 2), lambda i: (i, ya_blk)),
            pl.BlockSpec((tm, d // 2), lambda i: (i, yb_blk)),
            pl.BlockSpec((None, d // 2, d), lambda i: (idx, 0, 0)),
            pl.BlockSpec((None, d // 2, d), lambda i: (idx, 1, 0)),
        ],
        out_specs=pl.BlockSpec((tm, d), lambda i: (i, 0)),
        out_shape=jax.ShapeDtypeStruct((n_tiles * tm, d), F32),
        compiler_params=_params("parallel"),
        name=f"mixer_out_l{layer}",
    )(h, mods, norm_g, ya, yb, w_out, w_out)


def kernel(x, c, ctx, c_ctx, mod_w, mod_b, norm_g, ffn_w_gate, ffn_w_up, ffn_w_down, ev_w_in, ev_w_out, ret_decay, pool_w, pool_scale, od_w_in, od_w_out, lam_q1, lam_k1, lam_q2, lam_k2, diff_subln_g):
    n_batch, seq, d = x.shape
    ctx_len = ctx.shape[1]
    depth = mod_w.shape[0]
    tm = ROW_TILE
    lat_rows, ctx_rows = n_batch * seq, n_batch * ctx_len
    assert seq % tm == 0 and ctx_rows % tm == 0 and n_batch + 1 <= 8
    assert d // 2 // RET_HEADS == 2 * LANES and d // DIFF_HEADS == 2 * LANES and ctx_len == POOL_TILE
    lat_tiles, all_tiles = lat_rows // tm, (lat_rows + ctx_rows) // tm
    tiles_per_batch = seq // tm
    grp = lambda i: jnp.minimum(i // tiles_per_batch, n_batch)
    assert seq % WIDE_ROW_TILE == 0
    wide_per_batch = seq // WIDE_ROW_TILE
    wide_grp = lambda i: jnp.minimum(i // wide_per_batch, n_batch)
    rope_blk = lambda i: jnp.where(i < n_batch * wide_per_batch, i % wide_per_batch, wide_per_batch)

    cond = jnp.concatenate([c, c_ctx[None, :], jnp.zeros((8 - n_batch - 1, d), F32)], axis=0)
    mods = _modulation(cond, mod_w, mod_b)[:, :n_batch + 1].reshape(depth, n_batch + 1, N_MOD, d)

    wg, wu, wd = ffn_w_gate.astype(BF16), ffn_w_up.astype(BF16), ffn_w_down.astype(BF16)
    ev_in, ev_out, od_in, od_out = ev_w_in.astype(BF16), ev_w_out.astype(BF16), od_w_in.astype(BF16), od_w_out.astype(BF16)
    pw = pool_w.astype(BF16)
    log_decay = jax.nn.log_sigmoid(ret_decay.astype(F32))
    tabs_even = _rope_tables(seq, WIDE_ROW_TILE, True)
    tabs_odd = _rope_tables(seq, WIDE_ROW_TILE, False)

    h = jnp.concatenate([x.reshape(lat_rows, d), ctx.reshape(ctx_rows, d)], axis=0)
    for l in range(depth):
        need_ctx = l < depth - 1
        out_tiles = all_tiles if need_ctx else lat_tiles
        h = _ffn(h, mods, norm_g, wg, wu, wd, l, 0, lat_rows + ctx_rows, wide_grp)
        if l % 2 == 0:
            e = l // 2
            qkv = _proj_in(h, mods, norm_g, ev_in, tabs_even, l, e, True, wide_grp, rope_blk)
            y = _retention(qkv, log_decay[e], n_batch, seq, ctx_len)
            pm = _pool(qkv, pw, pool_scale, e, n_batch, seq, ctx_len)
            h = _out_proj(h, mods, norm_g, y, 0, pm, 0, ev_out, l, e, out_tiles, grp)
        else:
            o = l // 2
            lam_init = 0.8 - 0.6 * math.exp(-0.3 * l)
            qkv = _proj_in(h, mods, norm_g, od_in, tabs_odd, l, o, False, wide_grp, rope_blk)
            lam_vecs = jnp.stack([lam_q1[o], lam_k1[o], lam_q2[o], lam_k2[o]], axis=0)
            gain = jnp.broadcast_to(diff_subln_g[o][:, None], (diff_subln_g.shape[1], LANES))
            y = _diff_attention(qkv, lam_vecs, gain, lam_init, n_batch, seq, ctx_len, need_ctx)
            h = _out_proj(h, mods, norm_g, y, 0, y, 1, od_out, l, o, out_tiles, grp)
        h = _ffn(h, mods, norm_g, wg, wu, wd, l, 1, out_tiles * tm, wide_grp)
    return h[:lat_rows].reshape(n_batch, seq, d)
```

```python
import functools
import math

import jax
import jax.numpy as jnp
from jax import lax
from jax.experimental import pallas as pl
from jax.experimental.pallas import tpu as pltpu

F32 = jnp.float32
BF16 = jnp.bfloat16

GRID_W = 64
ROPE_BASE = 10000.0
N_MOD = 9
RET_HEADS = 4
POOL_WINDOWS = (2, 4, 8, 16)
DIFF_HEADS = 8
NORM_EPS = 1e-6
SUBLN_EPS = 1e-5

LANES = 128
ROW_TILE = 512
WIDE_ROW_TILE = 1024
FFN_CHUNK = 512
NORM_STRIP = 16
PROJ_CHUNK = 1024
MOD_CHUNK = 1024
ATTN_Q_TILE = 1024
ATTN_K_TILE = 512
POOL_TILE = 256
POOL_HALO = 16
VMEM_LIMIT = 56 * 1024 * 1024
FFN_VMEM_LIMIT = 60 * 1024 * 1024


def _params(*sem):
    return pltpu.CompilerParams(dimension_semantics=sem, vmem_limit_bytes=VMEM_LIMIT)


def _silu(x):
    return x * jax.nn.sigmoid(x)


def _rms(x, g, eps=NORM_EPS):
    return x * lax.rsqrt(jnp.mean(x * x, axis=-1, keepdims=True) + eps) * g


def _modulated_norm_strips(h_ref, u_ref, g, shift, scale, zero_ref=None):
    gain = g * (1.0 + scale)

    def strip(i, carry):
        rows = pl.ds(pl.multiple_of(i * NORM_STRIP, NORM_STRIP), NORM_STRIP)
        x = h_ref[rows, :]
        inv = lax.rsqrt(jnp.mean(x * x, axis=-1, keepdims=True) + NORM_EPS)
        u_ref[rows, :] = (x * inv * gain + shift).astype(BF16)
        if zero_ref is not None:
            zero_ref[rows, :] = jnp.zeros((NORM_STRIP, zero_ref.shape[1]), F32)
        return carry

    lax.fori_loop(0, h_ref.shape[0] // NORM_STRIP, strip, 0, unroll=8)


def _dot(a, b):
    return jnp.dot(a, b, preferred_element_type=F32)


def _dot_nt(a, b):
    return lax.dot_general(a, b, (((1,), (1,)), ((), ())), preferred_element_type=F32)


def _dot_tn(a, b):
    return lax.dot_general(a, b, (((0,), (0,)), ((), ())), preferred_element_type=F32)


def _mod_kernel(a_ref, w_ref, b_ref, o_ref):
    s = _silu(a_ref[...]).astype(BF16)
    o_ref[...] = _dot(s, w_ref[...].astype(BF16)) + b_ref[...]


def _modulation(cond, mod_w, mod_b):
    depth, d, n = mod_w.shape
    tn = MOD_CHUNK
    return pl.pallas_call(
        _mod_kernel,
        grid=(depth, n // tn),
        in_specs=[
            pl.BlockSpec((8, d), lambda l, j: (0, 0)),
            pl.BlockSpec((None, d, tn), lambda l, j: (l, 0, j)),
            pl.BlockSpec((None, 1, tn), lambda l, j: (l, 0, j)),
        ],
        out_specs=pl.BlockSpec((None, 8, tn), lambda l, j: (l, 0, j)),
        out_shape=jax.ShapeDtypeStruct((depth, 8, n), F32),
        compiler_params=_params("parallel", "parallel"),
        name="adaln_modulation",
    )(cond, mod_w, mod_b.reshape(depth, 1, n))


def _ffn_kernel(h_ref, m_ref, g_ref, wg_ref, wu_ref, wd_ref, o_ref, u_ref, *, mod0, g0):
    k = pl.program_id(1)
    sub = ROW_TILE
    n_strips = h_ref.shape[0] // NORM_STRIP

    @pl.when(k == 0)
    def _():
        _modulated_norm_strips(h_ref, u_ref, g_ref[g0:g0 + 1, :], m_ref[mod0:mod0 + 1, :], m_ref[mod0 + 1:mod0 + 2, :],
                               zero_ref=o_ref)

    for s in range(h_ref.shape[0] // sub):
        rows = slice(s * sub, (s + 1) * sub)
        u = u_ref[rows, :]
        a = _dot(u, wg_ref[...])
        b = _dot(u, wu_ref[...])
        o_ref[rows, :] += _dot((_silu(a) * b).astype(BF16), wd_ref[...])

    @pl.when(k == pl.num_programs(1) - 1)
    def _():
        gain = (0.5 * m_ref[mod0 + 2:mod0 + 3, :]) * g_ref[g0 + 1:g0 + 2, :]

        for i in range(n_strips):
            rows = slice(i * NORM_STRIP, (i + 1) * NORM_STRIP)
            y = o_ref[rows, :]
            inv = lax.rsqrt(jnp.mean(y * y, axis=-1, keepdims=True) + NORM_EPS)
            o_ref[rows, :] = h_ref[rows, :] + y * inv * gain


def _ffn(h, mods, norm_g, wg, wu, wd, layer, half, out_rows, grp):
    d = h.shape[1]
    f = wg.shape[-1]
    tm, th = WIDE_ROW_TILE, FFN_CHUNK
    return pl.pallas_call(
        functools.partial(_ffn_kernel, mod0=6 * half, g0=4 * half),
        grid=(pl.cdiv(out_rows, tm), f // th),
        in_specs=[
            pl.BlockSpec((tm, d), lambda i, k: (i, 0)),
            pl.BlockSpec((None, None, N_MOD, d), lambda i, k: (layer, grp(i), 0, 0)),
            pl.BlockSpec((None, 6, d), lambda i, k: (layer, 0, 0)),
            pl.BlockSpec((None, None, d, th), lambda i, k: (layer, half, 0, k)),
            pl.BlockSpec((None, None, d, th), lambda i, k: (layer, half, 0, k)),
            pl.BlockSpec((None, None, th, d), lambda i, k: (layer, half, k, 0)),
        ],
        out_specs=pl.BlockSpec((tm, d), lambda i, k: (i, 0)),
        out_shape=jax.ShapeDtypeStruct((out_rows, d), F32),
        scratch_shapes=[pltpu.VMEM((tm, d), BF16)],
        compiler_params=pltpu.CompilerParams(dimension_semantics=("parallel", "arbitrary"),
                                             vmem_limit_bytes=FFN_VMEM_LIMIT),
        name=f"swiglu_l{layer}_{half}",
    )(h, mods, norm_g, wg, wu, wd)


def _proj_kernel(*refs, even, n_rope, scale):
    if even:
        h_ref, m_ref, g_ref, w_ref, c_ref, sa_ref, o_ref, u_ref = refs
    else:
        h_ref, m_ref, g_ref, w_ref, c_ref, sa_ref, sb_ref, o_ref, u_ref = refs
    n = pl.program_id(1)

    @pl.when(n == 0)
    def _():
        _modulated_norm_strips(h_ref, u_ref, g_ref[2:3, :], m_ref[3:4, :], m_ref[4:5, :])

    r = _dot(u_ref[...], w_ref[...])

    is_rope = n < n_rope
    first = n < n_rope // 2
    sc = jnp.where(is_rope, jnp.where(first, 1.0, scale) if even else jnp.where(first, scale, 1.0), 1.0)
    cos = jnp.where(is_rope, c_ref[...], 1.0) * sc
    sin_a = jnp.where(is_rope, sa_ref[...], 0.0) * sc
    if not even:
        sin_b = jnp.where(is_rope, sb_ref[...], 0.0) * sc
    half, quarter = LANES // 2, LANES // 4
    for j in range(r.shape[1] // LANES):
        xb = r[:, j * LANES:(j + 1) * LANES]
        if even:
            tb = (j % 2) * LANES
            ob = xb * cos[:, tb:tb + LANES] + pltpu.roll(xb, half, 1) * sin_a[:, tb:tb + LANES]
        else:
            ob = xb * cos + pltpu.roll(xb, LANES - quarter, 1) * sin_a + pltpu.roll(xb, quarter, 1) * sin_b
        o_ref[:, j * LANES:(j + 1) * LANES] = ob.astype(BF16)


def _proj_in(h, mods, norm_g, w_in, tables, layer, idx, even, grp, rope_blk):
    out_rows, d = h.shape
    n_out = w_in.shape[-1]
    tm, tn = WIDE_ROW_TILE, PROJ_CHUNK
    tw = tables[0].shape[1]
    if even:
        n_rope, scale = (2 * (d // 2)) // tn, float((d // 2 // RET_HEADS) ** -0.5)
    else:
        n_rope, scale = (2 * d) // tn, float((d // DIFF_HEADS // 2) ** -0.5 * math.log2(math.e))
    return pl.pallas_call(
        functools.partial(_proj_kernel, even=even, n_rope=n_rope, scale=scale),
        grid=(pl.cdiv(out_rows, tm), n_out // tn),
        in_specs=[
            pl.BlockSpec((tm, d), lambda i, n: (i, 0)),
            pl.BlockSpec((None, None, N_MOD, d), lambda i, n: (layer, grp(i), 0, 0)),
            pl.BlockSpec((None, 6, d), lambda i, n: (layer, 0, 0)),
            pl.BlockSpec((None, d, tn), lambda i, n: (idx, 0, n)),
        ] + [pl.BlockSpec((tm, tw), lambda i, n: (rope_blk(i), 0)) for _ in tables],
        out_specs=pl.BlockSpec((tm, tn), lambda i, n: (i, n)),
        out_shape=jax.ShapeDtypeStruct((out_rows, n_out), BF16),
        scratch_shapes=[pltpu.VMEM((tm, d), BF16)],
        compiler_params=_params("parallel", "arbitrary"),
        name=f"mixer_in_l{layer}",
    )(h, mods, norm_g, w_in, *tables)


def _rope_tables(seq, n_ident, even):
    n_rows = seq // GRID_W
    nf = (2 * LANES if even else LANES) // 4
    inv = ROPE_BASE ** (-jnp.arange(nf, dtype=F32) / nf)
    ar = jnp.arange(n_rows, dtype=F32)[:, None] * inv[None, :]
    ac = jnp.arange(GRID_W, dtype=F32)[:, None] * inv[None, :]
    cr, sr = jnp.repeat(jnp.cos(ar), GRID_W, axis=0), jnp.repeat(jnp.sin(ar), GRID_W, axis=0)
    cc, sc = jnp.tile(jnp.cos(ac), (n_rows, 1)), jnp.tile(jnp.sin(ac), (n_rows, 1))
    z = jnp.zeros_like(sr)
    if even:
        tabs = [jnp.concatenate([cr, cr, cc, cc], 1), jnp.concatenate([-sr, sr, -sc, sc], 1)]
    else:
        tabs = [jnp.concatenate([cr, cr, cc, cc], 1), jnp.concatenate([-sr, z, -sc, z], 1),
                jnp.concatenate([z, sr, z, sc], 1)]
    ident = [jnp.ones((n_ident, tabs[0].shape[1]), F32)] + [jnp.zeros((n_ident, tabs[0].shape[1]), F32)] * (len(tabs) - 1)
    return [jnp.concatenate([a, b], 0) for a, b in zip(tabs, ident)]


def _attn_kernel(*refs, tk, n_kt, lam_init):
    if n_kt:
        lam_ref, sg_ref, q_ref, kc_ref, vc_ref, k_ref, vt_ref, o_ref, m_ref, l_ref, acc_ref, s_ref, mx_ref = refs
    else:
        lam_ref, sg_ref, q_ref, kc_ref, vc_ref, _, o_ref, m_ref, l_ref, acc_ref = refs
    if n_kt:
        dh, tq = q_ref.shape[0] // 2, q_ref.shape[1]

        def scores_t(kb, mi):
            return _dot(kb[:, mi * dh:(mi + 1) * dh], q_ref[mi * dh:(mi + 1) * dh, :])
    else:
        dh, tq = q_ref.shape[1] // 2, q_ref.shape[0]

        def scores_t(kb, mi):
            return _dot_nt(kb[:, mi * dh:(mi + 1) * dh], q_ref[:, mi * dh:(mi + 1) * dh])

    def absorb(slot, vt, mi):
        m_prev = m_ref[mi]
        m_next = jnp.maximum(m_prev, mx_ref[slot, mi])
        alpha = jnp.exp2(m_prev - m_next)
        p_t = jnp.exp2(s_ref[slot, mi] - m_next)
        l_ref[mi] = alpha * l_ref[mi] + jnp.sum(p_t, axis=0, keepdims=True)
        acc_ref[mi] = alpha * acc_ref[mi] + _dot(vt, p_t.astype(BF16))
        m_ref[mi] = m_next

    if n_kt:
        def qk(j, slot):
            kb = k_ref[pl.ds(pl.multiple_of(j * tk, tk), tk), :]
            for mi in range(2):
                s_t = scores_t(kb, mi)
                s_ref[slot, mi] = s_t
                mx_ref[slot, mi] = jnp.max(s_t, axis=0, keepdims=True)

        def pv(j, slot):
            vt = vt_ref[j]
            for mi in range(2):
                absorb(slot, vt, mi)

        qk(0, 0)

    kc, vc = kc_ref[...], vc_ref[...]
    for mi in range(2):
        s_t = scores_t(kc, mi)
        m0 = jnp.max(s_t, axis=0, keepdims=True)
        p_t = jnp.exp2(s_t - m0)
        m_ref[mi] = m0
        l_ref[mi] = jnp.sum(p_t, axis=0, keepdims=True)
        acc_ref[mi] = _dot_tn(vc, p_t.astype(BF16))

    if n_kt:
        def body(i, carry):
            j = 2 * i
            qk(j + 1, 1)
            pv(j, 0)
            qk(j + 2, 0)
            pv(j + 1, 1)
            return carry

        lax.fori_loop(0, n_kt // 2 - 1, body, 0)
        qk(n_kt - 1, 1)
        pv(n_kt - 2, 0)
        pv(n_kt - 1, 1)

    t1 = jnp.sum(lam_ref[0:1, :] * lam_ref[1:2, :], axis=-1, keepdims=True)
    t2 = jnp.sum(lam_ref[2:3, :] * lam_ref[3:4, :], axis=-1, keepdims=True)
    lam = jnp.exp(t1) - jnp.exp(t2) + lam_init
    o_t = acc_ref[0] * (1.0 / l_ref[0]) - acc_ref[1] * (lam / l_ref[1])
    gain = jnp.concatenate([sg_ref[...]] * (tq // LANES), axis=1) * (1.0 - lam_init)
    o_t = o_t * lax.rsqrt(jnp.mean(o_t * o_t, axis=0, keepdims=True) + SUBLN_EPS) * gain
    o_ref[...] = o_t.T.astype(BF16)


def _transpose_kernel(x_ref, o_ref):
    o_ref[...] = x_ref[...].astype(F32).T.astype(BF16)


def _diff_attention(qkv, lam_vecs, subln_g, lam_init, n_batch, seq, ctx_len, need_ctx):
    rows, width = qkv.shape
    d = width // 3
    hd = d // DIFF_HEADS
    tq, tk = ATTN_Q_TILE, ATTN_K_TILE
    nq = seq // tq
    assert seq % (2 * tk) == 0
    ctx_blk0 = (n_batch * seq) // ctx_len
    nkt = seq // tk
    scratch = lambda t: [pltpu.VMEM((2, 1, t), F32), pltpu.VMEM((2, 1, t), F32), pltpu.VMEM((2, hd, t), F32)]
    small = [pl.BlockSpec((4, hd // 2), lambda b, h, qi: (0, 0)),
             pl.BlockSpec((hd, LANES), lambda b, h, qi: (0, 0))]
    ctx_kv = [pl.BlockSpec((ctx_len, hd), lambda b, h, qi: (ctx_blk0 + b, DIFF_HEADS + h)),
              pl.BlockSpec((ctx_len, hd), lambda b, h, qi: (ctx_blk0 + b, 2 * DIFF_HEADS + h))]
    def transposed(tile, col_blk, name):
        n = seq // tile
        return pl.pallas_call(
            _transpose_kernel,
            grid=(n_batch, n),
            in_specs=[pl.BlockSpec((tile, d), lambda b, j: (b * n + j, col_blk))],
            out_specs=pl.BlockSpec((None, None, d, tile), lambda b, j: (b, j, 0, 0)),
            out_shape=jax.ShapeDtypeStruct((n_batch, n, d, tile), BF16),
            compiler_params=_params("parallel", "parallel"),
            name=name,
        )(qkv)

    v_t = transposed(tk, 2, "value_transpose")
    q_t = transposed(tq, 0, "query_transpose")
    y = pl.pallas_call(
        functools.partial(_attn_kernel, tk=tk, n_kt=nkt, lam_init=lam_init),
        grid=(n_batch, DIFF_HEADS, nq),
        in_specs=small + [pl.BlockSpec((None, None, hd, tq), lambda b, h, qi: (b, qi, h, 0))] + ctx_kv + [
            pl.BlockSpec((seq, hd), lambda b, h, qi: (b, DIFF_HEADS + h)),
            pl.BlockSpec((None, nkt, hd, tk), lambda b, h, qi: (b, 0, h, 0)),
        ],
        out_specs=pl.BlockSpec((tq, hd), lambda b, h, qi: (b * nq + qi, h)),
        out_shape=jax.ShapeDtypeStruct((rows, d), BF16),
        scratch_shapes=scratch(tq) + [pltpu.VMEM((2, 2, tk, tq), F32), pltpu.VMEM((2, 2, 1, tq), F32)],
        compiler_params=_params("parallel", "parallel", "arbitrary"),
        name="diff_attention_latent",
    )(lam_vecs, subln_g, q_t, qkv, qkv, qkv, v_t)
    if not need_ctx:
        return y
    return pl.pallas_call(
        functools.partial(_attn_kernel, tk=tk, n_kt=0, lam_init=lam_init),
        grid=(n_batch, DIFF_HEADS, 1),
        in_specs=small + [pl.BlockSpec((ctx_len, hd), lambda b, h, qi: (ctx_blk0 + b, h))] + ctx_kv + [
            pl.BlockSpec(memory_space=pl.ANY)],
        out_specs=pl.BlockSpec((ctx_len, hd), lambda b, h, qi: (ctx_blk0 + b, h)),
        out_shape=jax.ShapeDtypeStruct((rows, d), BF16),
        scratch_shapes=scratch(ctx_len),
        input_output_aliases={5: 0},
        compiler_params=_params("parallel", "parallel", "arbitrary"),
        name="diff_attention_context",
    )(lam_vecs, subln_g, qkv, qkv, qkv, y)


def _ret_fwd_kernel(lg_ref, q_ref, k_ref, v_ref, o_ref, d_ref, s_ref):
    n = pl.program_id(1)
    c = q_ref.shape[0]
    hd = q_ref.shape[1] // RET_HEADS

    @pl.when(n == 0)
    def _():
        s_ref[...] = jnp.zeros(s_ref.shape, F32)
        i = lax.broadcasted_iota(jnp.int32, (c, c), 0)
        j = lax.broadcasted_iota(jnp.int32, (c, c), 1)
        dist = (i - j).astype(F32)
        for h in range(RET_HEADS):
            fwd = jnp.exp(lg_ref[0, h] * jnp.maximum(dist, 0.0))
            bwd = jnp.exp(lg_ref[1, h] * jnp.maximum(-dist, 0.0))
            d_ref[h] = jnp.where(dist >= 0, fwd, bwd)

    pos = lax.broadcasted_iota(jnp.int32, (c, 1), 0).astype(F32)
    for h in range(RET_HEADS):
        lg = lg_ref[0, h]
        sl = slice(h * hd, (h + 1) * hd)
        qh, kh, vh = q_ref[:, sl], k_ref[:, sl], v_ref[:, sl]
        p = (_dot_nt(qh, kh) * d_ref[h]).astype(BF16)
        qd = (qh.astype(F32) * jnp.exp(lg * (pos + 1.0))).astype(BF16)
        s = s_ref[h]
        o_ref[:, sl] = _dot(p, vh) + _dot(qd, s.astype(BF16))
        kd = (kh.astype(F32) * jnp.exp(lg * (c - 1.0 - pos))).astype(BF16)
        s_ref[h] = jnp.exp(jnp.full((1, 1), lg * c, F32)) * s + _dot_tn(kd, vh)


def _ret_bwd_kernel(lg_ref, q_ref, k_ref, v_ref, g_ref, r_ref, o_ref, s_ref):
    n = pl.program_id(1)
    c = q_ref.shape[0]
    hd = q_ref.shape[1] // RET_HEADS

    @pl.when(n == 0)
    def _():
        s_ref[...] = jnp.zeros(s_ref.shape, F32)

    pos = lax.broadcasted_iota(jnp.int32, (c, 1), 0).astype(F32)
    for h in range(RET_HEADS):
        lg = lg_ref[1, h]
        sl = slice(h * hd, (h + 1) * hd)
        qh, kh, vh = q_ref[:, sl], k_ref[:, sl], v_ref[:, sl]
        qd = (qh.astype(F32) * jnp.exp(lg * (c - pos))).astype(BF16)
        s = s_ref[h]
        r = r_ref[:, sl] + _dot(qd, s.astype(BF16))
        y = r * lax.rsqrt(jnp.mean(r * r, axis=-1, keepdims=True) + NORM_EPS)
        o_ref[:, sl] = (y * _silu(g_ref[:, sl].astype(F32))).astype(BF16)
        kd = (kh.astype(F32) * jnp.exp(lg * pos)).astype(BF16)
        s_ref[h] = jnp.exp(jnp.full((1, 1), lg * c, F32)) * s + _dot_tn(kd, vh)


def _retention(qkv, log_decay, n_batch, seq, ctx_len):
    rows = qkv.shape[0]
    c = ctx_len
    w = (qkv.shape[1] // 5)
    nc = seq // c
    ctx_blk0 = (n_batch * seq) // c
    fwd_blk = lambda b, n: jnp.where(n == 0, ctx_blk0 + b, b * nc + n - 1)
    bwd_blk = lambda b, n: jnp.where(n == 0, ctx_blk0 + b, b * nc + nc - n)
    smem = pl.BlockSpec(memory_space=pltpu.SMEM)
    state = pltpu.VMEM((RET_HEADS, w // RET_HEADS, w // RET_HEADS), F32)
    r1 = pl.pallas_call(
        _ret_fwd_kernel,
        grid=(n_batch, nc + 1),
        in_specs=[smem] + [pl.BlockSpec((c, w), functools.partial(lambda b, n, j: (fwd_blk(b, n), j), j=j)) for j in range(3)],
        out_specs=pl.BlockSpec((c, w), lambda b, n: (fwd_blk(b, n), 0)),
        out_shape=jax.ShapeDtypeStruct((rows, w), F32),
        scratch_shapes=[pltpu.VMEM((RET_HEADS, c, c), F32), state],
        compiler_params=_params("parallel", "arbitrary"),
        name="retention_forward_sweep",
    )(log_decay, qkv, qkv, qkv)
    return pl.pallas_call(
        _ret_bwd_kernel,
        grid=(n_batch, nc + 1),
        in_specs=[smem] + [pl.BlockSpec((c, w), functools.partial(lambda b, n, j: (bwd_blk(b, n), j), j=j)) for j in range(4)]
        + [pl.BlockSpec((c, w), lambda b, n: (bwd_blk(b, n), 0))],
        out_specs=pl.BlockSpec((c, w), lambda b, n: (bwd_blk(b, n), 0)),
        out_shape=jax.ShapeDtypeStruct((rows, w), BF16),
        scratch_shapes=[state],
        compiler_params=_params("parallel", "arbitrary"),
        name="retention_backward_sweep",
    )(log_decay, qkv, qkv, qkv, qkv, r1)


def _pool_kernel(prev_ref, main_ref, next_ref, w_ref, sc_ref, o_ref, *, lat_tiles, tiles_per_seq, seq, ctx_len):
    i = pl.program_id(0)
    tp = main_ref.shape[0]
    gw = main_ref.shape[1] // len(POOL_WINDOWS)
    is_lat = i < lat_tiles
    t0 = jnp.where(is_lat, (i % tiles_per_seq) * tp, 0)
    length = jnp.where(is_lat, seq, ctx_len)
    n_ext = tp + 2 * POOL_HALO
    t = t0 - POOL_HALO + lax.broadcasted_iota(jnp.int32, (n_ext, 1), 0)
    valid = (t >= 0) & (t < length)
    tm = t[POOL_HALO:POOL_HALO + tp]
    for gi, win in enumerate(POOL_WINDOWS):
        sl = slice(gi * gw, (gi + 1) * gw)
        ext = jnp.concatenate([prev_ref[:, sl], main_ref[:, sl], next_ref[:, sl]], axis=0).astype(F32)
        x = jnp.where(valid, ext, 0.0)
        acc = x + pltpu.roll(x, 1, 0)
        half = 1
        while 2 * half < win:
            acc = pltpu.roll(acc, n_ext - half, 0) + pltpu.roll(acc, half, 0)
            half *= 2
        cnt = (jnp.minimum(tm + win // 2, length) - jnp.maximum(tm - win // 2, 0)).astype(F32)
        pooled = acc[POOL_HALO:POOL_HALO + tp] / cnt - x[POOL_HALO:POOL_HALO + tp]
        o_ref[:, sl] = (_dot(pooled.astype(BF16), w_ref[gi]) * sc_ref[:, sl]).astype(BF16)


def _pool(qkv, pool_w, pool_scale, idx, n_batch, seq, ctx_len):
    rows = qkv.shape[0]
    w = qkv.shape[1] // 5
    tp, halo = POOL_TILE, POOL_HALO
    assert ctx_len == tp and seq % tp == 0
    last_halo_blk = rows // halo - 1
    per = tp // halo
    return pl.pallas_call(
        functools.partial(_pool_kernel, lat_tiles=(n_batch * seq) // tp, tiles_per_seq=seq // tp, seq=seq, ctx_len=ctx_len),
        grid=(rows // tp,),
        in_specs=[
            pl.BlockSpec((halo, w), lambda i: (jnp.maximum(i * per - 1, 0), 4)),
            pl.BlockSpec((tp, w), lambda i: (i, 4)),
            pl.BlockSpec((halo, w), lambda i: (jnp.minimum((i + 1) * per, last_halo_blk), 4)),
            pl.BlockSpec((None,) + pool_w.shape[1:], lambda i: (idx, 0, 0, 0)),
            pl.BlockSpec((None, 1, w), lambda i: (idx, 0, 0)),
        ],
        out_specs=pl.BlockSpec((tp, w), lambda i: (i, 0)),
        out_shape=jax.ShapeDtypeStruct((rows, w), BF16),
        compiler_params=_params("parallel"),
        name="multiscale_pool",
    )(qkv, qkv, qkv, pool_w, pool_scale.reshape(pool_scale.shape[0], 1, w))


def _out_kernel(h_ref, m_ref, g_ref, ya_ref, yb_ref, wa_ref, wb_ref, o_ref):
    y = _dot(ya_ref[...], wa_ref[...]) + _dot(yb_ref[...], wb_ref[...])
    o_ref[...] = h_ref[...] + m_ref[5:6, :] * _rms(y, g_ref[3:4, :])


def _out_proj(h, mods, norm_g, ya, ya_blk, yb, yb_blk, w_out, layer, idx, n_tiles, grp):
    d = h.shape[1]
    tm = ROW_TILE
    return pl.pallas_call(
        _out_kernel,
        grid=(n_tiles,),
        in_specs=[
            pl.BlockSpec((tm, d), lambda i: (i, 0)),
            pl.BlockSpec((None, None, N_MOD, d), lambda i: (layer, grp(i), 0, 0)),
            pl.BlockSpec((None, 6, d), lambda i: (layer, 0, 0)),
            pl.BlockSpec((tm, d // 2), lambda i: (i, ya_blk)),
            pl.BlockSpec((tm, d // 2), lambda i: (i, yb_blk)),
            pl.BlockSpec((None, d // 2, d), lambda i: (idx, 0, 0)),
            pl.BlockSpec((None, d // 2, d), lambda i: (idx, 1, 0)),
        ],
        out_specs=pl.BlockSpec((tm, d), lambda i: (i, 0)),
        out_shape=jax.ShapeDtypeStruct((n_tiles * tm, d), F32),
        compiler_params=_params("parallel"),
        name=f"mixer_out_l{layer}",
    )(h, mods, norm_g, ya, yb, w_out, w_out)


def kernel(x, c, ctx, c_ctx, mod_w, mod_b, norm_g, ffn_w_gate, ffn_w_up, ffn_w_down, ev_w_in, ev_w_out, ret_decay, pool_w, pool_scale, od_w_in, od_w_out, lam_q1, lam_k1, lam_q2, lam_k2, diff_subln_g):
    n_batch, seq, d = x.shape
    ctx_len = ctx.shape[1]
    depth = mod_w.shape[0]
    tm = ROW_TILE
    lat_rows, ctx_rows = n_batch * seq, n_batch * ctx_len
    assert seq % tm == 0 and ctx_rows % tm == 0 and n_batch + 1 <= 8
    assert d // 2 // RET_HEADS == 2 * LANES and d // DIFF_HEADS == 2 * LANES and ctx_len == POOL_TILE
    lat_tiles, all_tiles = lat_rows // tm, (lat_rows + ctx_rows) // tm
    tiles_per_batch = seq // tm
    grp = lambda i: jnp.minimum(i // tiles_per_batch, n_batch)
    assert seq % WIDE_ROW_TILE == 0
    wide_per_batch = seq // WIDE_ROW_TILE
    wide_grp = lambda i: jnp.minimum(i // wide_per_batch, n_batch)
    rope_blk = lambda i: jnp.where(i < n_batch * wide_per_batch, i % wide_per_batch, wide_per_batch)

    cond = jnp.concatenate([c, c_ctx[None, :], jnp.zeros((8 - n_batch - 1, d), F32)], axis=0)
    mods = _modulation(cond, mod_w, mod_b)[:, :n_batch + 1].reshape(depth, n_batch + 1, N_MOD, d)

    wg, wu, wd = ffn_w_gate.astype(BF16), ffn_w_up.astype(BF16), ffn_w_down.astype(BF16)
    ev_in, ev_out, od_in, od_out = ev_w_in.astype(BF16), ev_w_out.astype(BF16), od_w_in.astype(BF16), od_w_out.astype(BF16)
    pw = pool_w.astype(BF16)
    log_decay = jax.nn.log_sigmoid(ret_decay.astype(F32))
    tabs_even = _rope_tables(seq, WIDE_ROW_TILE, True)
    tabs_odd = _rope_tables(seq, WIDE_ROW_TILE, False)

    h = jnp.concatenate([x.reshape(lat_rows, d), ctx.reshape(ctx_rows, d)], axis=0)
    for l in range(depth):
        need_ctx = l < depth - 1
        out_tiles = all_tiles if need_ctx else lat_tiles
        h = _ffn(h, mods, norm_g, wg, wu, wd, l, 0, lat_rows + ctx_rows, wide_grp)
        if l % 2 == 0:
            e = l // 2
            qkv = _proj_in(h, mods, norm_g, ev_in, tabs_even, l, e, True, wide_grp, rope_blk)
            y = _retention(qkv, log_decay[e], n_batch, seq, ctx_len)
            pm = _pool(qkv, pw, pool_scale, e, n_batch, seq, ctx_len)
            h = _out_proj(h, mods, norm_g, y, 0, pm, 0, ev_out, l, e, out_tiles, grp)
        else:
            o = l // 2
            lam_init = 0.8 - 0.6 * math.exp(-0.3 * l)
            qkv = _proj_in(h, mods, norm_g, od_in, tabs_odd, l, o, False, wide_grp, rope_blk)
            lam_vecs = jnp.stack([lam_q1[o], lam_k1[o], lam_q2[o], lam_k2[o]], axis=0)
            gain = jnp.broadcast_to(diff_subln_g[o][:, None], (diff_subln_g.shape[1], LANES))
            y = _diff_attention(qkv, lam_vecs, gain, lam_init, n_batch, seq, ctx_len, need_ctx)
            h = _out_proj(h, mods, norm_g, y, 0, y, 1, od_out, l, o, out_tiles, grp)
        h = _ffn(h, mods, norm_g, wg, wu, wd, l, 1, out_tiles * tm, wide_grp)
    return h[:lat_rows].reshape(n_batch, seq, d)
```

```python
import functools
import math

import jax
import jax.numpy as jnp
from jax import lax
from jax.experimental import pallas as pl
from jax.experimental.pallas import tpu as pltpu

F32 = jnp.float32
BF16 = jnp.bfloat16

GRID_W = 64
ROPE_BASE = 10000.0
N_MOD = 9
RET_HEADS = 4
POOL_WINDOWS = (2, 4, 8, 16)
DIFF_HEADS = 8
NORM_EPS = 1e-6
SUBLN_EPS = 1e-5

LANES = 128
ROW_TILE = 512
WIDE_ROW_TILE = 1024
FFN_CHUNK = 512
NORM_STRIP = 16
PROJ_CHUNK = 1024
MOD_CHUNK = 1024
ATTN_Q_TILE = 1024
ATTN_K_TILE = 512
KEY_TILES_PER_BODY = 8
POOL_TILE = 256
POOL_HALO = 16
VMEM_LIMIT = 56 * 1024 * 1024
FFN_VMEM_LIMIT = 60 * 1024 * 1024


def _params(*sem):
    return pltpu.CompilerParams(dimension_semantics=sem, vmem_limit_bytes=VMEM_LIMIT)


def _silu(x):
    return x * jax.nn.sigmoid(x)


def _rms(x, g, eps=NORM_EPS):
    return x * lax.rsqrt(jnp.mean(x * x, axis=-1, keepdims=True) + eps) * g


def _modulated_norm_strips(h_ref, u_ref, g, shift, scale, zero_ref=None):
    gain = g * (1.0 + scale)

    def strip(i, carry):
        rows = pl.ds(pl.multiple_of(i * NORM_STRIP, NORM_STRIP), NORM_STRIP)
        x = h_ref[rows, :]
        inv = lax.rsqrt(jnp.mean(x * x, axis=-1, keepdims=True) + NORM_EPS)
        u_ref[rows, :] = (x * inv * gain + shift).astype(BF16)
        if zero_ref is not None:
            zero_ref[rows, :] = jnp.zeros((NORM_STRIP, zero_ref.shape[1]), F32)
        return carry

    lax.fori_loop(0, h_ref.shape[0] // NORM_STRIP, strip, 0, unroll=8)


def _dot(a, b):
    return jnp.dot(a, b, preferred_element_type=F32)


def _dot_nt(a, b):
    return lax.dot_general(a, b, (((1,), (1,)), ((), ())), preferred_element_type=F32)


def _dot_tn(a, b):
    return lax.dot_general(a, b, (((0,), (0,)), ((), ())), preferred_element_type=F32)


def _mod_kernel(a_ref, w_ref, b_ref, o_ref):
    s = _silu(a_ref[...]).astype(BF16)
    o_ref[...] = _dot(s, w_ref[...].astype(BF16)) + b_ref[...]


def _modulation(cond, mod_w, mod_b):
    depth, d, n = mod_w.shape
    tn = MOD_CHUNK
    return pl.pallas_call(
        _mod_kernel,
        grid=(depth, n // tn),
        in_specs=[
            pl.BlockSpec((8, d), lambda l, j: (0, 0)),
            pl.BlockSpec((None, d, tn), lambda l, j: (l, 0, j)),
            pl.BlockSpec((None, 1, tn), lambda l, j: (l, 0, j)),
        ],
        out_specs=pl.BlockSpec((None, 8, tn), lambda l, j: (l, 0, j)),
        out_shape=jax.ShapeDtypeStruct((depth, 8, n), F32),
        compiler_params=_params("parallel", "parallel"),
        name="adaln_modulation",
    )(cond, mod_w, mod_b.reshape(depth, 1, n))


def _ffn_kernel(h_ref, m_ref, g_ref, wg_ref, wu_ref, wd_ref, o_ref, u_ref, *, mod0, g0):
    k = pl.program_id(1)
    sub = ROW_TILE
    n_strips = h_ref.shape[0] // NORM_STRIP

    @pl.when(k == 0)
    def _():
        _modulated_norm_strips(h_ref, u_ref, g_ref[g0:g0 + 1, :], m_ref[mod0:mod0 + 1, :], m_ref[mod0 + 1:mod0 + 2, :],
                               zero_ref=o_ref)

    for s in range(h_ref.shape[0] // sub):
        rows = slice(s * sub, (s + 1) * sub)
        u = u_ref[rows, :]
        a = _dot(u, wg_ref[...])
        b = _dot(u, wu_ref[...])
        o_ref[rows, :] += _dot((_silu(a) * b).astype(BF16), wd_ref[...])

    @pl.when(k == pl.num_programs(1) - 1)
    def _():
        gain = (0.5 * m_ref[mod0 + 2:mod0 + 3, :]) * g_ref[g0 + 1:g0 + 2, :]

        for i in range(n_strips):
            rows = slice(i * NORM_STRIP, (i + 1) * NORM_STRIP)
            y = o_ref[rows, :]
            inv = lax.rsqrt(jnp.mean(y * y, axis=-1, keepdims=True) + NORM_EPS)
            o_ref[rows, :] = h_ref[rows, :] + y * inv * gain


def _ffn(h, mods, norm_g, wg, wu, wd, layer, half, out_rows, grp):
    d = h.shape[1]
    f = wg.shape[-1]
    tm, th = WIDE_ROW_TILE, FFN_CHUNK
    return pl.pallas_call(
        functools.partial(_ffn_kernel, mod0=6 * half, g0=4 * half),
        grid=(pl.cdiv(out_rows, tm), f // th),
        in_specs=[
            pl.BlockSpec((tm, d), lambda i, k: (i, 0)),
            pl.BlockSpec((None, None, N_MOD, d), lambda i, k: (layer, grp(i), 0, 0)),
            pl.BlockSpec((None, 6, d), lambda i, k: (layer, 0, 0)),
            pl.BlockSpec((None, None, d, th), lambda i, k: (layer, half, 0, k)),
            pl.BlockSpec((None, None, d, th), lambda i, k: (layer, half, 0, k)),
            pl.BlockSpec((None, None, th, d), lambda i, k: (layer, half, k, 0)),
        ],
        out_specs=pl.BlockSpec((tm, d), lambda i, k: (i, 0)),
        out_shape=jax.ShapeDtypeStruct((out_rows, d), F32),
        scratch_shapes=[pltpu.VMEM((tm, d), BF16)],
        compiler_params=pltpu.CompilerParams(dimension_semantics=("parallel", "arbitrary"),
                                             vmem_limit_bytes=FFN_VMEM_LIMIT),
        name=f"swiglu_l{layer}_{half}",
    )(h, mods, norm_g, wg, wu, wd)


def _proj_kernel(*refs, even, n_rope, scale):
    if even:
        h_ref, m_ref, g_ref, w_ref, c_ref, sa_ref, o_ref, u_ref = refs
    else:
        h_ref, m_ref, g_ref, w_ref, c_ref, sa_ref, sb_ref, o_ref, u_ref = refs
    n = pl.program_id(1)

    @pl.when(n == 0)
    def _():
        _modulated_norm_strips(h_ref, u_ref, g_ref[2:3, :], m_ref[3:4, :], m_ref[4:5, :])

    r = _dot(u_ref[...], w_ref[...])

    is_rope = n < n_rope
    first = n < n_rope // 2
    sc = jnp.where(is_rope, jnp.where(first, 1.0, scale) if even else jnp.where(first, scale, 1.0), 1.0)
    cos = jnp.where(is_rope, c_ref[...], 1.0) * sc
    sin_a = jnp.where(is_rope, sa_ref[...], 0.0) * sc
    if not even:
        sin_b = jnp.where(is_rope, sb_ref[...], 0.0) * sc
    half, quarter = LANES // 2, LANES // 4
    for j in range(r.shape[1] // LANES):
        xb = r[:, j * LANES:(j + 1) * LANES]
        if even:
            tb = (j % 2) * LANES
            ob = xb * cos[:, tb:tb + LANES] + pltpu.roll(xb, half, 1) * sin_a[:, tb:tb + LANES]
        else:
            ob = xb * cos + pltpu.roll(xb, LANES - quarter, 1) * sin_a + pltpu.roll(xb, quarter, 1) * sin_b
        o_ref[:, j * LANES:(j + 1) * LANES] = ob.astype(BF16)


def _proj_in(h, mods, norm_g, w_in, tables, layer, idx, even, grp, rope_blk):
    out_rows, d = h.shape
    n_out = w_in.shape[-1]
    tm, tn = WIDE_ROW_TILE, PROJ_CHUNK
    tw = tables[0].shape[1]
    if even:
        n_rope, scale = (2 * (d // 2)) // tn, float((d // 2 // RET_HEADS) ** -0.5)
    else:
        n_rope, scale = (2 * d) // tn, float((d // DIFF_HEADS // 2) ** -0.5 * math.log2(math.e))
    return pl.pallas_call(
        functools.partial(_proj_kernel, even=even, n_rope=n_rope, scale=scale),
        grid=(pl.cdiv(out_rows, tm), n_out // tn),
        in_specs=[
            pl.BlockSpec((tm, d), lambda i, n: (i, 0)),
            pl.BlockSpec((None, None, N_MOD, d), lambda i, n: (layer, grp(i), 0, 0)),
            pl.BlockSpec((None, 6, d), lambda i, n: (layer, 0, 0)),
            pl.BlockSpec((None, d, tn), lambda i, n: (idx, 0, n)),
        ] + [pl.BlockSpec((tm, tw), lambda i, n: (rope_blk(i), 0)) for _ in tables],
        out_specs=pl.BlockSpec((tm, tn), lambda i, n: (i, n)),
        out_shape=jax.ShapeDtypeStruct((out_rows, n_out), BF16),
        scratch_shapes=[pltpu.VMEM((tm, d), BF16)],
        compiler_params=_params("parallel", "arbitrary"),
        name=f"mixer_in_l{layer}",
    )(h, mods, norm_g, w_in, *tables)


def _rope_tables(seq, n_ident, even):
    n_rows = seq // GRID_W
    nf = (2 * LANES if even else LANES) // 4
    inv = ROPE_BASE ** (-jnp.arange(nf, dtype=F32) / nf)
    ar = jnp.arange(n_rows, dtype=F32)[:, None] * inv[None, :]
    ac = jnp.arange(GRID_W, dtype=F32)[:, None] * inv[None, :]
    cr, sr = jnp.repeat(jnp.cos(ar), GRID_W, axis=0), jnp.repeat(jnp.sin(ar), GRID_W, axis=0)
    cc, sc = jnp.tile(jnp.cos(ac), (n_rows, 1)), jnp.tile(jnp.sin(ac), (n_rows, 1))
    z = jnp.zeros_like(sr)
    if even:
        tabs = [jnp.concatenate([cr, cr, cc, cc], 1), jnp.concatenate([-sr, sr, -sc, sc], 1)]
    else:
        tabs = [jnp.concatenate([cr, cr, cc, cc], 1), jnp.concatenate([-sr, z, -sc, z], 1),
                jnp.concatenate([z, sr, z, sc], 1)]
    ident = [jnp.ones((n_ident, tabs[0].shape[1]), F32)] + [jnp.zeros((n_ident, tabs[0].shape[1]), F32)] * (len(tabs) - 1)
    return [jnp.concatenate([a, b], 0) for a, b in zip(tabs, ident)]


def _attn_kernel(*refs, tk, n_kt, lam_init):
    if n_kt:
        lam_ref, sg_ref, q_ref, kc_ref, vc_ref, k_ref, vt_ref, o_ref, m_ref, l_ref, acc_ref, s_ref, mx_ref = refs
    else:
        lam_ref, sg_ref, q_ref, kc_ref, vc_ref, _, o_ref, m_ref, l_ref, acc_ref = refs
    if n_kt:
        dh, tq = q_ref.shape[0] // 2, q_ref.shape[1]

        def scores_t(kb, mi):
            return _dot(kb[:, mi * dh:(mi + 1) * dh], q_ref[mi * dh:(mi + 1) * dh, :])
    else:
        dh, tq = q_ref.shape[1] // 2, q_ref.shape[0]

        def scores_t(kb, mi):
            return _dot_nt(kb[:, mi * dh:(mi + 1) * dh], q_ref[:, mi * dh:(mi + 1) * dh])

    def absorb(slot, vt, mi):
        m_prev = m_ref[mi]
        m_next = jnp.maximum(m_prev, mx_ref[slot, mi])
        alpha = jnp.exp2(m_prev - m_next)
        p_t = jnp.exp2(s_ref[slot, mi] - m_next)
        l_ref[mi] = alpha * l_ref[mi] + jnp.sum(p_t, axis=0, keepdims=True)
        acc_ref[mi] = alpha * acc_ref[mi] + _dot(vt, p_t.astype(BF16))
        m_ref[mi] = m_next

    if n_kt:
        def qk(j, slot):
            kb = k_ref[pl.ds(pl.multiple_of(j * tk, tk), tk), :]
            for mi in range(2):
                s_t = scores_t(kb, mi)
                s_ref[slot, mi] = s_t
                mx_ref[slot, mi] = jnp.max(s_t, axis=0, keepdims=True)

        def pv(j, slot):
            vt = vt_ref[j]
            for mi in range(2):
                absorb(slot, vt, mi)

        qk(0, 0)

    kc, vc = kc_ref[...], vc_ref[...]
    for mi in range(2):
        s_t = scores_t(kc, mi)
        m0 = jnp.max(s_t, axis=0, keepdims=True)
        p_t = jnp.exp2(s_t - m0)
        m_ref[mi] = m0
        l_ref[mi] = jnp.sum(p_t, axis=0, keepdims=True)
        acc_ref[mi] = _dot_tn(vc, p_t.astype(BF16))

    if n_kt:
        def body(i, carry):
            j = KEY_TILES_PER_BODY * i
            for u in range(KEY_TILES_PER_BODY):
                qk(j + u + 1, (u + 1) % 2)
                pv(j + u, u % 2)
            return carry

        lax.fori_loop(0, n_kt // KEY_TILES_PER_BODY - 1, body, 0)
        for j in range(n_kt - KEY_TILES_PER_BODY, n_kt):
            if j + 1 < n_kt:
                qk(j + 1, (j + 1) % 2)
            pv(j, j % 2)

    t1 = jnp.sum(lam_ref[0:1, :] * lam_ref[1:2, :], axis=-1, keepdims=True)
    t2 = jnp.sum(lam_ref[2:3, :] * lam_ref[3:4, :], axis=-1, keepdims=True)
    lam = jnp.exp(t1) - jnp.exp(t2) + lam_init
    o_t = acc_ref[0] * (1.0 / l_ref[0]) - acc_ref[1] * (lam / l_ref[1])
    gain = jnp.concatenate([sg_ref[...]] * (tq // LANES), axis=1) * (1.0 - lam_init)
    o_t = o_t * lax.rsqrt(jnp.mean(o_t * o_t, axis=0, keepdims=True) + SUBLN_EPS) * gain
    o_ref[...] = o_t.T.astype(BF16)


def _transpose_kernel(x_ref, o_ref):
    o_ref[...] = x_ref[...].astype(F32).T.astype(BF16)


def _diff_attention(qkv, lam_vecs, subln_g, lam_init, n_batch, seq, ctx_len, need_ctx):
    rows, width = qkv.shape
    d = width // 3
    hd = d // DIFF_HEADS
    tq, tk = ATTN_Q_TILE, ATTN_K_TILE
    nq = seq // tq
    assert seq % (KEY_TILES_PER_BODY * tk) == 0 and KEY_TILES_PER_BODY % 2 == 0
    ctx_blk0 = (n_batch * seq) // ctx_len
    nkt = seq // tk
    scratch = lambda t: [pltpu.VMEM((2, 1, t), F32), pltpu.VMEM((2, 1, t), F32), pltpu.VMEM((2, hd, t), F32)]
    small = [pl.BlockSpec((4, hd // 2), lambda b, h, qi: (0, 0)),
             pl.BlockSpec((hd, LANES), lambda b, h, qi: (0, 0))]
    ctx_kv = [pl.BlockSpec((ctx_len, hd), lambda b, h, qi: (ctx_blk0 + b, DIFF_HEADS + h)),
              pl.BlockSpec((ctx_len, hd), lambda b, h, qi: (ctx_blk0 + b, 2 * DIFF_HEADS + h))]
    def transposed(tile, col_blk, name):
        n = seq // tile
        return pl.pallas_call(
            _transpose_kernel,
            grid=(n_batch, n),
            in_specs=[pl.BlockSpec((tile, d), lambda b, j: (b * n + j, col_blk))],
            out_specs=pl.BlockSpec((None, None, d, tile), lambda b, j: (b, j, 0, 0)),
            out_shape=jax.ShapeDtypeStruct((n_batch, n, d, tile), BF16),
            compiler_params=_params("parallel", "parallel"),
            name=name,
        )(qkv)

    v_t = transposed(tk, 2, "value_transpose")
    q_t = transposed(tq, 0, "query_transpose")
    y = pl.pallas_call(
        functools.partial(_attn_kernel, tk=tk, n_kt=nkt, lam_init=lam_init),
        grid=(n_batch, DIFF_HEADS, nq),
        in_specs=small + [pl.BlockSpec((None, None, hd, tq), lambda b, h, qi: (b, qi, h, 0))] + ctx_kv + [
            pl.BlockSpec((seq, hd), lambda b, h, qi: (b, DIFF_HEADS + h)),
            pl.BlockSpec((None, nkt, hd, tk), lambda b, h, qi: (b, 0, h, 0)),
        ],
        out_specs=pl.BlockSpec((tq, hd), lambda b, h, qi: (b * nq + qi, h)),
        out_shape=jax.ShapeDtypeStruct((rows, d), BF16),
        scratch_shapes=scratch(tq) + [pltpu.VMEM((2, 2, tk, tq), F32), pltpu.VMEM((2, 2, 1, tq), F32)],
        compiler_params=_params("parallel", "parallel", "arbitrary"),
        name="diff_attention_latent",
    )(lam_vecs, subln_g, q_t, qkv, qkv, qkv, v_t)
    if not need_ctx:
        return y
    return pl.pallas_call(
        functools.partial(_attn_kernel, tk=tk, n_kt=0, lam_init=lam_init),
        grid=(n_batch, DIFF_HEADS, 1),
        in_specs=small + [pl.BlockSpec((ctx_len, hd), lambda b, h, qi: (ctx_blk0 + b, h))] + ctx_kv + [
            pl.BlockSpec(memory_space=pl.ANY)],
        out_specs=pl.BlockSpec((ctx_len, hd), lambda b, h, qi: (ctx_blk0 + b, h)),
        out_shape=jax.ShapeDtypeStruct((rows, d), BF16),
        scratch_shapes=scratch(ctx_len),
        input_output_aliases={5: 0},
        compiler_params=_params("parallel", "parallel", "arbitrary"),
        name="diff_attention_context",
    )(lam_vecs, subln_g, qkv, qkv, qkv, y)


def _ret_fwd_kernel(lg_ref, q_ref, k_ref, v_ref, o_ref, d_ref, s_ref):
    n = pl.program_id(1)
    c = q_ref.shape[0]
    hd = q_ref.shape[1] // RET_HEADS

    @pl.when(n == 0)
    def _():
        s_ref[...] = jnp.zeros(s_ref.shape, F32)
        i = lax.broadcasted_iota(jnp.int32, (c, c), 0)
        j = lax.broadcasted_iota(jnp.int32, (c, c), 1)
        dist = (i - j).astype(F32)
        for h in range(RET_HEADS):
            fwd = jnp.exp(lg_ref[0, h] * jnp.maximum(dist, 0.0))
            bwd = jnp.exp(lg_ref[1, h] * jnp.maximum(-dist, 0.0))
            d_ref[h] = jnp.where(dist >= 0, fwd, bwd)

    pos = lax.broadcasted_iota(jnp.int32, (c, 1), 0).astype(F32)
    for h in range(RET_HEADS):
        lg = lg_ref[0, h]
        sl = slice(h * hd, (h + 1) * hd)
        qh, kh, vh = q_ref[:, sl], k_ref[:, sl], v_ref[:, sl]
        p = (_dot_nt(qh, kh) * d_ref[h]).astype(BF16)
        qd = (qh.astype(F32) * jnp.exp(lg * (pos + 1.0))).astype(BF16)
        s = s_ref[h]
        o_ref[:, sl] = _dot(p, vh) + _dot(qd, s.astype(BF16))
        kd = (kh.astype(F32) * jnp.exp(lg * (c - 1.0 - pos))).astype(BF16)
        s_ref[h] = jnp.exp(jnp.full((1, 1), lg * c, F32)) * s + _dot_tn(kd, vh)


def _ret_bwd_kernel(lg_ref, q_ref, k_ref, v_ref, g_ref, r_ref, o_ref, s_ref):
    n = pl.program_id(1)
    c = q_ref.shape[0]
    hd = q_ref.shape[1] // RET_HEADS

    @pl.when(n == 0)
    def _():
        s_ref[...] = jnp.zeros(s_ref.shape, F32)

    pos = lax.broadcasted_iota(jnp.int32, (c, 1), 0).astype(F32)
    for h in range(RET_HEADS):
        lg = lg_ref[1, h]
        sl = slice(h * hd, (h + 1) * hd)
        qh, kh, vh = q_ref[:, sl], k_ref[:, sl], v_ref[:, sl]
        qd = (qh.astype(F32) * jnp.exp(lg * (c - pos))).astype(BF16)
        s = s_ref[h]
        r = r_ref[:, sl] + _dot(qd, s.astype(BF16))
        y = r * lax.rsqrt(jnp.mean(r * r, axis=-1, keepdims=True) + NORM_EPS)
        o_ref[:, sl] = (y * _silu(g_ref[:, sl].astype(F32))).astype(BF16)
        kd = (kh.astype(F32) * jnp.exp(lg * pos)).astype(BF16)
        s_ref[h] = jnp.exp(jnp.full((1, 1), lg * c, F32)) * s + _dot_tn(kd, vh)


def _retention(qkv, log_decay, n_batch, seq, ctx_len):
    rows = qkv.shape[0]
    c = ctx_len
    w = (qkv.shape[1] // 5)
    nc = seq // c
    ctx_blk0 = (n_batch * seq) // c
    fwd_blk = lambda b, n: jnp.where(n == 0, ctx_blk0 + b, b * nc + n - 1)
    bwd_blk = lambda b, n: jnp.where(n == 0, ctx_blk0 + b, b * nc + nc - n)
    smem = pl.BlockSpec(memory_space=pltpu.SMEM)
    state = pltpu.VMEM((RET_HEADS, w // RET_HEADS, w // RET_HEADS), F32)
    r1 = pl.pallas_call(
        _ret_fwd_kernel,
        grid=(n_batch, nc + 1),
        in_specs=[smem] + [pl.BlockSpec((c, w), functools.partial(lambda b, n, j: (fwd_blk(b, n), j), j=j)) for j in range(3)],
        out_specs=pl.BlockSpec((c, w), lambda b, n: (fwd_blk(b, n), 0)),
        out_shape=jax.ShapeDtypeStruct((rows, w), F32),
        scratch_shapes=[pltpu.VMEM((RET_HEADS, c, c), F32), state],
        compiler_params=_params("parallel", "arbitrary"),
        name="retention_forward_sweep",
    )(log_decay, qkv, qkv, qkv)
    return pl.pallas_call(
        _ret_bwd_kernel,
        grid=(n_batch, nc + 1),
        in_specs=[smem] + [pl.BlockSpec((c, w), functools.partial(lambda b, n, j: (bwd_blk(b, n), j), j=j)) for j in range(4)]
        + [pl.BlockSpec((c, w), lambda b, n: (bwd_blk(b, n), 0))],
        out_specs=pl.BlockSpec((c, w), lambda b, n: (bwd_blk(b, n), 0)),
        out_shape=jax.ShapeDtypeStruct((rows, w), BF16),
        scratch_shapes=[state],
        compiler_params=_params("parallel", "arbitrary"),
        name="retention_backward_sweep",
    )(log_decay, qkv, qkv, qkv, qkv, r1)


def _pool_kernel(prev_ref, main_ref, next_ref, w_ref, sc_ref, o_ref, *, lat_tiles, tiles_per_seq, seq, ctx_len):
    i = pl.program_id(0)
    tp = main_ref.shape[0]
    gw = main_ref.shape[1] // len(POOL_WINDOWS)
    is_lat = i < lat_tiles
    t0 = jnp.where(is_lat, (i % tiles_per_seq) * tp, 0)
    length = jnp.where(is_lat, seq, ctx_len)
    n_ext = tp + 2 * POOL_HALO
    t = t0 - POOL_HALO + lax.broadcasted_iota(jnp.int32, (n_ext, 1), 0)
    valid = (t >= 0) & (t < length)
    tm = t[POOL_HALO:POOL_HALO + tp]
    for gi, win in enumerate(POOL_WINDOWS):
        sl = slice(gi * gw, (gi + 1) * gw)
        ext = jnp.concatenate([prev_ref[:, sl], main_ref[:, sl], next_ref[:, sl]], axis=0).astype(F32)
        x = jnp.where(valid, ext, 0.0)
        acc = x + pltpu.roll(x, 1, 0)
        half = 1
        while 2 * half < win:
            acc = pltpu.roll(acc, n_ext - half, 0) + pltpu.roll(acc, half, 0)
            half *= 2
        cnt = (jnp.minimum(tm + win // 2, length) - jnp.maximum(tm - win // 2, 0)).astype(F32)
        pooled = acc[POOL_HALO:POOL_HALO + tp] / cnt - x[POOL_HALO:POOL_HALO + tp]
        o_ref[:, sl] = (_dot(pooled.astype(BF16), w_ref[gi]) * sc_ref[:, sl]).astype(BF16)


def _pool(qkv, pool_w, pool_scale, idx, n_batch, seq, ctx_len):
    rows = qkv.shape[0]
    w = qkv.shape[1] // 5
    tp, halo = POOL_TILE, POOL_HALO
    assert ctx_len == tp and seq % tp == 0
    last_halo_blk = rows // halo - 1
    per = tp // halo
    return pl.pallas_call(
        functools.partial(_pool_kernel, lat_tiles=(n_batch * seq) // tp, tiles_per_seq=seq // tp, seq=seq, ctx_len=ctx_len),
        grid=(rows // tp,),
        in_specs=[
            pl.BlockSpec((halo, w), lambda i: (jnp.maximum(i * per - 1, 0), 4)),
            pl.BlockSpec((tp, w), lambda i: (i, 4)),
            pl.BlockSpec((halo, w), lambda i: (jnp.minimum((i + 1) * per, last_halo_blk), 4)),
            pl.BlockSpec((None,) + pool_w.shape[1:], lambda i: (idx, 0, 0, 0)),
            pl.BlockSpec((None, 1, w), lambda i: (idx, 0, 0)),
        ],
        out_specs=pl.BlockSpec((tp, w), lambda i: (i, 0)),
        out_shape=jax.ShapeDtypeStruct((rows, w), BF16),
        compiler_params=_params("parallel"),
        name="multiscale_pool",
    )(qkv, qkv, qkv, pool_w, pool_scale.reshape(pool_scale.shape[0], 1, w))


def _out_kernel(h_ref, m_ref, g_ref, ya_ref, yb_ref, wa_ref, wb_ref, o_ref):
    y = _dot(ya_ref[...], wa_ref[...]) + _dot(yb_ref[...], wb_ref[...])
    o_ref[...] = h_ref[...] + m_ref[5:6, :] * _rms(y, g_ref[3:4, :])


def _out_proj(h, mods, norm_g, ya, ya_blk, yb, yb_blk, w_out, layer, idx, n_tiles, grp):
    d = h.shape[1]
    tm = ROW_TILE
    return pl.pallas_call(
        _out_kernel,
        grid=(n_tiles,),
        in_specs=[
            pl.BlockSpec((tm, d), lambda i: (i, 0)),
            pl.BlockSpec((None, None, N_MOD, d), lambda i: (layer, grp(i), 0, 0)),
            pl.BlockSpec((None, 6, d), lambda i: (layer, 0, 0)),
            pl.BlockSpec((tm, d // 2), lambda i: (i, ya_blk)),
            pl.BlockSpec((tm, d // 2), lambda i: (i, yb_blk)),
            pl.BlockSpec((None, d // 2, d), lambda i: (idx, 0, 0)),
            pl.BlockSpec((None, d // 2, d), lambda i: (idx, 1, 0)),
        ],
        out_specs=pl.BlockSpec((tm, d), lambda i: (i, 0)),
        out_shape=jax.ShapeDtypeStruct((n_tiles * tm, d), F32),
        compiler_params=_params("parallel"),
        name=f"mixer_out_l{layer}",
    )(h, mods, norm_g, ya, yb, w_out, w_out)


def kernel(x, c, ctx, c_ctx, mod_w, mod_b, norm_g, ffn_w_gate, ffn_w_up, ffn_w_down, ev_w_in, ev_w_out, ret_decay, pool_w, pool_scale, od_w_in, od_w_out, lam_q1, lam_k1, lam_q2, lam_k2, diff_subln_g):
    n_batch, seq, d = x.shape
    ctx_len = ctx.shape[1]
    depth = mod_w.shape[0]
    tm = ROW_TILE
    lat_rows, ctx_rows = n_batch * seq, n_batch * ctx_len
    assert seq % tm == 0 and ctx_rows % tm == 0 and n_batch + 1 <= 8
    assert d // 2 // RET_HEADS == 2 * LANES and d // DIFF_HEADS == 2 * LANES and ctx_len == POOL_TILE
    lat_tiles, all_tiles = lat_rows // tm, (lat_rows + ctx_rows) // tm
    tiles_per_batch = seq // tm
    grp = lambda i: jnp.minimum(i // tiles_per_batch, n_batch)
    assert seq % WIDE_ROW_TILE == 0
    wide_per_batch = seq // WIDE_ROW_TILE
    wide_grp = lambda i: jnp.minimum(i // wide_per_batch, n_batch)
    rope_blk = lambda i: jnp.where(i < n_batch * wide_per_batch, i % wide_per_batch, wide_per_batch)

    cond = jnp.concatenate([c, c_ctx[None, :], jnp.zeros((8 - n_batch - 1, d), F32)], axis=0)
    mods = _modulation(cond, mod_w, mod_b)[:, :n_batch + 1].reshape(depth, n_batch + 1, N_MOD, d)

    wg, wu, wd = ffn_w_gate.astype(BF16), ffn_w_up.astype(BF16), ffn_w_down.astype(BF16)
    ev_in, ev_out, od_in, od_out = ev_w_in.astype(BF16), ev_w_out.astype(BF16), od_w_in.astype(BF16), od_w_out.astype(BF16)
    pw = pool_w.astype(BF16)
    log_decay = jax.nn.log_sigmoid(ret_decay.astype(F32))
    tabs_even = _rope_tables(seq, WIDE_ROW_TILE, True)
    tabs_odd = _rope_tables(seq, WIDE_ROW_TILE, False)

    h = jnp.concatenate([x.reshape(lat_rows, d), ctx.reshape(ctx_rows, d)], axis=0)
    for l in range(depth):
        need_ctx = l < depth - 1
        out_tiles = all_tiles if need_ctx else lat_tiles
        h = _ffn(h, mods, norm_g, wg, wu, wd, l, 0, lat_rows + ctx_rows, wide_grp)
        if l % 2 == 0:
            e = l // 2
            qkv = _proj_in(h, mods, norm_g, ev_in, tabs_even, l, e, True, wide_grp, rope_blk)
            y = _retention(qkv, log_decay[e], n_batch, seq, ctx_len)
            pm = _pool(qkv, pw, pool_scale, e, n_batch, seq, ctx_len)
            h = _out_proj(h, mods, norm_g, y, 0, pm, 0, ev_out, l, e, out_tiles, grp)
        else:
            o = l // 2
            lam_init = 0.8 - 0.6 * math.exp(-0.3 * l)
            qkv = _proj_in(h, mods, norm_g, od_in, tabs_odd, l, o, False, wide_grp, rope_blk)
            lam_vecs = jnp.stack([lam_q1[o], lam_k1[o], lam_q2[o], lam_k2[o]], axis=0)
            gain = jnp.broadcast_to(diff_subln_g[o][:, None], (diff_subln_g.shape[1], LANES))
            y = _diff_attention(qkv, lam_vecs, gain, lam_init, n_batch, seq, ctx_len, need_ctx)
            h = _out_proj(h, mods, norm_g, y, 0, y, 1, od_out, l, o, out_tiles, grp)
        h = _ffn(h, mods, norm_g, wg, wu, wd, l, 1, out_tiles * tm, wide_grp)
    return h[:lat_rows].reshape(n_batch, seq, d)
```

```python
import functools
import math

import jax
import jax.numpy as jnp
from jax import lax
from jax.experimental import pallas as pl
from jax.experimental.pallas import tpu as pltpu

F32 = jnp.float32
BF16 = jnp.bfloat16

GRID_W = 64
ROPE_BASE = 10000.0
N_MOD = 9
RET_HEADS = 4
POOL_WINDOWS = (2, 4, 8, 16)
DIFF_HEADS = 8
NORM_EPS = 1e-6
SUBLN_EPS = 1e-5

LANES = 128
ROW_TILE = 512
WIDE_ROW_TILE = 1024
FFN_CHUNK = 512
NORM_STRIP = 16
PROJ_CHUNK = 1024
MOD_CHUNK = 1024
ATTN_Q_TILE = 1024
ATTN_K_TILE = 512
KEY_TILES_PER_BODY = 8
POOL_TILE = 256
POOL_HALO = 16
VMEM_LIMIT = 56 * 1024 * 1024
FFN_VMEM_LIMIT = 60 * 1024 * 1024


def _params(*sem):
    return pltpu.CompilerParams(dimension_semantics=sem, vmem_limit_bytes=VMEM_LIMIT)


def _silu(x):
    return x * jax.nn.sigmoid(x)


def _rms(x, g, eps=NORM_EPS):
    return x * lax.rsqrt(jnp.mean(x * x, axis=-1, keepdims=True) + eps) * g


def _modulated_norm_strips(h_ref, u_ref, g, shift, scale, zero_ref=None):
    gain = g * (1.0 + scale)

    def strip(i, carry):
        rows = pl.ds(pl.multiple_of(i * NORM_STRIP, NORM_STRIP), NORM_STRIP)
        x = h_ref[rows, :]
        inv = lax.rsqrt(jnp.mean(x * x, axis=-1, keepdims=True) + NORM_EPS)
        u_ref[rows, :] = (x * inv * gain + shift).astype(BF16)
        if zero_ref is not None:
            zero_ref[rows, :] = jnp.zeros((NORM_STRIP, zero_ref.shape[1]), F32)
        return carry

    lax.fori_loop(0, h_ref.shape[0] // NORM_STRIP, strip, 0, unroll=8)


def _dot(a, b):
    return jnp.dot(a, b, preferred_element_type=F32)


def _dot_nt(a, b):
    return lax.dot_general(a, b, (((1,), (1,)), ((), ())), preferred_element_type=F32)


def _dot_tn(a, b):
    return lax.dot_general(a, b, (((0,), (0,)), ((), ())), preferred_element_type=F32)


def _mod_kernel(a_ref, w_ref, b_ref, o_ref):
    s = _silu(a_ref[...]).astype(BF16)
    o_ref[...] = _dot(s, w_ref[...].astype(BF16)) + b_ref[...]


def _modulation(cond, mod_w, mod_b):
    depth, d, n = mod_w.shape
    tn = MOD_CHUNK
    return pl.pallas_call(
        _mod_kernel,
        grid=(depth, n // tn),
        in_specs=[
            pl.BlockSpec((8, d), lambda l, j: (0, 0)),
            pl.BlockSpec((None, d, tn), lambda l, j: (l, 0, j)),
            pl.BlockSpec((None, 1, tn), lambda l, j: (l, 0, j)),
        ],
        out_specs=pl.BlockSpec((None, 8, tn), lambda l, j: (l, 0, j)),
        out_shape=jax.ShapeDtypeStruct((depth, 8, n), F32),
        compiler_params=_params("parallel", "parallel"),
        name="adaln_modulation",
    )(cond, mod_w, mod_b.reshape(depth, 1, n))


def _ffn_kernel(h_ref, m_ref, g_ref, wg_ref, wu_ref, wd_ref, o_ref, u_ref, *, mod0, g0):
    k = pl.program_id(1)
    sub = ROW_TILE
    n_strips = h_ref.shape[0] // NORM_STRIP

    @pl.when(k == 0)
    def _():
        _modulated_norm_strips(h_ref, u_ref, g_ref[g0:g0 + 1, :], m_ref[mod0:mod0 + 1, :], m_ref[mod0 + 1:mod0 + 2, :],
                               zero_ref=o_ref)

    for s in range(h_ref.shape[0] // sub):
        rows = slice(s * sub, (s + 1) * sub)
        u = u_ref[rows, :]
        a = _dot(u, wg_ref[...])
        b = _dot(u, wu_ref[...])
        o_ref[rows, :] += _dot((_silu(a) * b).astype(BF16), wd_ref[...])

    @pl.when(k == pl.num_programs(1) - 1)
    def _():
        gain = (0.5 * m_ref[mod0 + 2:mod0 + 3, :]) * g_ref[g0 + 1:g0 + 2, :]

        for i in range(n_strips):
            rows = slice(i * NORM_STRIP, (i + 1) * NORM_STRIP)
            y = o_ref[rows, :]
            inv = lax.rsqrt(jnp.mean(y * y, axis=-1, keepdims=True) + NORM_EPS)
            o_ref[rows, :] = h_ref[rows, :] + y * inv * gain


def _ffn(h, mods, norm_g, wg, wu, wd, layer, half, out_rows, grp):
    d = h.shape[1]
    f = wg.shape[-1]
    tm, th = WIDE_ROW_TILE, FFN_CHUNK
    return pl.pallas_call(
        functools.partial(_ffn_kernel, mod0=6 * half, g0=4 * half),
        grid=(pl.cdiv(out_rows, tm), f // th),
        in_specs=[
            pl.BlockSpec((tm, d), lambda i, k: (i, 0)),
            pl.BlockSpec((None, None, N_MOD, d), lambda i, k: (layer, grp(i), 0, 0)),
            pl.BlockSpec((None, 6, d), lambda i, k: (layer, 0, 0)),
            pl.BlockSpec((None, None, d, th), lambda i, k: (layer, half, 0, k)),
            pl.BlockSpec((None, None, d, th), lambda i, k: (layer, half, 0, k)),
            pl.BlockSpec((None, None, th, d), lambda i, k: (layer, half, k, 0)),
        ],
        out_specs=pl.BlockSpec((tm, d), lambda i, k: (i, 0)),
        out_shape=jax.ShapeDtypeStruct((out_rows, d), F32),
        scratch_shapes=[pltpu.VMEM((tm, d), BF16)],
        compiler_params=pltpu.CompilerParams(dimension_semantics=("parallel", "arbitrary"),
                                             vmem_limit_bytes=FFN_VMEM_LIMIT),
        name=f"swiglu_l{layer}_{half}",
    )(h, mods, norm_g, wg, wu, wd)


def _proj_kernel(*refs, even, n_rope, scale):
    if even:
        h_ref, m_ref, g_ref, w_ref, c_ref, sa_ref, o_ref, u_ref = refs
    else:
        h_ref, m_ref, g_ref, w_ref, c_ref, sa_ref, sb_ref, o_ref, u_ref = refs
    n = pl.program_id(1)

    @pl.when(n == 0)
    def _():
        _modulated_norm_strips(h_ref, u_ref, g_ref[2:3, :], m_ref[3:4, :], m_ref[4:5, :])

    is_rope = n < n_rope
    first = n < n_rope // 2
    sc = jnp.where(is_rope, jnp.where(first, 1.0, scale) if even else jnp.where(first, scale, 1.0), 1.0)
    half, quarter = LANES // 2, LANES // 4
    for s in range(h_ref.shape[0] // ROW_TILE):
        rows = slice(s * ROW_TILE, (s + 1) * ROW_TILE)
        r = _dot(u_ref[rows, :], w_ref[...])
        cos = jnp.where(is_rope, c_ref[rows, :], 1.0) * sc
        sin_a = jnp.where(is_rope, sa_ref[rows, :], 0.0) * sc
        if not even:
            sin_b = jnp.where(is_rope, sb_ref[rows, :], 0.0) * sc
        for j in range(r.shape[1] // LANES):
            xb = r[:, j * LANES:(j + 1) * LANES]
            if even:
                tb = (j % 2) * LANES
                ob = xb * cos[:, tb:tb + LANES] + pltpu.roll(xb, half, 1) * sin_a[:, tb:tb + LANES]
            else:
                ob = xb * cos + pltpu.roll(xb, LANES - quarter, 1) * sin_a + pltpu.roll(xb, quarter, 1) * sin_b
            o_ref[rows, j * LANES:(j + 1) * LANES] = ob.astype(BF16)


def _proj_in(h, mods, norm_g, w_in, tables, layer, idx, even, grp, rope_blk):
    out_rows, d = h.shape
    n_out = w_in.shape[-1]
    tm, tn = WIDE_ROW_TILE, PROJ_CHUNK
    tw = tables[0].shape[1]
    if even:
        n_rope, scale = (2 * (d // 2)) // tn, float((d // 2 // RET_HEADS) ** -0.5)
    else:
        n_rope, scale = (2 * d) // tn, float((d // DIFF_HEADS // 2) ** -0.5 * math.log2(math.e))
    return pl.pallas_call(
        functools.partial(_proj_kernel, even=even, n_rope=n_rope, scale=scale),
        grid=(pl.cdiv(out_rows, tm), n_out // tn),
        in_specs=[
            pl.BlockSpec((tm, d), lambda i, n: (i, 0)),
            pl.BlockSpec((None, None, N_MOD, d), lambda i, n: (layer, grp(i), 0, 0)),
            pl.BlockSpec((None, 6, d), lambda i, n: (layer, 0, 0)),
            pl.BlockSpec((None, d, tn), lambda i, n: (idx, 0, n)),
        ] + [pl.BlockSpec((tm, tw), lambda i, n: (rope_blk(i), 0)) for _ in tables],
        out_specs=pl.BlockSpec((tm, tn), lambda i, n: (i, n)),
        out_shape=jax.ShapeDtypeStruct((out_rows, n_out), BF16),
        scratch_shapes=[pltpu.VMEM((tm, d), BF16)],
        compiler_params=_params("parallel", "arbitrary"),
        name=f"mixer_in_l{layer}",
    )(h, mods, norm_g, w_in, *tables)


def _rope_tables(seq, n_ident, even):
    n_rows = seq // GRID_W
    nf = (2 * LANES if even else LANES) // 4
    inv = ROPE_BASE ** (-jnp.arange(nf, dtype=F32) / nf)
    ar = jnp.arange(n_rows, dtype=F32)[:, None] * inv[None, :]
    ac = jnp.arange(GRID_W, dtype=F32)[:, None] * inv[None, :]
    cr, sr = jnp.repeat(jnp.cos(ar), GRID_W, axis=0), jnp.repeat(jnp.sin(ar), GRID_W, axis=0)
    cc, sc = jnp.tile(jnp.cos(ac), (n_rows, 1)), jnp.tile(jnp.sin(ac), (n_rows, 1))
    z = jnp.zeros_like(sr)
    if even:
        tabs = [jnp.concatenate([cr, cr, cc, cc], 1), jnp.concatenate([-sr, sr, -sc, sc], 1)]
    else:
        tabs = [jnp.concatenate([cr, cr, cc, cc], 1), jnp.concatenate([-sr, z, -sc, z], 1),
                jnp.concatenate([z, sr, z, sc], 1)]
    ident = [jnp.ones((n_ident, tabs[0].shape[1]), F32)] + [jnp.zeros((n_ident, tabs[0].shape[1]), F32)] * (len(tabs) - 1)
    return [jnp.concatenate([a, b], 0) for a, b in zip(tabs, ident)]


def _attn_kernel(*refs, tk, n_kt, lam_init):
    if n_kt:
        lam_ref, sg_ref, q_ref, kc_ref, vc_ref, k_ref, vt_ref, o_ref, m_ref, l_ref, acc_ref, s_ref, mx_ref = refs
    else:
        lam_ref, sg_ref, q_ref, kc_ref, vc_ref, _, o_ref, m_ref, l_ref, acc_ref = refs
    if n_kt:
        dh, tq = q_ref.shape[0] // 2, q_ref.shape[1]

        def scores_t(kb, mi):
            return _dot(kb[:, mi * dh:(mi + 1) * dh], q_ref[mi * dh:(mi + 1) * dh, :])
    else:
        dh, tq = q_ref.shape[1] // 2, q_ref.shape[0]

        def scores_t(kb, mi):
            return _dot_nt(kb[:, mi * dh:(mi + 1) * dh], q_ref[:, mi * dh:(mi + 1) * dh])

    def absorb(slot, vt, mi):
        m_prev = m_ref[mi]
        m_next = jnp.maximum(m_prev, mx_ref[slot, mi])
        alpha = jnp.exp2(m_prev - m_next)
        p_t = jnp.exp2(s_ref[slot, mi] - m_next)
        l_ref[mi] = alpha * l_ref[mi] + jnp.sum(p_t, axis=0, keepdims=True)
        acc_ref[mi] = alpha * acc_ref[mi] + _dot(vt, p_t.astype(BF16))
        m_ref[mi] = m_next

    if n_kt:
        def qk(j, slot):
            kb = k_ref[pl.ds(pl.multiple_of(j * tk, tk), tk), :]
            for mi in range(2):
                s_t = scores_t(kb, mi)
                s_ref[slot, mi] = s_t
                mx_ref[slot, mi] = jnp.max(s_t, axis=0, keepdims=True)

        def pv(j, slot):
            vt = vt_ref[j]
            for mi in range(2):
                absorb(slot, vt, mi)

        qk(0, 0)

    kc, vc = kc_ref[...], vc_ref[...]
    for mi in range(2):
        s_t = scores_t(kc, mi)
        m0 = jnp.max(s_t, axis=0, keepdims=True)
        p_t = jnp.exp2(s_t - m0)
        m_ref[mi] = m0
        l_ref[mi] = jnp.sum(p_t, axis=0, keepdims=True)
        acc_ref[mi] = _dot_tn(vc, p_t.astype(BF16))

    if n_kt:
        def body(i, carry):
            j = KEY_TILES_PER_BODY * i
            for u in range(KEY_TILES_PER_BODY):
                qk(j + u + 1, (u + 1) % 2)
                pv(j + u, u % 2)
            return carry

        lax.fori_loop(0, n_kt // KEY_TILES_PER_BODY - 1, body, 0)
        for j in range(n_kt - KEY_TILES_PER_BODY, n_kt):
            if j + 1 < n_kt:
                qk(j + 1, (j + 1) % 2)
            pv(j, j % 2)

    t1 = jnp.sum(lam_ref[0:1, :] * lam_ref[1:2, :], axis=-1, keepdims=True)
    t2 = jnp.sum(lam_ref[2:3, :] * lam_ref[3:4, :], axis=-1, keepdims=True)
    lam = jnp.exp(t1) - jnp.exp(t2) + lam_init
    o_t = acc_ref[0] * (1.0 / l_ref[0]) - acc_ref[1] * (lam / l_ref[1])
    gain = jnp.concatenate([sg_ref[...]] * (tq // LANES), axis=1) * (1.0 - lam_init)
    o_t = o_t * lax.rsqrt(jnp.mean(o_t * o_t, axis=0, keepdims=True) + SUBLN_EPS) * gain
    o_ref[...] = o_t.T.astype(BF16)


def _transpose_kernel(x_ref, o_ref):
    o_ref[...] = x_ref[...].astype(F32).T.astype(BF16)


def _diff_attention(qkv, lam_vecs, subln_g, lam_init, n_batch, seq, ctx_len, need_ctx):
    rows, width = qkv.shape
    d = width // 3
    hd = d // DIFF_HEADS
    tq, tk = ATTN_Q_TILE, ATTN_K_TILE
    nq = seq // tq
    assert seq % (KEY_TILES_PER_BODY * tk) == 0 and KEY_TILES_PER_BODY % 2 == 0
    ctx_blk0 = (n_batch * seq) // ctx_len
    nkt = seq // tk
    scratch = lambda t: [pltpu.VMEM((2, 1, t), F32), pltpu.VMEM((2, 1, t), F32), pltpu.VMEM((2, hd, t), F32)]
    small = [pl.BlockSpec((4, hd // 2), lambda b, h, qi: (0, 0)),
             pl.BlockSpec((hd, LANES), lambda b, h, qi: (0, 0))]
    ctx_kv = [pl.BlockSpec((ctx_len, hd), lambda b, h, qi: (ctx_blk0 + b, DIFF_HEADS + h)),
              pl.BlockSpec((ctx_len, hd), lambda b, h, qi: (ctx_blk0 + b, 2 * DIFF_HEADS + h))]
    def transposed(tile, col_blk, name):
        n = seq // tile
        return pl.pallas_call(
            _transpose_kernel,
            grid=(n_batch, n),
            in_specs=[pl.BlockSpec((tile, d), lambda b, j: (b * n + j, col_blk))],
            out_specs=pl.BlockSpec((None, None, d, tile), lambda b, j: (b, j, 0, 0)),
            out_shape=jax.ShapeDtypeStruct((n_batch, n, d, tile), BF16),
            compiler_params=_params("parallel", "parallel"),
            name=name,
        )(qkv)

    v_t = transposed(tk, 2, "value_transpose")
    q_t = transposed(tq, 0, "query_transpose")
    y = pl.pallas_call(
        functools.partial(_attn_kernel, tk=tk, n_kt=nkt, lam_init=lam_init),
        grid=(n_batch, DIFF_HEADS, nq),
        in_specs=small + [pl.BlockSpec((None, None, hd, tq), lambda b, h, qi: (b, qi, h, 0))] + ctx_kv + [
            pl.BlockSpec((seq, hd), lambda b, h, qi: (b, DIFF_HEADS + h)),
            pl.BlockSpec((None, nkt, hd, tk), lambda b, h, qi: (b, 0, h, 0)),
        ],
        out_specs=pl.BlockSpec((tq, hd), lambda b, h, qi: (b * nq + qi, h)),
        out_shape=jax.ShapeDtypeStruct((rows, d), BF16),
        scratch_shapes=scratch(tq) + [pltpu.VMEM((2, 2, tk, tq), F32), pltpu.VMEM((2, 2, 1, tq), F32)],
        compiler_params=_params("parallel", "parallel", "arbitrary"),
        name="diff_attention_latent",
    )(lam_vecs, subln_g, q_t, qkv, qkv, qkv, v_t)
    if not need_ctx:
        return y
    return pl.pallas_call(
        functools.partial(_attn_kernel, tk=tk, n_kt=0, lam_init=lam_init),
        grid=(n_batch, DIFF_HEADS, 1),
        in_specs=small + [pl.BlockSpec((ctx_len, hd), lambda b, h, qi: (ctx_blk0 + b, h))] + ctx_kv + [
            pl.BlockSpec(memory_space=pl.ANY)],
        out_specs=pl.BlockSpec((ctx_len, hd), lambda b, h, qi: (ctx_blk0 + b, h)),
        out_shape=jax.ShapeDtypeStruct((rows, d), BF16),
        scratch_shapes=scratch(ctx_len),
        input_output_aliases={5: 0},
        compiler_params=_params("parallel", "parallel", "arbitrary"),
        name="diff_attention_context",
    )(lam_vecs, subln_g, qkv, qkv, qkv, y)


def _ret_fwd_kernel(lg_ref, q_ref, k_ref, v_ref, o_ref, d_ref, s_ref):
    n = pl.program_id(1)
    c = q_ref.shape[0]
    hd = q_ref.shape[1] // RET_HEADS

    @pl.when(n == 0)
    def _():
        s_ref[...] = jnp.zeros(s_ref.shape, F32)
        i = lax.broadcasted_iota(jnp.int32, (c, c), 0)
        j = lax.broadcasted_iota(jnp.int32, (c, c), 1)
        dist = (i - j).astype(F32)
        for h in range(RET_HEADS):
            fwd = jnp.exp(lg_ref[0, h] * jnp.maximum(dist, 0.0))
            bwd = jnp.exp(lg_ref[1, h] * jnp.maximum(-dist, 0.0))
            d_ref[h] = jnp.where(dist >= 0, fwd, bwd)

    pos = lax.broadcasted_iota(jnp.int32, (c, 1), 0).astype(F32)
    for h in range(RET_HEADS):
        lg = lg_ref[0, h]
        sl = slice(h * hd, (h + 1) * hd)
        qh, kh, vh = q_ref[:, sl], k_ref[:, sl], v_ref[:, sl]
        p = (_dot_nt(qh, kh) * d_ref[h]).astype(BF16)
        qd = (qh.astype(F32) * jnp.exp(lg * (pos + 1.0))).astype(BF16)
        s = s_ref[h]
        o_ref[:, sl] = _dot(p, vh) + _dot(qd, s.astype(BF16))
        kd = (kh.astype(F32) * jnp.exp(lg * (c - 1.0 - pos))).astype(BF16)
        s_ref[h] = jnp.exp(jnp.full((1, 1), lg * c, F32)) * s + _dot_tn(kd, vh)


def _ret_bwd_kernel(lg_ref, q_ref, k_ref, v_ref, g_ref, r_ref, o_ref, s_ref):
    n = pl.program_id(1)
    c = q_ref.shape[0]
    hd = q_ref.shape[1] // RET_HEADS

    @pl.when(n == 0)
    def _():
        s_ref[...] = jnp.zeros(s_ref.shape, F32)

    pos = lax.broadcasted_iota(jnp.int32, (c, 1), 0).astype(F32)
    for h in range(RET_HEADS):
        lg = lg_ref[1, h]
        sl = slice(h * hd, (h + 1) * hd)
        qh, kh, vh = q_ref[:, sl], k_ref[:, sl], v_ref[:, sl]
        qd = (qh.astype(F32) * jnp.exp(lg * (c - pos))).astype(BF16)
        s = s_ref[h]
        r = r_ref[:, sl] + _dot(qd, s.astype(BF16))
        y = r * lax.rsqrt(jnp.mean(r * r, axis=-1, keepdims=True) + NORM_EPS)
        o_ref[:, sl] = (y * _silu(g_ref[:, sl].astype(F32))).astype(BF16)
        kd = (kh.astype(F32) * jnp.exp(lg * pos)).astype(BF16)
        s_ref[h] = jnp.exp(jnp.full((1, 1), lg * c, F32)) * s + _dot_tn(kd, vh)


def _retention(qkv, log_decay, n_batch, seq, ctx_len):
    rows = qkv.shape[0]
    c = ctx_len
    w = (qkv.shape[1] // 5)
    nc = seq // c
    ctx_blk0 = (n_batch * seq) // c
    fwd_blk = lambda b, n: jnp.where(n == 0, ctx_blk0 + b, b * nc + n - 1)
    bwd_blk = lambda b, n: jnp.where(n == 0, ctx_blk0 + b, b * nc + nc - n)
    smem = pl.BlockSpec(memory_space=pltpu.SMEM)
    state = pltpu.VMEM((RET_HEADS, w // RET_HEADS, w // RET_HEADS), F32)
    r1 = pl.pallas_call(
        _ret_fwd_kernel,
        grid=(n_batch, nc + 1),
        in_specs=[smem] + [pl.BlockSpec((c, w), functools.partial(lambda b, n, j: (fwd_blk(b, n), j), j=j)) for j in range(3)],
        out_specs=pl.BlockSpec((c, w), lambda b, n: (fwd_blk(b, n), 0)),
        out_shape=jax.ShapeDtypeStruct((rows, w), F32),
        scratch_shapes=[pltpu.VMEM((RET_HEADS, c, c), F32), state],
        compiler_params=_params("parallel", "arbitrary"),
        name="retention_forward_sweep",
    )(log_decay, qkv, qkv, qkv)
    return pl.pallas_call(
        _ret_bwd_kernel,
        grid=(n_batch, nc + 1),
        in_specs=[smem] + [pl.BlockSpec((c, w), functools.partial(lambda b, n, j: (bwd_blk(b, n), j), j=j)) for j in range(4)]
        + [pl.BlockSpec((c, w), lambda b, n: (bwd_blk(b, n), 0))],
        out_specs=pl.BlockSpec((c, w), lambda b, n: (bwd_blk(b, n), 0)),
        out_shape=jax.ShapeDtypeStruct((rows, w), BF16),
        scratch_shapes=[state],
        compiler_params=_params("parallel", "arbitrary"),
        name="retention_backward_sweep",
    )(log_decay, qkv, qkv, qkv, qkv, r1)


def _pool_kernel(prev_ref, main_ref, next_ref, w_ref, sc_ref, o_ref, *, lat_tiles, tiles_per_seq, seq, ctx_len):
    i = pl.program_id(0)
    tp = main_ref.shape[0]
    gw = main_ref.shape[1] // len(POOL_WINDOWS)
    is_lat = i < lat_tiles
    t0 = jnp.where(is_lat, (i % tiles_per_seq) * tp, 0)
    length = jnp.where(is_lat, seq, ctx_len)
    n_ext = tp + 2 * POOL_HALO
    t = t0 - POOL_HALO + lax.broadcasted_iota(jnp.int32, (n_ext, 1), 0)
    valid = (t >= 0) & (t < length)
    tm = t[POOL_HALO:POOL_HALO + tp]
    for gi, win in enumerate(POOL_WINDOWS):
        sl = slice(gi * gw, (gi + 1) * gw)
        ext = jnp.concatenate([prev_ref[:, sl], main_ref[:, sl], next_ref[:, sl]], axis=0).astype(F32)
        x = jnp.where(valid, ext, 0.0)
        acc = x + pltpu.roll(x, 1, 0)
        half = 1
        while 2 * half < win:
            acc = pltpu.roll(acc, n_ext - half, 0) + pltpu.roll(acc, half, 0)
            half *= 2
        cnt = (jnp.minimum(tm + win // 2, length) - jnp.maximum(tm - win // 2, 0)).astype(F32)
        pooled = acc[POOL_HALO:POOL_HALO + tp] / cnt - x[POOL_HALO:POOL_HALO + tp]
        o_ref[:, sl] = (_dot(pooled.astype(BF16), w_ref[gi]) * sc_ref[:, sl]).astype(BF16)


def _pool(qkv, pool_w, pool_scale, idx, n_batch, seq, ctx_len):
    rows = qkv.shape[0]
    w = qkv.shape[1] // 5
    tp, halo = POOL_TILE, POOL_HALO
    assert ctx_len == tp and seq % tp == 0
    last_halo_blk = rows // halo - 1
    per = tp // halo
    return pl.pallas_call(
        functools.partial(_pool_kernel, lat_tiles=(n_batch * seq) // tp, tiles_per_seq=seq // tp, seq=seq, ctx_len=ctx_len),
        grid=(rows // tp,),
        in_specs=[
            pl.BlockSpec((halo, w), lambda i: (jnp.maximum(i * per - 1, 0), 4)),
            pl.BlockSpec((tp, w), lambda i: (i, 4)),
            pl.BlockSpec((halo, w), lambda i: (jnp.minimum((i + 1) * per, last_halo_blk), 4)),
            pl.BlockSpec((None,) + pool_w.shape[1:], lambda i: (idx, 0, 0, 0)),
            pl.BlockSpec((None, 1, w), lambda i: (idx, 0, 0)),
        ],
        out_specs=pl.BlockSpec((tp, w), lambda i: (i, 0)),
        out_shape=jax.ShapeDtypeStruct((rows, w), BF16),
        compiler_params=_params("parallel"),
        name="multiscale_pool",
    )(qkv, qkv, qkv, pool_w, pool_scale.reshape(pool_scale.shape[0], 1, w))


def _out_kernel(h_ref, m_ref, g_ref, ya_ref, yb_ref, wa_ref, wb_ref, o_ref):
    y = _dot(ya_ref[...], wa_ref[...]) + _dot(yb_ref[...], wb_ref[...])
    o_ref[...] = h_ref[...] + m_ref[5:6, :] * _rms(y, g_ref[3:4, :])


def _out_proj(h, mods, norm_g, ya, ya_blk, yb, yb_blk, w_out, layer, idx, n_tiles, grp):
    d = h.shape[1]
    tm = ROW_TILE
    return pl.pallas_call(
        _out_kernel,
        grid=(n_tiles,),
        in_specs=[
            pl.BlockSpec((tm, d), lambda i: (i, 0)),
            pl.BlockSpec((None, None, N_MOD, d), lambda i: (layer, grp(i), 0, 0)),
            pl.BlockSpec((None, 6, d), lambda i: (layer, 0, 0)),
            pl.BlockSpec((tm, d // 2), lambda i: (i, ya_blk)),
            pl.BlockSpec((tm, d // 2), lambda i: (i, yb_blk)),
            pl.BlockSpec((None, d // 2, d), lambda i: (idx, 0, 0)),
            pl.BlockSpec((None, d // 2, d), lambda i: (idx, 1, 0)),
        ],
        out_specs=pl.BlockSpec((tm, d), lambda i: (i, 0)),
        out_shape=jax.ShapeDtypeStruct((n_tiles * tm, d), F32),
        compiler_params=_params("parallel"),
        name=f"mixer_out_l{layer}",
    )(h, mods, norm_g, ya, yb, w_out, w_out)


def kernel(x, c, ctx, c_ctx, mod_w, mod_b, norm_g, ffn_w_gate, ffn_w_up, ffn_w_down, ev_w_in, ev_w_out, ret_decay, pool_w, pool_scale, od_w_in, od_w_out, lam_q1, lam_k1, lam_q2, lam_k2, diff_subln_g):
    n_batch, seq, d = x.shape
    ctx_len = ctx.shape[1]
    depth = mod_w.shape[0]
    tm = ROW_TILE
    lat_rows, ctx_rows = n_batch * seq, n_batch * ctx_len
    assert seq % tm == 0 and ctx_rows % tm == 0 and n_batch + 1 <= 8
    assert d // 2 // RET_HEADS == 2 * LANES and d // DIFF_HEADS == 2 * LANES and ctx_len == POOL_TILE
    lat_tiles, all_tiles = lat_rows // tm, (lat_rows + ctx_rows) // tm
    tiles_per_batch = seq // tm
    grp = lambda i: jnp.minimum(i // tiles_per_batch, n_batch)
    assert seq % WIDE_ROW_TILE == 0
    wide_per_batch = seq // WIDE_ROW_TILE
    wide_grp = lambda i: jnp.minimum(i // wide_per_batch, n_batch)
    rope_blk = lambda i: jnp.where(i < n_batch * wide_per_batch, i % wide_per_batch, wide_per_batch)

    cond = jnp.concatenate([c, c_ctx[None, :], jnp.zeros((8 - n_batch - 1, d), F32)], axis=0)
    mods = _modulation(cond, mod_w, mod_b)[:, :n_batch + 1].reshape(depth, n_batch + 1, N_MOD, d)

    wg, wu, wd = ffn_w_gate.astype(BF16), ffn_w_up.astype(BF16), ffn_w_down.astype(BF16)
    ev_in, ev_out, od_in, od_out = ev_w_in.astype(BF16), ev_w_out.astype(BF16), od_w_in.astype(BF16), od_w_out.astype(BF16)
    pw = pool_w.astype(BF16)
    log_decay = jax.nn.log_sigmoid(ret_decay.astype(F32))
    tabs_even = _rope_tables(seq, WIDE_ROW_TILE, True)
    tabs_odd = _rope_tables(seq, WIDE_ROW_TILE, False)

    h = jnp.concatenate([x.reshape(lat_rows, d), ctx.reshape(ctx_rows, d)], axis=0)
    for l in range(depth):
        need_ctx = l < depth - 1
        out_tiles = all_tiles if need_ctx else lat_tiles
        h = _ffn(h, mods, norm_g, wg, wu, wd, l, 0, lat_rows + ctx_rows, wide_grp)
        if l % 2 == 0:
            e = l // 2
            qkv = _proj_in(h, mods, norm_g, ev_in, tabs_even, l, e, True, wide_grp, rope_blk)
            y = _retention(qkv, log_decay[e], n_batch, seq, ctx_len)
            pm = _pool(qkv, pw, pool_scale, e, n_batch, seq, ctx_len)
            h = _out_proj(h, mods, norm_g, y, 0, pm, 0, ev_out, l, e, out_tiles, grp)
        else:
            o = l // 2
            lam_init = 0.8 - 0.6 * math.exp(-0.3 * l)
            qkv = _proj_in(h, mods, norm_g, od_in, tabs_odd, l, o, False, wide_grp, rope_blk)
            lam_vecs = jnp.stack([lam_q1[o], lam_k1[o], lam_q2[o], lam_k2[o]], axis=0)
            gain = jnp.broadcast_to(diff_subln_g[o][:, None], (diff_subln_g.shape[1], LANES))
            y = _diff_attention(qkv, lam_vecs, gain, lam_init, n_batch, seq, ctx_len, need_ctx)
            h = _out_proj(h, mods, norm_g, y, 0, y, 1, od_out, l, o, out_tiles, grp)
        h = _ffn(h, mods, norm_g, wg, wu, wd, l, 1, out_tiles * tm, wide_grp)
    return h[:lat_rows].reshape(n_batch, seq, d)
```

```python
import functools
import math

import jax
import jax.numpy as jnp
from jax import lax
from jax.experimental import pallas as pl
from jax.experimental.pallas import tpu as pltpu

F32 = jnp.float32
BF16 = jnp.bfloat16

GRID_W = 64
ROPE_BASE = 10000.0
N_MOD = 9
RET_HEADS = 4
POOL_WINDOWS = (2, 4, 8, 16)
DIFF_HEADS = 8
NORM_EPS = 1e-6
SUBLN_EPS = 1e-5

LANES = 128
ROW_TILE = 512
WIDE_ROW_TILE = 1024
FFN_CHUNK = 512
NORM_STRIP = 16
PROJ_CHUNK = 1024
MOD_CHUNK = 1024
ATTN_Q_TILE = 1024
ATTN_K_TILE = 512
KEY_TILES_PER_BODY = 8
POOL_TILE = 256
POOL_HALO = 16
VMEM_LIMIT = 56 * 1024 * 1024
FFN_VMEM_LIMIT = 60 * 1024 * 1024


def _params(*sem):
    return pltpu.CompilerParams(dimension_semantics=sem, vmem_limit_bytes=VMEM_LIMIT)


def _silu(x):
    return x * jax.nn.sigmoid(x)


def _rms(x, g, eps=NORM_EPS):
    return x * lax.rsqrt(jnp.mean(x * x, axis=-1, keepdims=True) + eps) * g


def _modulated_norm_strips(h_ref, u_ref, g, shift, scale, zero_ref=None):
    gain = g * (1.0 + scale)

    def strip(i, carry):
        rows = pl.ds(pl.multiple_of(i * NORM_STRIP, NORM_STRIP), NORM_STRIP)
        x = h_ref[rows, :]
        inv = lax.rsqrt(jnp.mean(x * x, axis=-1, keepdims=True) + NORM_EPS)
        u_ref[rows, :] = (x * inv * gain + shift).astype(BF16)
        if zero_ref is not None:
            zero_ref[rows, :] = jnp.zeros((NORM_STRIP, zero_ref.shape[1]), F32)
        return carry

    lax.fori_loop(0, h_ref.shape[0] // NORM_STRIP, strip, 0, unroll=8)


def _dot(a, b):
    return jnp.dot(a, b, preferred_element_type=F32)


def _dot_nt(a, b):
    return lax.dot_general(a, b, (((1,), (1,)), ((), ())), preferred_element_type=F32)


def _dot_tn(a, b):
    return lax.dot_general(a, b, (((0,), (0,)), ((), ())), preferred_element_type=F32)


def _mod_kernel(a_ref, w_ref, b_ref, o_ref):
    s = _silu(a_ref[...]).astype(BF16)
    o_ref[...] = _dot(s, w_ref[...].astype(BF16)) + b_ref[...]


def _modulation(cond, mod_w, mod_b):
    depth, d, n = mod_w.shape
    tn = MOD_CHUNK
    return pl.pallas_call(
        _mod_kernel,
        grid=(depth, n // tn),
        in_specs=[
            pl.BlockSpec((8, d), lambda l, j: (0, 0)),
            pl.BlockSpec((None, d, tn), lambda l, j: (l, 0, j)),
            pl.BlockSpec((None, 1, tn), lambda l, j: (l, 0, j)),
        ],
        out_specs=pl.BlockSpec((None, 8, tn), lambda l, j: (l, 0, j)),
        out_shape=jax.ShapeDtypeStruct((depth, 8, n), F32),
        compiler_params=_params("parallel", "parallel"),
        name="adaln_modulation",
    )(cond, mod_w, mod_b.reshape(depth, 1, n))


def _ffn_kernel(h_ref, m_ref, g_ref, wg_ref, wu_ref, wd_ref, o_ref, u_ref, *, mod0, g0):
    k = pl.program_id(1)
    sub = ROW_TILE
    n_strips = h_ref.shape[0] // NORM_STRIP

    @pl.when(k == 0)
    def _():
        _modulated_norm_strips(h_ref, u_ref, g_ref[g0:g0 + 1, :], m_ref[mod0:mod0 + 1, :], m_ref[mod0 + 1:mod0 + 2, :],
                               zero_ref=o_ref)

    for s in range(h_ref.shape[0] // sub):
        rows = slice(s * sub, (s + 1) * sub)
        u = u_ref[rows, :]
        a = _dot(u, wg_ref[...])
        b = _dot(u, wu_ref[...])
        o_ref[rows, :] += _dot((_silu(a) * b).astype(BF16), wd_ref[...])

    @pl.when(k == pl.num_programs(1) - 1)
    def _():
        gain = (0.5 * m_ref[mod0 + 2:mod0 + 3, :]) * g_ref[g0 + 1:g0 + 2, :]

        for i in range(n_strips):
            rows = slice(i * NORM_STRIP, (i + 1) * NORM_STRIP)
            y = o_ref[rows, :]
            inv = lax.rsqrt(jnp.mean(y * y, axis=-1, keepdims=True) + NORM_EPS)
            o_ref[rows, :] = h_ref[rows, :] + y * inv * gain


def _ffn(h, mods, norm_g, wg, wu, wd, layer, half, out_rows, grp):
    d = h.shape[1]
    f = wg.shape[-1]
    tm, th = WIDE_ROW_TILE, FFN_CHUNK
    return pl.pallas_call(
        functools.partial(_ffn_kernel, mod0=6 * half, g0=4 * half),
        grid=(pl.cdiv(out_rows, tm), f // th),
        in_specs=[
            pl.BlockSpec((tm, d), lambda i, k: (i, 0)),
            pl.BlockSpec((None, None, N_MOD, d), lambda i, k: (layer, grp(i), 0, 0)),
            pl.BlockSpec((None, 6, d), lambda i, k: (layer, 0, 0)),
            pl.BlockSpec((None, None, d, th), lambda i, k: (layer, half, 0, k)),
            pl.BlockSpec((None, None, d, th), lambda i, k: (layer, half, 0, k)),
            pl.BlockSpec((None, None, th, d), lambda i, k: (layer, half, k, 0)),
        ],
        out_specs=pl.BlockSpec((tm, d), lambda i, k: (i, 0)),
        out_shape=jax.ShapeDtypeStruct((out_rows, d), F32),
        scratch_shapes=[pltpu.VMEM((tm, d), BF16)],
        compiler_params=pltpu.CompilerParams(dimension_semantics=("parallel", "arbitrary"),
                                             vmem_limit_bytes=FFN_VMEM_LIMIT),
        name=f"swiglu_l{layer}_{half}",
    )(h, mods, norm_g, wg, wu, wd)


def _proj_kernel(*refs, even, n_rope, scale):
    if even:
        h_ref, m_ref, g_ref, w_ref, c_ref, sa_ref, o_ref, u_ref = refs
    else:
        h_ref, m_ref, g_ref, w_ref, c_ref, sa_ref, sb_ref, o_ref, u_ref = refs
    n = pl.program_id(1)

    @pl.when(n == 0)
    def _():
        _modulated_norm_strips(h_ref, u_ref, g_ref[2:3, :], m_ref[3:4, :], m_ref[4:5, :])

    is_rope = n < n_rope
    first = n < n_rope // 2
    sc = jnp.where(is_rope, jnp.where(first, 1.0, scale) if even else jnp.where(first, scale, 1.0), 1.0)
    half, quarter = LANES // 2, LANES // 4
    for s in range(h_ref.shape[0] // ROW_TILE):
        rows = slice(s * ROW_TILE, (s + 1) * ROW_TILE)
        r = _dot(u_ref[rows, :], w_ref[...])
        cos = jnp.where(is_rope, c_ref[rows, :], 1.0) * sc
        sin_a = jnp.where(is_rope, sa_ref[rows, :], 0.0) * sc
        if not even:
            sin_b = jnp.where(is_rope, sb_ref[rows, :], 0.0) * sc
        for j in range(r.shape[1] // LANES):
            xb = r[:, j * LANES:(j + 1) * LANES]
            if even:
                tb = (j % 2) * LANES
                ob = xb * cos[:, tb:tb + LANES] + pltpu.roll(xb, half, 1) * sin_a[:, tb:tb + LANES]
            else:
                ob = xb * cos + pltpu.roll(xb, LANES - quarter, 1) * sin_a + pltpu.roll(xb, quarter, 1) * sin_b
            o_ref[rows, j * LANES:(j + 1) * LANES] = ob.astype(BF16)


def _proj_in(h, mods, norm_g, w_in, tables, layer, idx, even, grp, rope_blk):
    out_rows, d = h.shape
    n_out = w_in.shape[-1]
    tm, tn = WIDE_ROW_TILE, PROJ_CHUNK
    tw = tables[0].shape[1]
    if even:
        n_rope, scale = (2 * (d // 2)) // tn, float((d // 2 // RET_HEADS) ** -0.5)
    else:
        n_rope, scale = (2 * d) // tn, float((d // DIFF_HEADS // 2) ** -0.5 * math.log2(math.e))
    return pl.pallas_call(
        functools.partial(_proj_kernel, even=even, n_rope=n_rope, scale=scale),
        grid=(pl.cdiv(out_rows, tm), n_out // tn),
        in_specs=[
            pl.BlockSpec((tm, d), lambda i, n: (i, 0)),
            pl.BlockSpec((None, None, N_MOD, d), lambda i, n: (layer, grp(i), 0, 0)),
            pl.BlockSpec((None, 6, d), lambda i, n: (layer, 0, 0)),
            pl.BlockSpec((None, d, tn), lambda i, n: (idx, 0, n)),
        ] + [pl.BlockSpec((tm, tw), lambda i, n: (rope_blk(i), 0)) for _ in tables],
        out_specs=pl.BlockSpec((tm, tn), lambda i, n: (i, n)),
        out_shape=jax.ShapeDtypeStruct((out_rows, n_out), BF16),
        scratch_shapes=[pltpu.VMEM((tm, d), BF16)],
        compiler_params=_params("parallel", "arbitrary"),
        name=f"mixer_in_l{layer}",
    )(h, mods, norm_g, w_in, *tables)


def _rope_tables(seq, n_ident, even):
    n_rows = seq // GRID_W
    nf = (2 * LANES if even else LANES) // 4
    inv = ROPE_BASE ** (-jnp.arange(nf, dtype=F32) / nf)
    ar = jnp.arange(n_rows, dtype=F32)[:, None] * inv[None, :]
    ac = jnp.arange(GRID_W, dtype=F32)[:, None] * inv[None, :]
    cr, sr = jnp.repeat(jnp.cos(ar), GRID_W, axis=0), jnp.repeat(jnp.sin(ar), GRID_W, axis=0)
    cc, sc = jnp.tile(jnp.cos(ac), (n_rows, 1)), jnp.tile(jnp.sin(ac), (n_rows, 1))
    z = jnp.zeros_like(sr)
    if even:
        tabs = [jnp.concatenate([cr, cr, cc, cc], 1), jnp.concatenate([-sr, sr, -sc, sc], 1)]
    else:
        tabs = [jnp.concatenate([cr, cr, cc, cc], 1), jnp.concatenate([-sr, z, -sc, z], 1),
                jnp.concatenate([z, sr, z, sc], 1)]
    ident = [jnp.ones((n_ident, tabs[0].shape[1]), F32)] + [jnp.zeros((n_ident, tabs[0].shape[1]), F32)] * (len(tabs) - 1)
    return [jnp.concatenate([a, b], 0) for a, b in zip(tabs, ident)]


def _attn_kernel(*refs, tk, n_kt, lam_init):
    if n_kt:
        lam_ref, sg_ref, q_ref, kc_ref, vc_ref, k_ref, vt_ref, o_ref, m_ref, l_ref, acc_ref, s_ref, mx_ref = refs
    else:
        lam_ref, sg_ref, q_ref, kc_ref, vc_ref, _, o_ref, m_ref, l_ref, acc_ref = refs
    if n_kt:
        dh, tq = q_ref.shape[0] // 2, q_ref.shape[1]

        def scores_t(kb, mi):
            return _dot(kb[:, mi * dh:(mi + 1) * dh], q_ref[mi * dh:(mi + 1) * dh, :])
    else:
        dh, tq = q_ref.shape[1] // 2, q_ref.shape[0]

        def scores_t(kb, mi):
            return _dot_nt(kb[:, mi * dh:(mi + 1) * dh], q_ref[:, mi * dh:(mi + 1) * dh])

    def absorb(slot, vt, mi):
        m_prev = m_ref[mi]
        m_next = jnp.maximum(m_prev, mx_ref[slot, mi])
        alpha = jnp.exp2(m_prev - m_next)
        p_t = jnp.exp2(s_ref[slot, mi] - m_next)
        l_ref[mi] = alpha * l_ref[mi] + jnp.sum(p_t, axis=0, keepdims=True)
        acc_ref[mi] = alpha * acc_ref[mi] + _dot(vt, p_t.astype(BF16))
        m_ref[mi] = m_next

    if n_kt:
        def qk(j, slot):
            kb = k_ref[pl.ds(pl.multiple_of(j * tk, tk), tk), :]
            for mi in range(2):
                s_t = scores_t(kb, mi)
                s_ref[slot, mi] = s_t
                mx_ref[slot, mi] = jnp.max(s_t, axis=0, keepdims=True)

        def pv(j, slot):
            vt = vt_ref[j]
            for mi in range(2):
                absorb(slot, vt, mi)

        qk(0, 0)

    kc, vc = kc_ref[...], vc_ref[...]
    for mi in range(2):
        s_t = scores_t(kc, mi)
        m0 = jnp.max(s_t, axis=0, keepdims=True)
        p_t = jnp.exp2(s_t - m0)
        m_ref[mi] = m0
        l_ref[mi] = jnp.sum(p_t, axis=0, keepdims=True)
        acc_ref[mi] = _dot_tn(vc, p_t.astype(BF16))

    if n_kt:
        def body(i, carry):
            j = KEY_TILES_PER_BODY * i
            for u in range(KEY_TILES_PER_BODY):
                qk(j + u + 1, (u + 1) % 2)
                pv(j + u, u % 2)
            return carry

        lax.fori_loop(0, n_kt // KEY_TILES_PER_BODY - 1, body, 0)
        for j in range(n_kt - KEY_TILES_PER_BODY, n_kt):
            if j + 1 < n_kt:
                qk(j + 1, (j + 1) % 2)
            pv(j, j % 2)

    t1 = jnp.sum(lam_ref[0:1, :] * lam_ref[1:2, :], axis=-1, keepdims=True)
    t2 = jnp.sum(lam_ref[2:3, :] * lam_ref[3:4, :], axis=-1, keepdims=True)
    lam = jnp.exp(t1) - jnp.exp(t2) + lam_init
    o_t = acc_ref[0] * (1.0 / l_ref[0]) - acc_ref[1] * (lam / l_ref[1])
    gain = jnp.concatenate([sg_ref[...]] * (tq // LANES), axis=1) * (1.0 - lam_init)
    o_t = o_t * lax.rsqrt(jnp.mean(o_t * o_t, axis=0, keepdims=True) + SUBLN_EPS) * gain
    o_ref[...] = o_t.T.astype(BF16)


def _transpose_kernel(x_ref, o_ref):
    o_ref[...] = x_ref[...].astype(F32).T.astype(BF16)


def _diff_attention(qkv, lam_vecs, subln_g, lam_init, n_batch, seq, ctx_len, need_ctx):
    rows, width = qkv.shape
    d = width // 3
    hd = d // DIFF_HEADS
    tq, tk = ATTN_Q_TILE, ATTN_K_TILE
    nq = seq // tq
    assert seq % (KEY_TILES_PER_BODY * tk) == 0 and KEY_TILES_PER_BODY % 2 == 0
    ctx_blk0 = (n_batch * seq) // ctx_len
    nkt = seq // tk
    scratch = lambda t: [pltpu.VMEM((2, 1, t), F32), pltpu.VMEM((2, 1, t), F32), pltpu.VMEM((2, hd, t), F32)]
    small = [pl.BlockSpec((4, hd // 2), lambda b, h, qi: (0, 0)),
             pl.BlockSpec((hd, LANES), lambda b, h, qi: (0, 0))]
    ctx_kv = [pl.BlockSpec((ctx_len, hd), lambda b, h, qi: (ctx_blk0 + b, DIFF_HEADS + h)),
              pl.BlockSpec((ctx_len, hd), lambda b, h, qi: (ctx_blk0 + b, 2 * DIFF_HEADS + h))]
    def transposed(tile, col_blk, name):
        n = seq // tile
        return pl.pallas_call(
            _transpose_kernel,
            grid=(n_batch, n),
            in_specs=[pl.BlockSpec((tile, d), lambda b, j: (b * n + j, col_blk))],
            out_specs=pl.BlockSpec((None, None, d, tile), lambda b, j: (b, j, 0, 0)),
            out_shape=jax.ShapeDtypeStruct((n_batch, n, d, tile), BF16),
            compiler_params=_params("parallel", "parallel"),
            name=name,
        )(qkv)

    v_t = transposed(tk, 2, "value_transpose")
    q_t = transposed(tq, 0, "query_transpose")
    y = pl.pallas_call(
        functools.partial(_attn_kernel, tk=tk, n_kt=nkt, lam_init=lam_init),
        grid=(n_batch, DIFF_HEADS, nq),
        in_specs=small + [pl.BlockSpec((None, None, hd, tq), lambda b, h, qi: (b, qi, h, 0))] + ctx_kv + [
            pl.BlockSpec((seq, hd), lambda b, h, qi: (b, DIFF_HEADS + h)),
            pl.BlockSpec((None, nkt, hd, tk), lambda b, h, qi: (b, 0, h, 0)),
        ],
        out_specs=pl.BlockSpec((tq, hd), lambda b, h, qi: (b * nq + qi, h)),
        out_shape=jax.ShapeDtypeStruct((rows, d), BF16),
        scratch_shapes=scratch(tq) + [pltpu.VMEM((2, 2, tk, tq), F32), pltpu.VMEM((2, 2, 1, tq), F32)],
        compiler_params=_params("parallel", "parallel", "arbitrary"),
        name="diff_attention_latent",
    )(lam_vecs, subln_g, q_t, qkv, qkv, qkv, v_t)
    if not need_ctx:
        return y
    return pl.pallas_call(
        functools.partial(_attn_kernel, tk=tk, n_kt=0, lam_init=lam_init),
        grid=(n_batch, DIFF_HEADS, 1),
        in_specs=small + [pl.BlockSpec((ctx_len, hd), lambda b, h, qi: (ctx_blk0 + b, h))] + ctx_kv + [
            pl.BlockSpec(memory_space=pl.ANY)],
        out_specs=pl.BlockSpec((ctx_len, hd), lambda b, h, qi: (ctx_blk0 + b, h)),
        out_shape=jax.ShapeDtypeStruct((rows, d), BF16),
        scratch_shapes=scratch(ctx_len),
        input_output_aliases={5: 0},
        compiler_params=_params("parallel", "parallel", "arbitrary"),
        name="diff_attention_context",
    )(lam_vecs, subln_g, qkv, qkv, qkv, y)


def _ret_fwd_kernel(lg_ref, q_ref, k_ref, v_ref, o_ref, d_ref, s_ref):
    n = pl.program_id(1)
    c = q_ref.shape[0]
    hd = q_ref.shape[1] // RET_HEADS

    @pl.when(n == 0)
    def _():
        s_ref[...] = jnp.zeros(s_ref.shape, F32)
        i = lax.broadcasted_iota(jnp.int32, (c, c), 0)
        j = lax.broadcasted_iota(jnp.int32, (c, c), 1)
        dist = (i - j).astype(F32)
        for h in range(RET_HEADS):
            fwd = jnp.exp(lg_ref[0, h] * jnp.maximum(dist, 0.0))
            bwd = jnp.exp(lg_ref[1, h] * jnp.maximum(-dist, 0.0))
            d_ref[h] = jnp.where(dist >= 0, fwd, bwd)

    pos = lax.broadcasted_iota(jnp.int32, (c, 1), 0).astype(F32)
    for h in range(RET_HEADS):
        lg = lg_ref[0, h]
        sl = slice(h * hd, (h + 1) * hd)
        qh, kh, vh = q_ref[:, sl], k_ref[:, sl], v_ref[:, sl]
        p = (_dot_nt(qh, kh) * d_ref[h]).astype(BF16)
        qd = (qh.astype(F32) * jnp.exp(lg * (pos + 1.0))).astype(BF16)
        s = s_ref[h]
        o_ref[:, sl] = _dot(p, vh) + _dot(qd, s.astype(BF16))
        kd = (kh.astype(F32) * jnp.exp(lg * (c - 1.0 - pos))).astype(BF16)
        s_ref[h] = jnp.exp(jnp.full((1, 1), lg * c, F32)) * s + _dot_tn(kd, vh)


def _ret_bwd_kernel(lg_ref, q_ref, k_ref, v_ref, g_ref, r_ref, o_ref, s_ref, w_ref):
    n = pl.program_id(1)
    c = q_ref.shape[0]
    hd = q_ref.shape[1] // RET_HEADS

    @pl.when(n == 0)
    def _():
        s_ref[...] = jnp.zeros(s_ref.shape, F32)
        pos_w = lax.broadcasted_iota(jnp.int32, (c, hd), 0).astype(F32)
        for h in range(RET_HEADS):
            w_ref[h, 0] = jnp.exp(lg_ref[1, h] * (c - pos_w))
            w_ref[h, 1] = jnp.exp(lg_ref[1, h] * pos_w)

    for h in range(RET_HEADS):
        lg = lg_ref[1, h]
        sl = slice(h * hd, (h + 1) * hd)
        qh, kh, vh = q_ref[:, sl], k_ref[:, sl], v_ref[:, sl]
        qd = (qh.astype(F32) * w_ref[h, 0]).astype(BF16)
        s = s_ref[h]
        r = r_ref[:, sl] + _dot(qd, s.astype(BF16))
        y = r * lax.rsqrt(jnp.mean(r * r, axis=-1, keepdims=True) + NORM_EPS)
        o_ref[:, sl] = (y * _silu(g_ref[:, sl].astype(F32))).astype(BF16)
        kd = (kh.astype(F32) * w_ref[h, 1]).astype(BF16)
        s_ref[h] = jnp.exp(jnp.full((1, 1), lg * c, F32)) * s + _dot_tn(kd, vh)


def _retention(qkv, log_decay, n_batch, seq, ctx_len):
    rows = qkv.shape[0]
    c = ctx_len
    w = (qkv.shape[1] // 5)
    nc = seq // c
    ctx_blk0 = (n_batch * seq) // c
    fwd_blk = lambda b, n: jnp.where(n == 0, ctx_blk0 + b, b * nc + n - 1)
    bwd_blk = lambda b, n: jnp.where(n == 0, ctx_blk0 + b, b * nc + nc - n)
    smem = pl.BlockSpec(memory_space=pltpu.SMEM)
    state = pltpu.VMEM((RET_HEADS, w // RET_HEADS, w // RET_HEADS), F32)
    r1 = pl.pallas_call(
        _ret_fwd_kernel,
        grid=(n_batch, nc + 1),
        in_specs=[smem] + [pl.BlockSpec((c, w), functools.partial(lambda b, n, j: (fwd_blk(b, n), j), j=j)) for j in range(3)],
        out_specs=pl.BlockSpec((c, w), lambda b, n: (fwd_blk(b, n), 0)),
        out_shape=jax.ShapeDtypeStruct((rows, w), F32),
        scratch_shapes=[pltpu.VMEM((RET_HEADS, c, c), F32), state],
        compiler_params=_params("parallel", "arbitrary"),
        name="retention_forward_sweep",
    )(log_decay, qkv, qkv, qkv)
    return pl.pallas_call(
        _ret_bwd_kernel,
        grid=(n_batch, nc + 1),
        in_specs=[smem] + [pl.BlockSpec((c, w), functools.partial(lambda b, n, j: (bwd_blk(b, n), j), j=j)) for j in range(4)]
        + [pl.BlockSpec((c, w), lambda b, n: (bwd_blk(b, n), 0))],
        out_specs=pl.BlockSpec((c, w), lambda b, n: (bwd_blk(b, n), 0)),
        out_shape=jax.ShapeDtypeStruct((rows, w), BF16),
        scratch_shapes=[state, pltpu.VMEM((RET_HEADS, 2, c, w // RET_HEADS), F32)],
        compiler_params=_params("parallel", "arbitrary"),
        name="retention_backward_sweep",
    )(log_decay, qkv, qkv, qkv, qkv, r1)


def _pool_kernel(prev_ref, main_ref, next_ref, w_ref, sc_ref, o_ref, *, lat_tiles, tiles_per_seq, seq, ctx_len):
    i = pl.program_id(0)
    tp = main_ref.shape[0]
    gw = main_ref.shape[1] // len(POOL_WINDOWS)
    is_lat = i < lat_tiles
    t0 = jnp.where(is_lat, (i % tiles_per_seq) * tp, 0)
    length = jnp.where(is_lat, seq, ctx_len)
    n_ext = tp + 2 * POOL_HALO
    t = t0 - POOL_HALO + lax.broadcasted_iota(jnp.int32, (n_ext, 1), 0)
    valid = (t >= 0) & (t < length)
    tm = t[POOL_HALO:POOL_HALO + tp]
    for gi, win in enumerate(POOL_WINDOWS):
        sl = slice(gi * gw, (gi + 1) * gw)
        ext = jnp.concatenate([prev_ref[:, sl], main_ref[:, sl], next_ref[:, sl]], axis=0).astype(F32)
        x = jnp.where(valid, ext, 0.0)
        acc = x + pltpu.roll(x, 1, 0)
        half = 1
        while 2 * half < win:
            acc = pltpu.roll(acc, n_ext - half, 0) + pltpu.roll(acc, half, 0)
            half *= 2
        cnt = (jnp.minimum(tm + win // 2, length) - jnp.maximum(tm - win // 2, 0)).astype(F32)
        pooled = acc[POOL_HALO:POOL_HALO + tp] / cnt - x[POOL_HALO:POOL_HALO + tp]
        o_ref[:, sl] = (_dot(pooled.astype(BF16), w_ref[gi]) * sc_ref[:, sl]).astype(BF16)


def _pool(qkv, pool_w, pool_scale, idx, n_batch, seq, ctx_len):
    rows = qkv.shape[0]
    w = qkv.shape[1] // 5
    tp, halo = POOL_TILE, POOL_HALO
    assert ctx_len == tp and seq % tp == 0
    last_halo_blk = rows // halo - 1
    per = tp // halo
    return pl.pallas_call(
        functools.partial(_pool_kernel, lat_tiles=(n_batch * seq) // tp, tiles_per_seq=seq // tp, seq=seq, ctx_len=ctx_len),
        grid=(rows // tp,),
        in_specs=[
            pl.BlockSpec((halo, w), lambda i: (jnp.maximum(i * per - 1, 0), 4)),
            pl.BlockSpec((tp, w), lambda i: (i, 4)),
            pl.BlockSpec((halo, w), lambda i: (jnp.minimum((i + 1) * per, last_halo_blk), 4)),
            pl.BlockSpec((None,) + pool_w.shape[1:], lambda i: (idx, 0, 0, 0)),
            pl.BlockSpec((None, 1, w), lambda i: (idx, 0, 0)),
        ],
        out_specs=pl.BlockSpec((tp, w), lambda i: (i, 0)),
        out_shape=jax.ShapeDtypeStruct((rows, w), BF16),
        compiler_params=_params("parallel"),
        name="multiscale_pool",
    )(qkv, qkv, qkv, pool_w, pool_scale.reshape(pool_scale.shape[0], 1, w))


def _out_kernel(h_ref, m_ref, g_ref, ya_ref, yb_ref, wa_ref, wb_ref, o_ref):
    sub = h_ref.shape[0] // 2
    gain = m_ref[5:6, :] * g_ref[3:4, :]
    for s in range(2):
        rows = slice(s * sub, (s + 1) * sub)
        y = _dot(ya_ref[rows, :], wa_ref[...]) + _dot(yb_ref[rows, :], wb_ref[...])
        inv = lax.rsqrt(jnp.mean(y * y, axis=-1, keepdims=True) + NORM_EPS)
        o_ref[rows, :] = h_ref[rows, :] + y * inv * gain


def _out_proj(h, mods, norm_g, ya, ya_blk, yb, yb_blk, w_out, layer, idx, n_tiles, grp):
    d = h.shape[1]
    tm = ROW_TILE
    return pl.pallas_call(
        _out_kernel,
        grid=(n_tiles,),
        in_specs=[
            pl.BlockSpec((tm, d), lambda i: (i, 0)),
            pl.BlockSpec((None, None, N_MOD, d), lambda i: (layer, grp(i), 0, 0)),
            pl.BlockSpec((None, 6, d), lambda i: (layer, 0, 0)),
            pl.BlockSpec((tm, d // 2), lambda i: (i, ya_blk)),
            pl.BlockSpec((tm, d // 2), lambda i: (i, yb_blk)),
            pl.BlockSpec((None, d // 2, d), lambda i: (idx, 0, 0)),
            pl.BlockSpec((None, d // 2, d), lambda i: (idx, 1, 0)),
        ],
        out_specs=pl.BlockSpec((tm, d), lambda i: (i, 0)),
        out_shape=jax.ShapeDtypeStruct((n_tiles * tm, d), F32),
        compiler_params=_params("parallel"),
        name=f"mixer_out_l{layer}",
    )(h, mods, norm_g, ya, yb, w_out, w_out)


def kernel(x, c, ctx, c_ctx, mod_w, mod_b, norm_g, ffn_w_gate, ffn_w_up, ffn_w_down, ev_w_in, ev_w_out, ret_decay, pool_w, pool_scale, od_w_in, od_w_out, lam_q1, lam_k1, lam_q2, lam_k2, diff_subln_g):
    n_batch, seq, d = x.shape
    ctx_len = ctx.shape[1]
    depth = mod_w.shape[0]
    tm = ROW_TILE
    lat_rows, ctx_rows = n_batch * seq, n_batch * ctx_len
    assert seq % tm == 0 and ctx_rows % tm == 0 and n_batch + 1 <= 8
    assert d // 2 // RET_HEADS == 2 * LANES and d // DIFF_HEADS == 2 * LANES and ctx_len == POOL_TILE
    lat_tiles, all_tiles = lat_rows // tm, (lat_rows + ctx_rows) // tm
    tiles_per_batch = seq // tm
    grp = lambda i: jnp.minimum(i // tiles_per_batch, n_batch)
    assert seq % WIDE_ROW_TILE == 0
    wide_per_batch = seq // WIDE_ROW_TILE
    wide_grp = lambda i: jnp.minimum(i // wide_per_batch, n_batch)
    rope_blk = lambda i: jnp.where(i < n_batch * wide_per_batch, i % wide_per_batch, wide_per_batch)

    cond = jnp.concatenate([c, c_ctx[None, :], jnp.zeros((8 - n_batch - 1, d), F32)], axis=0)
    mods = _modulation(cond, mod_w, mod_b)[:, :n_batch + 1].reshape(depth, n_batch + 1, N_MOD, d)

    wg, wu, wd = ffn_w_gate.astype(BF16), ffn_w_up.astype(BF16), ffn_w_down.astype(BF16)
    ev_in, ev_out, od_in, od_out = ev_w_in.astype(BF16), ev_w_out.astype(BF16), od_w_in.astype(BF16), od_w_out.astype(BF16)
    pw = pool_w.astype(BF16)
    log_decay = jax.nn.log_sigmoid(ret_decay.astype(F32))
    tabs_even = _rope_tables(seq, WIDE_ROW_TILE, True)
    tabs_odd = _rope_tables(seq, WIDE_ROW_TILE, False)

    h = jnp.concatenate([x.reshape(lat_rows, d), ctx.reshape(ctx_rows, d)], axis=0)
    for l in range(depth):
        need_ctx = l < depth - 1
        out_tiles = all_tiles if need_ctx else lat_tiles
        h = _ffn(h, mods, norm_g, wg, wu, wd, l, 0, lat_rows + ctx_rows, wide_grp)
        if l % 2 == 0:
            e = l // 2
            qkv = _proj_in(h, mods, norm_g, ev_in, tabs_even, l, e, True, wide_grp, rope_blk)
            y = _retention(qkv, log_decay[e], n_batch, seq, ctx_len)
            pm = _pool(qkv, pw, pool_scale, e, n_batch, seq, ctx_len)
            h = _out_proj(h, mods, norm_g, y, 0, pm, 0, ev_out, l, e, out_tiles, grp)
        else:
            o = l // 2
            lam_init = 0.8 - 0.6 * math.exp(-0.3 * l)
            qkv = _proj_in(h, mods, norm_g, od_in, tabs_odd, l, o, False, wide_grp, rope_blk)
            lam_vecs = jnp.stack([lam_q1[o], lam_k1[o], lam_q2[o], lam_k2[o]], axis=0)
            gain = jnp.broadcast_to(diff_subln_g[o][:, None], (diff_subln_g.shape[1], LANES))
            y = _diff_attention(qkv, lam_vecs, gain, lam_init, n_batch, seq, ctx_len, need_ctx)
            h = _out_proj(h, mods, norm_g, y, 0, y, 1, od_out, l, o, out_tiles, grp)
        h = _ffn(h, mods, norm_g, wg, wu, wd, l, 1, out_tiles * tm, wide_grp)
    return h[:lat_rows].reshape(n_batch, seq, d)
```
